```python
import math
import jax, jax.numpy as jnp
from jax import lax
import numpy as np

D_MODEL = 2048
BATCH = 4
SEQ = 2048
DEPTH = 4

CHUNK = 64
Q_BLOCK = 128
ROPE_THETA = 10000.0
EPS = 1e-6
NEG = -1e30
D_MIX = D_MODEL
D_FF = 4 * D_MODEL

A_HEADS = 4
A_DQK = D_MIX // 16
A_DV = 2 * A_DQK
A_WIDTH = A_HEADS * A_DV

B_GROUPS = 4
B_CH = D_MIX // 16
B_WIDTH = B_GROUPS * B_CH
B_BLOCK = 128

C_HEADS = 4
C_DH = D_MIX // 16
C_WIDTH = C_HEADS * C_DH
IDX_HEADS = 8
IDX_DH = D_MIX // 32
TOPK_MAX = 256

IN_SIZES = (
    A_HEADS * 2 * A_DQK,
    A_HEADS * 2 * A_DQK,
    A_HEADS * A_DV,
    B_WIDTH,
    B_WIDTH,
    C_WIDTH,
    C_WIDTH,
    C_WIDTH,
    IDX_HEADS * IDX_DH,
    IDX_DH,
    IDX_HEADS,
)
N_IN = sum(IN_SIZES)

kernel_name = "hybrid_diffattn_gmlp_dsa_trunk"


def rms_norm(x, gain=None):
    xf = x.astype(jnp.float32)
    y = xf * lax.rsqrt(jnp.mean(xf * xf, axis=-1, keepdims=True) + EPS)
    if gain is not None:
        y = y * gain.astype(jnp.float32)
    return y.astype(x.dtype)


def rope_tables(seq, dim, dtype):
    inv = 1.0 / (ROPE_THETA ** (jnp.arange(0, dim, 2, dtype=jnp.float32) / dim))
    ang = jnp.arange(seq, dtype=jnp.float32)[:, None] * inv[None, :]
    ang = jnp.concatenate([ang, ang], axis=-1)
    return jnp.cos(ang).astype(dtype), jnp.sin(ang).astype(dtype)


def apply_rope(x, cos, sin):
    shape = (x.shape[1],) + (1,) * (x.ndim - 3) + (x.shape[-1],)
    half = x.shape[-1] // 2
    rot = jnp.concatenate([-x[..., half:], x[..., :half]], axis=-1)
    return x * cos.reshape(shape) + rot * sin.reshape(shape)


def to_blocks(t):
    b, s = t.shape[:2]
    return jnp.moveaxis(t.reshape((b, s // Q_BLOCK, Q_BLOCK) + t.shape[2:]), 1, 0)


def from_blocks(t):
    t = jnp.moveaxis(t, 0, 1)
    return t.reshape((t.shape[0], t.shape[1] * t.shape[2]) + t.shape[3:])


def diff_attention(q, k, v, lam, lambda_init, subln_gain):
    S = q.shape[1]
    scale = q.shape[-1] ** -0.5
    key_chunk = jnp.arange(S) // CHUNK
    starts = jnp.arange(S // Q_BLOCK) * Q_BLOCK

    def one_block(args):
        qb, start = args
        s = jnp.einsum('bqhmd,bshmd->bmhqs', qb, k).astype(jnp.float32) * scale
        q_chunk = (start + jnp.arange(Q_BLOCK)) // CHUNK
        mask = key_chunk[None, :] <= q_chunk[:, None]
        p = jax.nn.softmax(jnp.where(mask, s, NEG), axis=-1)
        a = p[:, 0] - lam * p[:, 1]
        return jnp.einsum('bhqs,bshe->bqhe', a.astype(v.dtype), v)

    o = from_blocks(lax.map(one_block, (to_blocks(q), starts)))
    return rms_norm(o, subln_gain) * (1.0 - lambda_init)


def spatial_gating(u, v, v_gain, ws, bias):
    b, S = u.shape[:2]
    v = rms_norm(v.reshape(b, S, B_GROUPS, B_CH), v_gain.reshape(B_GROUPS, B_CH))
    v = v.reshape(b, S // B_BLOCK, B_BLOCK, B_GROUPS, B_CH)
    pos_chunk = jnp.arange(B_BLOCK) // CHUNK
    mask = pos_chunk[None, :] <= pos_chunk[:, None]
    w = jnp.where(mask[None], ws, 0.0).astype(v.dtype)
    z = jnp.einsum('gij,bnjgc->bnigc', w, v) + bias.T[None, None, :, :, None]
    return u * z.reshape(b, S, B_WIDTH)


def dsa_attention(q, k, v, qi, ki, wi, topk):
    b, S = q.shape[:2]
    scale = q.shape[-1] ** -0.5
    key_chunk = jnp.arange(S) // CHUNK
    starts = jnp.arange(S // Q_BLOCK) * Q_BLOCK
    bidx = jnp.arange(b)[:, None, None]

    def one_block(args):
        qb, qib, wib, start = args
        q_chunk = (start + jnp.arange(Q_BLOCK)) // CHUNK
        mask = key_chunk[None, :] <= q_chunk[:, None]
        iscore = jnp.einsum('bqhd,bsd->bqhs', qib, ki).astype(jnp.float32)
        iscore = jnp.einsum('bqhs,bqh->bqs', jax.nn.relu(iscore), wib.astype(jnp.float32))
        iscore = jnp.where(mask[None], iscore, NEG)
        top_val, top_idx = lax.top_k(iscore, topk)
        valid = top_val > 0.5 * NEG
        kg = k[bidx, top_idx]
        vg = v[bidx, top_idx]
        s = jnp.einsum('bqhd,bqkhd->bhqk', qb, kg).astype(jnp.float32) * scale
        p = jax.nn.softmax(jnp.where(valid[:, None], s, NEG), axis=-1)
        return jnp.einsum('bhqk,bqkhd->bqhd', p.astype(vg.dtype), vg)

    return from_blocks(lax.map(one_block, (to_blocks(q), to_blocks(qi), to_blocks(wi), starts)))


def setup_inputs(seed: int = 0) -> dict:
    key = jax.random.key(seed)
    ks = jax.random.split(key, 17)

    def nrm(k, shape, s):
        return jax.random.normal(k, shape, jnp.float32) * s

    L = DEPTH
    return {
        "x": nrm(ks[0], (BATCH, SEQ, D_MODEL), 1.0),
        "c": nrm(ks[1], (BATCH, D_MODEL), 1.0),
        "w_mod": nrm(ks[2], (L, D_MODEL, 6 * D_MODEL), 0.5 * D_MODEL ** -0.5),
        "b_mod": nrm(ks[3], (L, 6 * D_MODEL), 0.01),
        "w_in": nrm(ks[4], (L, D_MODEL, N_IN), D_MODEL ** -0.5),
        "w_out": nrm(ks[5], (L, D_MIX, D_MODEL), D_MIX ** -0.5),
        "a_qnorm": 1.0 + nrm(ks[6], (L, A_DQK), 0.05),
        "a_knorm": 1.0 + nrm(ks[7], (L, A_DQK), 0.05),
        "a_lambda": nrm(ks[8], (L, 4, A_DQK), 0.1),
        "a_subln": 1.0 + nrm(ks[9], (L, A_DV), 0.05),
        "b_vnorm": 1.0 + nrm(ks[10], (L, B_WIDTH), 0.05),
        "b_ws": nrm(ks[11], (L, B_GROUPS, B_BLOCK, B_BLOCK), B_BLOCK ** -0.5),
        "b_bias": 1.0 + nrm(ks[12], (L, B_GROUPS, B_BLOCK), 0.1),
        "c_qnorm": 1.0 + nrm(ks[13], (L, C_DH), 0.05),
        "c_knorm": 1.0 + nrm(ks[14], (L, C_DH), 0.05),
        "w_ff1": nrm(ks[15], (L, D_MODEL, D_FF), D_MODEL ** -0.5),
        "w_ff2": nrm(ks[16], (L, D_FF, D_MODEL), D_FF ** -0.5),
    }


def reference(x, c, w_mod, b_mod, w_in, w_out, a_qnorm, a_knorm, a_lambda, a_subln,
              b_vnorm, b_ws, b_bias, c_qnorm, c_knorm, w_ff1, w_ff2):
    B, S, _ = x.shape
    topk = min(TOPK_MAX, S // 4)
    cos128, sin128 = rope_tables(S, A_DQK, x.dtype)
    cos64, sin64 = rope_tables(S, IDX_DH, x.dtype)
    offs = []
    acc = 0
    for n in IN_SIZES[:-1]:
        acc += n
        offs.append(acc)
    c_act = jax.nn.silu(c)

    for l in range(DEPTH):
        lambda_init = 0.8 - 0.6 * math.exp(-0.3 * l)
        mod = jnp.einsum('bd,de->be', c_act, w_mod[l]) + b_mod[l]
        sh1, sc1, g1, sh2, sc2, g2 = jnp.split(mod[:, None, :], 6, axis=-1)

        h = rms_norm(x) * (1.0 + sc1) + sh1
        proj = jnp.einsum('bsd,de->bse', h, w_in[l])
        aq, ak, av, bu, bv, cq, ck, cv, iq, ik, iw = jnp.split(proj, offs, axis=-1)

        aq = apply_rope(rms_norm(aq.reshape(B, S, A_HEADS, 2, A_DQK), a_qnorm[l]), cos128, sin128)
        ak = apply_rope(rms_norm(ak.reshape(B, S, A_HEADS, 2, A_DQK), a_knorm[l]), cos128, sin128)
        lam_p = a_lambda[l].astype(jnp.float32)
        lam = (jnp.exp(jnp.sum(lam_p[0] * lam_p[1])) - jnp.exp(jnp.sum(lam_p[2] * lam_p[3]))
               + lambda_init)
        oa = diff_attention(aq, ak, av.reshape(B, S, A_HEADS, A_DV), lam, lambda_init, a_subln[l])

        ob = spatial_gating(jax.nn.gelu(bu), jax.nn.gelu(bv), b_vnorm[l], b_ws[l], b_bias[l])

        cq = apply_rope(rms_norm(cq.reshape(B, S, C_HEADS, C_DH), c_qnorm[l]), cos128, sin128)
        ck = apply_rope(rms_norm(ck.reshape(B, S, C_HEADS, C_DH), c_knorm[l]), cos128, sin128)
        iq = apply_rope(iq.reshape(B, S, IDX_HEADS, IDX_DH), cos64, sin64)
        ik = apply_rope(ik, cos64, sin64)
        iw = iw * (IDX_HEADS ** -0.5 * IDX_DH ** -0.5)
        oc = dsa_attention(cq, ck, cv.reshape(B, S, C_HEADS, C_DH), iq, ik, iw, topk)

        mix = jnp.concatenate([oa.reshape(B, S, A_WIDTH), ob, oc.reshape(B, S, C_WIDTH)], axis=-1)
        x = x + g1 * jnp.einsum('bse,ed->bsd', mix, w_out[l])

        h = rms_norm(x) * (1.0 + sc2) + sh2
        ff = jnp.square(jax.nn.relu(jnp.einsum('bsd,df->bsf', h, w_ff1[l])))
        x = x + g2 * jnp.einsum('bsf,fd->bsd', ff, w_ff2[l])

    return x
```

```python
import functools
import math

import jax
import jax.numpy as jnp
import numpy as np
from jax import lax
from jax.experimental import pallas as pl
from jax.experimental.pallas import tpu as pltpu

D_MODEL = 2048
CHUNK = 64
ROPE_THETA = 10000.0
EPS = 1e-6
NEG = -1e30
D_FF = 4 * D_MODEL
A_HEADS = 4
A_DQK = D_MODEL // 16
A_DV = 2 * A_DQK
A_WIDTH = A_HEADS * A_DV
B_GROUPS = 4
B_CH = D_MODEL // 16
B_WIDTH = B_GROUPS * B_CH
B_BLOCK = 128
C_HEADS = 4
C_DH = D_MODEL // 16
C_WIDTH = C_HEADS * C_DH
IDX_HEADS = 8
IDX_DH = D_MODEL // 32
TOPK_MAX = 256
Q_BLOCK = 128

LANES = 128
VMEM_LIMIT_BYTES = 56 * 1024 * 1024

PROJ_TILE = 512
T_AQ, T_AK, T_AV, T_BU, T_BV, T_CQ, T_CK, T_CV, T_IQ, T_MISC = 0, 2, 4, 6, 7, 8, 9, 10, 11, 12
N_PROJ_TILES = 13
N_IN = 3 * A_WIDTH + 2 * B_WIDTH + 3 * C_WIDTH + IDX_HEADS * IDX_DH + IDX_DH + IDX_HEADS
N_PROJ = N_PROJ_TILES * PROJ_TILE
IW_LANE0 = IDX_DH
IW_SCALE = IDX_HEADS ** -0.5 * IDX_DH ** -0.5

BF16 = jnp.bfloat16
F32 = jnp.float32
NT_DIMS = (((1,), (1,)), ((), ()))


def _cparams(*sem):
    return pltpu.CompilerParams(dimension_semantics=sem, vmem_limit_bytes=VMEM_LIMIT_BYTES)


def _rms(v, gain=None):
    y = v * lax.rsqrt(jnp.mean(v * v, axis=-1, keepdims=True) + EPS)
    return y if gain is None else y * gain


def _mod_kernel(c_ref, w_ref, b_ref, o_ref):
    ca = jax.nn.silu(c_ref[...]).astype(BF16)
    o_ref[...] = jnp.dot(ca, w_ref[...].astype(BF16), preferred_element_type=F32) + b_ref[...]


def _modulation(c, w_mod, b_mod):
    L, D, E = w_mod.shape
    B = c.shape[0]
    tn = 1024
    return pl.pallas_call(
        _mod_kernel,
        grid=(L, E // tn),
        in_specs=[
            pl.BlockSpec((B, D), lambda l, j: (0, 0)),
            pl.BlockSpec((None, D, tn), lambda l, j: (l, 0, j)),
            pl.BlockSpec((None, 1, tn), lambda l, j: (l, 0, j)),
        ],
        out_specs=pl.BlockSpec((None, B, tn), lambda l, j: (l, 0, j)),
        out_shape=jax.ShapeDtypeStruct((L, B, E), F32),
        compiler_params=_cparams("parallel", "parallel"),
        name="modulation",
    )(c, w_mod, b_mod.reshape(L, 1, E))


def _inproj_kernel(x_ref, mod_ref, w_ref, gain_ref, rope_ref, o_ref, h_ref):
    j = pl.program_id(1)

    @pl.when(j == 0)
    def _():
        y = _rms(x_ref[...])
        h_ref[...] = (y * (1.0 + mod_ref[0, 1:2, :]) + mod_ref[0, 0:1, :]).astype(BF16)

    acc = jnp.dot(h_ref[...], w_ref[...], preferred_element_type=F32)
    groups = PROJ_TILE // LANES

    def grp(a, g):
        return a[:, g * LANES:(g + 1) * LANES]

    def rope128(v):
        return v * rope_ref[0] + pltpu.roll(v, LANES // 2, 1) * rope_ref[1]

    def rope64(v):
        return (v * rope_ref[2] + pltpu.roll(v, 96, 1) * rope_ref[3]
                + pltpu.roll(v, 32, 1) * rope_ref[4])

    def emit(fn):
        for g in range(groups):
            o_ref[:, g * LANES:(g + 1) * LANES] = fn(grp(acc, g), g).astype(o_ref.dtype)

    is_norm_rope = (j < T_AV) | (j == T_CQ) | (j == T_CK)
    is_plain = (j == T_AV) | (j == T_AV + 1) | (j == T_CV)

    @pl.when(is_norm_rope)
    def _():
        emit(lambda v, g: rope128(_rms(v, grp(gain_ref[...], g))))

    @pl.when(is_plain)
    def _():
        emit(lambda v, g: v)

    @pl.when(j == T_BU)
    def _():
        emit(lambda v, g: jax.nn.gelu(v))

    @pl.when(j == T_BV)
    def _():
        emit(lambda v, g: _rms(jax.nn.gelu(v), grp(gain_ref[...], g)))

    @pl.when(j == T_IQ)
    def _():
        emit(lambda v, g: rope64(v))

    @pl.when(j == T_MISC)
    def _():
        def misc(v, g):
            if g > 0:
                return v
            lane = lax.broadcasted_iota(jnp.int32, v.shape, 1)
            return jnp.where(lane < IDX_DH, rope64(v), v * IW_SCALE)
        emit(misc)


def _inproj(x2, mod_l, w, gains, rope, S):
    M, D = x2.shape
    tm = min(512, S)
    nrow = S // tm
    return pl.pallas_call(
        _inproj_kernel,
        grid=(M // tm, N_PROJ_TILES),
        in_specs=[
            pl.BlockSpec((tm, D), lambda i, j: (i, 0)),
            pl.BlockSpec((1, 6, D), lambda i, j: (i // nrow, 0, 0)),
            pl.BlockSpec((D, PROJ_TILE), lambda i, j: (0, j)),
            pl.BlockSpec((None, 1, PROJ_TILE), lambda i, j: (j, 0, 0)),
            pl.BlockSpec((5, tm, LANES), lambda i, j: (0, i % nrow, 0)),
        ],
        out_specs=pl.BlockSpec((tm, PROJ_TILE), lambda i, j: (i, j)),
        out_shape=jax.ShapeDtypeStruct((M, N_PROJ), BF16),
        scratch_shapes=[pltpu.VMEM((tm, D), BF16)],
        compiler_params=_cparams("parallel", "arbitrary"),
        name="inproj",
    )(x2, mod_l, w, gains, rope)


def _chunk_mask(row0, rows, cols):
    r = row0 + lax.broadcasted_iota(jnp.int32, (rows, cols), 0)
    c = lax.broadcasted_iota(jnp.int32, (rows, cols), 1)
    return (c // CHUNK) <= (r // CHUNK)


def _diffattn_kernel(lam_ref, subln_ref, q_ref, k_ref, v_ref, o_ref, *, lambda_init):
    tq = q_ref.shape[0]
    S = k_ref.shape[0]
    scale = A_DQK ** -0.5
    lp = lam_ref[...]
    lam = (jnp.exp(jnp.sum(lp[0:1] * lp[1:2], axis=-1, keepdims=True))
           - jnp.exp(jnp.sum(lp[2:3] * lp[3:4], axis=-1, keepdims=True)) + lambda_init)
    mask = _chunk_mask(pl.program_id(2) * tq, tq, S)

    def softmax_parts(m):
        s = lax.dot_general(q_ref[:, m * A_DQK:(m + 1) * A_DQK], k_ref[:, m * A_DQK:(m + 1) * A_DQK],
                            NT_DIMS, preferred_element_type=F32) * scale
        s = jnp.where(mask, s, NEG)
        e = jnp.exp(s - jnp.max(s, axis=-1, keepdims=True))
        return e, jnp.sum(e, axis=-1, keepdims=True)

    e1, l1 = softmax_parts(0)
    e2, l2 = softmax_parts(1)
    a = e1 * (1.0 / l1) - e2 * (lam / l2)
    o = jnp.dot(a.astype(BF16), v_ref[...], preferred_element_type=F32)
    o_ref[...] = (_rms(o, subln_ref[...]) * (1.0 - lambda_init)).astype(o_ref.dtype)


def _diffattn(pb, a_lambda_l, a_subln_l, lambda_init, B, S):
    tq = min(256, S)
    nq = S // tq
    return pl.pallas_call(
        functools.partial(_diffattn_kernel, lambda_init=lambda_init),
        grid=(B, A_HEADS, nq),
        in_specs=[
            pl.BlockSpec((4, A_DQK), lambda b, h, i: (0, 0)),
            pl.BlockSpec((1, A_DV), lambda b, h, i: (0, 0)),
            pl.BlockSpec((tq, A_DV), lambda b, h, i: (b * nq + i, h)),
            pl.BlockSpec((S, A_DV), lambda b, h, i: (b, A_HEADS + h)),
            pl.BlockSpec((S, A_DV), lambda b, h, i: (b, 2 * A_HEADS + h)),
        ],
        out_specs=pl.BlockSpec((tq, A_DV), lambda b, h, i: (b * nq + i, h)),
        out_shape=jax.ShapeDtypeStruct((B * S, A_WIDTH), BF16),
        compiler_params=_cparams("parallel", "parallel", "arbitrary"),
        name="diffattn",
    )(a_lambda_l, a_subln_l.reshape(1, A_DV), pb, pb, pb)


def _gmlp_kernel(u_ref, v_ref, ws_ref, bias_ref, o_ref):
    mask = _chunk_mask(0, B_BLOCK, B_BLOCK)
    for g in range(B_GROUPS):
        w = jnp.where(mask, ws_ref[g], 0.0).astype(BF16)
        z = jnp.dot(w, v_ref[:, g * B_CH:(g + 1) * B_CH], preferred_element_type=F32)
        z = z + bias_ref[:, g:g + 1]
        o_ref[:, g * B_CH:(g + 1) * B_CH] = (
            u_ref[:, g * B_CH:(g + 1) * B_CH].astype(F32) * z).astype(o_ref.dtype)


def _gmlp(pb, ws_l, bias_l):
    M = pb.shape[0]
    return pl.pallas_call(
        _gmlp_kernel,
        grid=(M // B_BLOCK,),
        in_specs=[
            pl.BlockSpec((B_BLOCK, B_WIDTH), lambda i: (i, T_BU)),
            pl.BlockSpec((B_BLOCK, B_WIDTH), lambda i: (i, T_BV)),
            pl.BlockSpec((B_GROUPS, B_BLOCK, B_BLOCK), lambda i: (0, 0, 0)),
            pl.BlockSpec((B_BLOCK, B_GROUPS), lambda i: (0, 0)),
        ],
        out_specs=pl.BlockSpec((B_BLOCK, B_WIDTH), lambda i: (i, 0)),
        out_shape=jax.ShapeDtypeStruct((M, B_WIDTH), BF16),
        compiler_params=_cparams("parallel"),
        name="gmlp",
    )(pb, pb, ws_l, bias_l.T)


def _key_to_float(key):
    return lax.bitcast_convert_type(jnp.where(key < 0, key ^ jnp.int32(0x7FFFFFFF), key), F32)


def _dsa_kernel(q_ref, iq_ref, mq_ref, k_ref, v_ref, mk_ref, o_ref, *, topk):
    tq = q_ref.shape[0]
    S = k_ref.shape[0]
    nchunk = S // LANES
    scale = C_DH ** -0.5
    int_min = jnp.int32(-2 ** 31)

    ik = mk_ref[:, 0:IDX_DH]
    isc = jnp.zeros((tq, S), F32)
    for h in range(IDX_HEADS):
        d = lax.dot_general(iq_ref[:, h * IDX_DH:(h + 1) * IDX_DH], ik, NT_DIMS,
                            preferred_element_type=F32)
        isc = isc + jnp.maximum(d, 0.0) * mq_ref[:, IW_LANE0 + h:IW_LANE0 + h + 1].astype(F32)
    isc = jnp.where(_chunk_mask(pl.program_id(1) * tq, tq, S), isc, NEG)
    chunks = [isc[:, c * LANES:(c + 1) * LANES] for c in range(nchunk)]

    ones = jnp.ones((S, LANES), BF16)

    def count(masks):
        m = jnp.concatenate([jnp.where(mc, 1.0, 0.0).astype(BF16) for mc in masks], axis=1)
        return jnp.dot(m, ones, preferred_element_type=F32)

    def value_step(i, t_u):
        cand_u = t_u | lax.shift_left(jnp.int32(1), jnp.int32(31) - i)
        cand_f = _key_to_float(cand_u ^ int_min)
        ge = count([ch >= cand_f for ch in chunks]) >= topk
        return jnp.where(ge, cand_u, t_u)

    t_u = lax.fori_loop(0, 32, value_step, jnp.zeros((tq, LANES), jnp.int32))
    thr = _key_to_float(t_u ^ int_min)
    gts = [ch > thr for ch in chunks]
    eqs = [ch == thr for ch in chunks]
    need = topk - count(gts)

    lane = lax.broadcasted_iota(jnp.int32, (tq, LANES), 1)
    idxs = [lane + c * LANES for c in range(nchunk)]

    def index_step(i, p):
        cand = p | lax.shift_right_logical(jnp.int32(S // 2), i)
        lt = count([e & (ix < cand) for e, ix in zip(eqs, idxs)]) < need
        return jnp.where(lt, cand, p)

    nbits = int(math.log2(S))
    p_last = lax.fori_loop(0, nbits, index_step, jnp.zeros((tq, LANES), jnp.int32))
    bias = jnp.concatenate(
        [jnp.where((g | (e & (ix <= p_last))) & (ch > 0.5 * NEG), 0.0, NEG)
         for g, e, ix, ch in zip(gts, eqs, idxs, chunks)], axis=1)

    for h in range(C_HEADS):
        sl = slice(h * C_DH, (h + 1) * C_DH)
        s = lax.dot_general(q_ref[:, sl], k_ref[:, sl], NT_DIMS, preferred_element_type=F32)
        s = s * scale + bias
        e = jnp.exp(s - jnp.max(s, axis=-1, keepdims=True))
        p = e * (1.0 / jnp.sum(e, axis=-1, keepdims=True))
        o_ref[:, sl] = jnp.dot(p.astype(BF16), v_ref[:, sl],
                               preferred_element_type=F32).astype(o_ref.dtype)


def _dsa(pb, B, S):
    tq = Q_BLOCK
    nq = S // tq
    topk = min(TOPK_MAX, S // 4)
    qspec = lambda t: pl.BlockSpec((tq, PROJ_TILE), lambda b, i: (b * nq + i, t))
    kspec = lambda t: pl.BlockSpec((S, PROJ_TILE), lambda b, i: (b, t))
    return pl.pallas_call(
        functools.partial(_dsa_kernel, topk=topk),
        grid=(B, nq),
        in_specs=[qspec(T_CQ), qspec(T_IQ), qspec(T_MISC), kspec(T_CK), kspec(T_CV), kspec(T_MISC)],
        out_specs=pl.BlockSpec((tq, C_WIDTH), lambda b, i: (b * nq + i, 0)),
        out_shape=jax.ShapeDtypeStruct((B * S, C_WIDTH), BF16),
        compiler_params=_cparams("parallel", "arbitrary"),
        name="dsa",
    )(pb, pb, pb, pb, pb, pb)


def _outproj_kernel(x_ref, mod_ref, oa_ref, ob_ref, oc_ref, w_ref, o_ref):
    y = jnp.dot(oa_ref[...], w_ref[0:A_WIDTH, :], preferred_element_type=F32)
    y = y + jnp.dot(ob_ref[...], w_ref[A_WIDTH:A_WIDTH + B_WIDTH, :], preferred_element_type=F32)
    y = y + jnp.dot(oc_ref[...], w_ref[A_WIDTH + B_WIDTH:, :], preferred_element_type=F32)
    o_ref[...] = x_ref[...] + mod_ref[0, 2:3, :] * y


def _outproj(x2, mod_l, oa, ob, oc, w, S):
    M, D = x2.shape
    tm = min(512, S)
    nrow = S // tm
    row = lambda width: pl.BlockSpec((tm, width), lambda i: (i, 0))
    return pl.pallas_call(
        _outproj_kernel,
        grid=(M // tm,),
        in_specs=[
            row(D),
            pl.BlockSpec((1, 6, D), lambda i: (i // nrow, 0, 0)),
            row(A_WIDTH), row(B_WIDTH), row(C_WIDTH),
            pl.BlockSpec((D, D), lambda i: (0, 0)),
        ],
        out_specs=row(D),
        out_shape=jax.ShapeDtypeStruct((M, D), F32),
        compiler_params=_cparams("parallel"),
        name="outproj",
    )(x2, mod_l, oa, ob, oc, w)


def _ffn_kernel(x_ref, mod_ref, w1_ref, w2_ref, o_ref, h_ref, acc_ref):
    j = pl.program_id(1)

    @pl.when(j == 0)
    def _():
        y = _rms(x_ref[...])
        h_ref[...] = (y * (1.0 + mod_ref[0, 4:5, :]) + mod_ref[0, 3:4, :]).astype(BF16)
        acc_ref[...] = jnp.zeros_like(acc_ref)

    hid = jnp.square(jnp.maximum(jnp.dot(h_ref[...], w1_ref[...], preferred_element_type=F32), 0.0))
    acc_ref[...] += jnp.dot(hid.astype(BF16), w2_ref[...], preferred_element_type=F32)

    @pl.when(j == pl.num_programs(1) - 1)
    def _():
        o_ref[...] = x_ref[...] + mod_ref[0, 5:6, :] * acc_ref[...]


def _ffn(x2, mod_l, w1, w2, S):
    M, D = x2.shape
    F = w1.shape[1]
    tm = min(512, S)
    tf = 512
    nrow = S // tm
    return pl.pallas_call(
        _ffn_kernel,
        grid=(M // tm, F // tf),
        in_specs=[
            pl.BlockSpec((tm, D), lambda i, j: (i, 0)),
            pl.BlockSpec((1, 6, D), lambda i, j: (i // nrow, 0, 0)),
            pl.BlockSpec((D, tf), lambda i, j: (0, j)),
            pl.BlockSpec((tf, D), lambda i, j: (j, 0)),
        ],
        out_specs=pl.BlockSpec((tm, D), lambda i, j: (i, 0)),
        out_shape=jax.ShapeDtypeStruct((M, D), F32),
        scratch_shapes=[pltpu.VMEM((tm, D), BF16), pltpu.VMEM((tm, D), F32)],
        compiler_params=_cparams("parallel", "arbitrary"),
        name="ffn",
    )(x2, mod_l, w1, w2)


def _rope_tables(S):
    pos = jnp.arange(S, dtype=F32)[:, None]

    def angles(dim):
        inv = 1.0 / (ROPE_THETA ** (jnp.arange(0, dim, 2, dtype=F32) / dim))
        ang = pos * inv[None, :]
        return jnp.concatenate([ang, ang], axis=-1)

    a128 = angles(A_DQK)
    half = jnp.arange(LANES) < LANES // 2
    cos128, sin128 = jnp.cos(a128), jnp.sin(a128)
    a64 = jnp.tile(angles(IDX_DH), (1, 2))
    cos64, sin64 = jnp.cos(a64), jnp.sin(a64)
    low = (jnp.arange(LANES) % IDX_DH) < IDX_DH // 2
    return jnp.stack([
        cos128, jnp.where(half, -sin128, sin128),
        cos64, jnp.where(low, -sin64, 0.0), jnp.where(low, 0.0, sin64),
    ])


def _gain_rows(a_qnorm_l, a_knorm_l, b_vnorm_l, c_qnorm_l, c_knorm_l):
    rep = PROJ_TILE // LANES
    one = jnp.ones((PROJ_TILE,), F32)
    rows = [one] * N_PROJ_TILES
    rows[T_AQ] = rows[T_AQ + 1] = jnp.tile(a_qnorm_l, rep)
    rows[T_AK] = rows[T_AK + 1] = jnp.tile(a_knorm_l, rep)
    rows[T_BV] = b_vnorm_l
    rows[T_CQ] = jnp.tile(c_qnorm_l, rep)
    rows[T_CK] = jnp.tile(c_knorm_l, rep)
    return jnp.stack(rows)[:, None, :]


def kernel(x, c, w_mod, b_mod, w_in, w_out, a_qnorm, a_knorm, a_lambda, a_subln, b_vnorm, b_ws,
           b_bias, c_qnorm, c_knorm, w_ff1, w_ff2):
    B, S, D = x.shape
    L = w_mod.shape[0]
    assert D == D_MODEL and S % Q_BLOCK == 0 and w_in.shape[-1] == N_IN

    rope = _rope_tables(S)
    w_in_b = jnp.pad(w_in.astype(BF16), ((0, 0), (0, 0), (0, N_PROJ - N_IN)))
    w_out_b = w_out.astype(BF16)
    w_ff1_b = w_ff1.astype(BF16)
    w_ff2_b = w_ff2.astype(BF16)

    mod = _modulation(c, w_mod, b_mod).reshape(L, B, 6, D)
    x2 = x.reshape(B * S, D)
    for l in range(L):
        lambda_init = 0.8 - 0.6 * math.exp(-0.3 * l)
        gains = _gain_rows(a_qnorm[l], a_knorm[l], b_vnorm[l], c_qnorm[l], c_knorm[l])
        pb = _inproj(x2, mod[l], w_in_b[l], gains, rope, S)
        oa = _diffattn(pb, a_lambda[l], a_subln[l], lambda_init, B, S)
        ob = _gmlp(pb, b_ws[l], b_bias[l])
        oc = _dsa(pb, B, S)
        x2 = _outproj(x2, mod[l], oa, ob, oc, w_out_b[l], S)
        x2 = _ffn(x2, mod[l], w_ff1_b[l], w_ff2_b[l], S)
    return x2.reshape(B, S, D)
```

```python
import functools
import math

import jax
import jax.numpy as jnp
import numpy as np
from jax import lax
from jax.experimental import pallas as pl
from jax.experimental.pallas import tpu as pltpu

D_MODEL = 2048
CHUNK = 64
ROPE_THETA = 10000.0
EPS = 1e-6
NEG = -1e30
D_FF = 4 * D_MODEL
A_HEADS = 4
A_DQK = D_MODEL // 16
A_DV = 2 * A_DQK
A_WIDTH = A_HEADS * A_DV
B_GROUPS = 4
B_CH = D_MODEL // 16
B_WIDTH = B_GROUPS * B_CH
B_BLOCK = 128
C_HEADS = 4
C_DH = D_MODEL // 16
C_WIDTH = C_HEADS * C_DH
IDX_HEADS = 8
IDX_DH = D_MODEL // 32
TOPK_MAX = 256
Q_BLOCK = 128

LANES = 128
VMEM_LIMIT_BYTES = 56 * 1024 * 1024

PROJ_TILE = 512
T_AQ, T_AK, T_AV, T_BU, T_BV, T_CQ, T_CK, T_CV, T_IQ, T_MISC = 0, 2, 4, 6, 7, 8, 9, 10, 11, 12
N_PROJ_TILES = 13
N_IN = 3 * A_WIDTH + 2 * B_WIDTH + 3 * C_WIDTH + IDX_HEADS * IDX_DH + IDX_DH + IDX_HEADS
N_PROJ = N_PROJ_TILES * PROJ_TILE
IW_LANE0 = IDX_DH
IW_SCALE = IDX_HEADS ** -0.5 * IDX_DH ** -0.5

BF16 = jnp.bfloat16
F32 = jnp.float32
NT_DIMS = (((1,), (1,)), ((), ()))


def _cparams(*sem):
    return pltpu.CompilerParams(dimension_semantics=sem, vmem_limit_bytes=VMEM_LIMIT_BYTES)


def _rms(v, gain=None):
    y = v * lax.rsqrt(jnp.mean(v * v, axis=-1, keepdims=True) + EPS)
    return y if gain is None else y * gain


def _mod_kernel(c_ref, w_ref, b_ref, o_ref):
    ca = jax.nn.silu(c_ref[...]).astype(BF16)
    o_ref[...] = jnp.dot(ca, w_ref[...].astype(BF16), preferred_element_type=F32) + b_ref[...]


def _modulation(c, w_mod, b_mod):
    L, D, E = w_mod.shape
    B = c.shape[0]
    tn = 1024
    return pl.pallas_call(
        _mod_kernel,
        grid=(L, E // tn),
        in_specs=[
            pl.BlockSpec((B, D), lambda l, j: (0, 0)),
            pl.BlockSpec((None, D, tn), lambda l, j: (l, 0, j)),
            pl.BlockSpec((None, 1, tn), lambda l, j: (l, 0, j)),
        ],
        out_specs=pl.BlockSpec((None, B, tn), lambda l, j: (l, 0, j)),
        out_shape=jax.ShapeDtypeStruct((L, B, E), F32),
        compiler_params=_cparams("parallel", "parallel"),
        name="modulation",
    )(c, w_mod, b_mod.reshape(L, 1, E))


def _inproj_kernel(x_ref, mod_ref, w_ref, gain_ref, rope_ref, o_ref, h_ref):
    j = pl.program_id(1)

    @pl.when(j == 0)
    def _():
        y = _rms(x_ref[...])
        h_ref[...] = (y * (1.0 + mod_ref[0, 1:2, :]) + mod_ref[0, 0:1, :]).astype(BF16)

    acc = jnp.dot(h_ref[...], w_ref[...], preferred_element_type=F32)
    groups = PROJ_TILE // LANES

    def grp(a, g):
        return a[:, g * LANES:(g + 1) * LANES]

    def rope128(v):
        return v * rope_ref[0] + pltpu.roll(v, LANES // 2, 1) * rope_ref[1]

    def rope64(v):
        return (v * rope_ref[2] + pltpu.roll(v, 96, 1) * rope_ref[3]
                + pltpu.roll(v, 32, 1) * rope_ref[4])

    def emit(fn):
        for g in range(groups):
            o_ref[:, g * LANES:(g + 1) * LANES] = fn(grp(acc, g), g).astype(o_ref.dtype)

    is_norm_rope = (j < T_AV) | (j == T_CQ) | (j == T_CK)
    is_plain = (j == T_AV) | (j == T_AV + 1) | (j == T_CV)

    @pl.when(is_norm_rope)
    def _():
        emit(lambda v, g: rope128(_rms(v, grp(gain_ref[...], g))))

    @pl.when(is_plain)
    def _():
        emit(lambda v, g: v)

    @pl.when(j == T_BU)
    def _():
        emit(lambda v, g: jax.nn.gelu(v))

    @pl.when(j == T_BV)
    def _():
        emit(lambda v, g: _rms(jax.nn.gelu(v), grp(gain_ref[...], g)))

    @pl.when(j == T_IQ)
    def _():
        emit(lambda v, g: rope64(v))

    @pl.when(j == T_MISC)
    def _():
        def misc(v, g):
            if g > 0:
                return v
            lane = lax.broadcasted_iota(jnp.int32, v.shape, 1)
            return jnp.where(lane < IDX_DH, rope64(v), v * IW_SCALE)
        emit(misc)


def _inproj(x2, mod_l, w, gains, rope, S):
    M, D = x2.shape
    tm = min(512, S)
    nrow = S // tm
    return pl.pallas_call(
        _inproj_kernel,
        grid=(M // tm, N_PROJ_TILES),
        in_specs=[
            pl.BlockSpec((tm, D), lambda i, j: (i, 0)),
            pl.BlockSpec((1, 6, D), lambda i, j: (i // nrow, 0, 0)),
            pl.BlockSpec((D, PROJ_TILE), lambda i, j: (0, j)),
            pl.BlockSpec((None, 1, PROJ_TILE), lambda i, j: (j, 0, 0)),
            pl.BlockSpec((5, tm, LANES), lambda i, j: (0, i % nrow, 0)),
        ],
        out_specs=pl.BlockSpec((tm, PROJ_TILE), lambda i, j: (i, j)),
        out_shape=jax.ShapeDtypeStruct((M, N_PROJ), BF16),
        scratch_shapes=[pltpu.VMEM((tm, D), BF16)],
        compiler_params=_cparams("parallel", "arbitrary"),
        name="inproj",
    )(x2, mod_l, w, gains, rope)


def _chunk_mask(row0, rows, cols):
    r = row0 + lax.broadcasted_iota(jnp.int32, (rows, cols), 0)
    c = lax.broadcasted_iota(jnp.int32, (rows, cols), 1)
    return (c // CHUNK) <= (r // CHUNK)


def _diffattn_kernel(lam_ref, subln_ref, q_ref, k_ref, v_ref, o_ref, *, lambda_init, widths):
    tq = q_ref.shape[0]
    scale = A_DQK ** -0.5
    step = pl.program_id(2)
    lp = lam_ref[...]
    lam = (jnp.exp(jnp.sum(lp[0:1] * lp[1:2], axis=-1, keepdims=True))
           - jnp.exp(jnp.sum(lp[2:3] * lp[3:4], axis=-1, keepdims=True)) + lambda_init)

    def body(W):
        mask = _chunk_mask(step * tq, tq, W)

        def softmax_parts(m):
            s = lax.dot_general(q_ref[:, m * A_DQK:(m + 1) * A_DQK],
                                k_ref[0:W, m * A_DQK:(m + 1) * A_DQK],
                                NT_DIMS, preferred_element_type=F32) * scale
            s = jnp.where(mask, s, NEG)
            e = jnp.exp(s - jnp.max(s, axis=-1, keepdims=True))
            return e, jnp.sum(e, axis=-1, keepdims=True)

        e1, l1 = softmax_parts(0)
        e2, l2 = softmax_parts(1)
        a = e1 * (1.0 / l1) - e2 * (lam / l2)
        o = jnp.dot(a.astype(BF16), v_ref[0:W, :], preferred_element_type=F32)
        o_ref[...] = (_rms(o, subln_ref[...]) * (1.0 - lambda_init)).astype(o_ref.dtype)

    steps_per_class = pl.num_programs(2) // len(widths)
    for ci, W in enumerate(widths):
        pl.when(step // steps_per_class == ci)(functools.partial(body, W))


def _causal_widths(S, nq):
    n_classes = min(4, nq)
    widths = tuple(S * (ci + 1) // n_classes for ci in range(n_classes))
    assert nq % n_classes == 0 and all(w % LANES == 0 for w in widths)
    return widths


def _diffattn(pb, a_lambda_l, a_subln_l, lambda_init, B, S):
    tq = min(256, S)
    nq = S // tq
    return pl.pallas_call(
        functools.partial(_diffattn_kernel, lambda_init=lambda_init, widths=_causal_widths(S, nq)),
        grid=(B, A_HEADS, nq),
        in_specs=[
            pl.BlockSpec((4, A_DQK), lambda b, h, i: (0, 0)),
            pl.BlockSpec((1, A_DV), lambda b, h, i: (0, 0)),
            pl.BlockSpec((tq, A_DV), lambda b, h, i: (b * nq + i, h)),
            pl.BlockSpec((S, A_DV), lambda b, h, i: (b, A_HEADS + h)),
            pl.BlockSpec((S, A_DV), lambda b, h, i: (b, 2 * A_HEADS + h)),
        ],
        out_specs=pl.BlockSpec((tq, A_DV), lambda b, h, i: (b * nq + i, h)),
        out_shape=jax.ShapeDtypeStruct((B * S, A_WIDTH), BF16),
        compiler_params=_cparams("parallel", "parallel", "arbitrary"),
        name="diffattn",
    )(a_lambda_l, a_subln_l.reshape(1, A_DV), pb, pb, pb)


def _gmlp_kernel(u_ref, v_ref, ws_ref, bias_ref, o_ref):
    mask = _chunk_mask(0, B_BLOCK, B_BLOCK)
    for g in range(B_GROUPS):
        w = jnp.where(mask, ws_ref[g], 0.0).astype(BF16)
        z = jnp.dot(w, v_ref[:, g * B_CH:(g + 1) * B_CH], preferred_element_type=F32)
        z = z + bias_ref[:, g:g + 1]
        o_ref[:, g * B_CH:(g + 1) * B_CH] = (
            u_ref[:, g * B_CH:(g + 1) * B_CH].astype(F32) * z).astype(o_ref.dtype)


def _gmlp(pb, ws_l, bias_l):
    M = pb.shape[0]
    return pl.pallas_call(
        _gmlp_kernel,
        grid=(M // B_BLOCK,),
        in_specs=[
            pl.BlockSpec((B_BLOCK, B_WIDTH), lambda i: (i, T_BU)),
            pl.BlockSpec((B_BLOCK, B_WIDTH), lambda i: (i, T_BV)),
            pl.BlockSpec((B_GROUPS, B_BLOCK, B_BLOCK), lambda i: (0, 0, 0)),
            pl.BlockSpec((B_BLOCK, B_GROUPS), lambda i: (0, 0)),
        ],
        out_specs=pl.BlockSpec((B_BLOCK, B_WIDTH), lambda i: (i, 0)),
        out_shape=jax.ShapeDtypeStruct((M, B_WIDTH), BF16),
        compiler_params=_cparams("parallel"),
        name="gmlp",
    )(pb, pb, ws_l, bias_l.T)


def _key_to_float(key):
    return lax.bitcast_convert_type(jnp.where(key < 0, key ^ jnp.int32(0x7FFFFFFF), key), F32)


def _dsa_kernel(q_ref, iq_ref, mq_ref, k_ref, v_ref, mk_ref, o_ref, isc_ref, p_ref, *, topk, widths):
    tq = q_ref.shape[0]
    S = k_ref.shape[0]
    scale = C_DH ** -0.5
    int_min = jnp.int32(-2 ** 31)
    step = pl.program_id(1)
    ones = jnp.ones((LANES, LANES), BF16)
    lane = lax.broadcasted_iota(jnp.int32, (tq, LANES), 1)

    def lane_total(acc):
        return jnp.dot(acc.astype(BF16), ones, preferred_element_type=F32)

    def body(W):
        nchunk = W // LANES
        chunk = lambda c: isc_ref[:, c * LANES:(c + 1) * LANES]

        ik = mk_ref[0:W, 0:IDX_DH]
        isc = jnp.zeros((tq, W), F32)
        for h in range(IDX_HEADS):
            d = lax.dot_general(iq_ref[:, h * IDX_DH:(h + 1) * IDX_DH], ik, NT_DIMS,
                                preferred_element_type=F32)
            isc = isc + jnp.maximum(d, 0.0) * mq_ref[:, IW_LANE0 + h:IW_LANE0 + h + 1].astype(F32)
        isc_ref[:, 0:W] = jnp.where(_chunk_mask(step * tq, tq, W), isc, NEG)

        def count(pred):
            acc = jnp.zeros((tq, LANES), F32)
            for c in range(nchunk):
                acc = acc + jnp.where(pred(chunk(c), c), 1.0, 0.0)
            return lane_total(acc)

        def value_step(i, t_u):
            cand_u = t_u | lax.shift_left(jnp.int32(1), jnp.int32(31) - i)
            cand_f = _key_to_float(cand_u ^ int_min)
            cnt = count(lambda ch, c: ch >= cand_f) + jnp.where(cand_f <= NEG, float(S - W), 0.0)
            return jnp.where(cnt >= topk, cand_u, t_u)

        t_u = lax.fori_loop(0, 32, value_step, jnp.zeros((tq, LANES), jnp.int32))
        thr = _key_to_float(t_u ^ int_min)
        need = topk - count(lambda ch, c: ch > thr)
        n_tie = count(lambda ch, c: ch == thr)

        p_ref[...] = jnp.full((tq, LANES), S - 1, jnp.int32)

        @pl.when(jnp.max(jnp.where(n_tie > need, 1.0, 0.0)) > 0.0)
        def _():
            def index_step(i, p):
                cand = p | lax.shift_right_logical(jnp.int32(S // 2), i)
                lt = count(lambda ch, c: (ch == thr) & (lane + c * LANES < cand)) < need
                return jnp.where(lt, cand, p)
            p_ref[...] = lax.fori_loop(0, int(math.log2(S)), index_step,
                                       jnp.zeros((tq, LANES), jnp.int32))

        p_last = p_ref[...]
        for c in range(nchunk):
            ch = chunk(c)
            sel = ((ch > thr) | ((ch == thr) & (lane + c * LANES <= p_last))) & (ch > 0.5 * NEG)
            isc_ref[:, c * LANES:(c + 1) * LANES] = jnp.where(sel, 0.0, NEG)

        for h in range(C_HEADS):
            sl = slice(h * C_DH, (h + 1) * C_DH)
            s = lax.dot_general(q_ref[:, sl], k_ref[0:W, sl], NT_DIMS, preferred_element_type=F32)
            s = s * scale + isc_ref[:, 0:W]
            e = jnp.exp(s - jnp.max(s, axis=-1, keepdims=True))
            p = e * (1.0 / jnp.sum(e, axis=-1, keepdims=True))
            o_ref[:, sl] = jnp.dot(p.astype(BF16), v_ref[0:W, sl],
                                   preferred_element_type=F32).astype(o_ref.dtype)

    steps_per_class = pl.num_programs(1) // len(widths)
    for ci, W in enumerate(widths):
        pl.when(step // steps_per_class == ci)(functools.partial(body, W))


def _dsa(pb, B, S):
    tq = min(256, S)
    nq = S // tq
    topk = min(TOPK_MAX, S // 4)
    widths = _causal_widths(S, nq)
    qspec = lambda t: pl.BlockSpec((tq, PROJ_TILE), lambda b, i: (b * nq + i, t))
    kspec = lambda t: pl.BlockSpec((S, PROJ_TILE), lambda b, i: (b, t))
    return pl.pallas_call(
        functools.partial(_dsa_kernel, topk=topk, widths=widths),
        grid=(B, nq),
        in_specs=[qspec(T_CQ), qspec(T_IQ), qspec(T_MISC), kspec(T_CK), kspec(T_CV), kspec(T_MISC)],
        out_specs=pl.BlockSpec((tq, C_WIDTH), lambda b, i: (b * nq + i, 0)),
        out_shape=jax.ShapeDtypeStruct((B * S, C_WIDTH), BF16),
        scratch_shapes=[pltpu.VMEM((tq, S), F32), pltpu.VMEM((tq, LANES), jnp.int32)],
        compiler_params=_cparams("parallel", "arbitrary"),
        name="dsa",
    )(pb, pb, pb, pb, pb, pb)


def _outproj_kernel(x_ref, mod_ref, oa_ref, ob_ref, oc_ref, w_ref, o_ref):
    y = jnp.dot(oa_ref[...], w_ref[0:A_WIDTH, :], preferred_element_type=F32)
    y = y + jnp.dot(ob_ref[...], w_ref[A_WIDTH:A_WIDTH + B_WIDTH, :], preferred_element_type=F32)
    y = y + jnp.dot(oc_ref[...], w_ref[A_WIDTH + B_WIDTH:, :], preferred_element_type=F32)
    o_ref[...] = x_ref[...] + mod_ref[0, 2:3, :] * y


def _outproj(x2, mod_l, oa, ob, oc, w, S):
    M, D = x2.shape
    tm = min(512, S)
    nrow = S // tm
    row = lambda width: pl.BlockSpec((tm, width), lambda i: (i, 0))
    return pl.pallas_call(
        _outproj_kernel,
        grid=(M // tm,),
        in_specs=[
            row(D),
            pl.BlockSpec((1, 6, D), lambda i: (i // nrow, 0, 0)),
            row(A_WIDTH), row(B_WIDTH), row(C_WIDTH),
            pl.BlockSpec((D, D), lambda i: (0, 0)),
        ],
        out_specs=row(D),
        out_shape=jax.ShapeDtypeStruct((M, D), F32),
        compiler_params=_cparams("parallel"),
        name="outproj",
    )(x2, mod_l, oa, ob, oc, w)


def _ffn_kernel(x_ref, mod_ref, w1_ref, w2_ref, o_ref, h_ref, acc_ref):
    j = pl.program_id(1)

    @pl.when(j == 0)
    def _():
        y = _rms(x_ref[...])
        h_ref[...] = (y * (1.0 + mod_ref[0, 4:5, :]) + mod_ref[0, 3:4, :]).astype(BF16)
        acc_ref[...] = jnp.zeros_like(acc_ref)

    hid = jnp.square(jnp.maximum(jnp.dot(h_ref[...], w1_ref[...], preferred_element_type=F32), 0.0))
    acc_ref[...] += jnp.dot(hid.astype(BF16), w2_ref[...], preferred_element_type=F32)

    @pl.when(j == pl.num_programs(1) - 1)
    def _():
        o_ref[...] = x_ref[...] + mod_ref[0, 5:6, :] * acc_ref[...]


def _ffn(x2, mod_l, w1, w2, S):
    M, D = x2.shape
    F = w1.shape[1]
    tm = min(512, S)
    tf = 512
    nrow = S // tm
    return pl.pallas_call(
        _ffn_kernel,
        grid=(M // tm, F // tf),
        in_specs=[
            pl.BlockSpec((tm, D), lambda i, j: (i, 0)),
            pl.BlockSpec((1, 6, D), lambda i, j: (i // nrow, 0, 0)),
            pl.BlockSpec((D, tf), lambda i, j: (0, j)),
            pl.BlockSpec((tf, D), lambda i, j: (j, 0)),
        ],
        out_specs=pl.BlockSpec((tm, D), lambda i, j: (i, 0)),
        out_shape=jax.ShapeDtypeStruct((M, D), F32),
        scratch_shapes=[pltpu.VMEM((tm, D), BF16), pltpu.VMEM((tm, D), F32)],
        compiler_params=_cparams("parallel", "arbitrary"),
        name="ffn",
    )(x2, mod_l, w1, w2)


def _rope_tables(S):
    pos = jnp.arange(S, dtype=F32)[:, None]

    def angles(dim):
        inv = 1.0 / (ROPE_THETA ** (jnp.arange(0, dim, 2, dtype=F32) / dim))
        ang = pos * inv[None, :]
        return jnp.concatenate([ang, ang], axis=-1)

    a128 = angles(A_DQK)
    half = jnp.arange(LANES) < LANES // 2
    cos128, sin128 = jnp.cos(a128), jnp.sin(a128)
    a64 = jnp.tile(angles(IDX_DH), (1, 2))
    cos64, sin64 = jnp.cos(a64), jnp.sin(a64)
    low = (jnp.arange(LANES) % IDX_DH) < IDX_DH // 2
    return jnp.stack([
        cos128, jnp.where(half, -sin128, sin128),
        cos64, jnp.where(low, -sin64, 0.0), jnp.where(low, 0.0, sin64),
    ])


def _gain_rows(a_qnorm_l, a_knorm_l, b_vnorm_l, c_qnorm_l, c_knorm_l):
    rep = PROJ_TILE // LANES
    one = jnp.ones((PROJ_TILE,), F32)
    rows = [one] * N_PROJ_TILES
    rows[T_AQ] = rows[T_AQ + 1] = jnp.tile(a_qnorm_l, rep)
    rows[T_AK] = rows[T_AK + 1] = jnp.tile(a_knorm_l, rep)
    rows[T_BV] = b_vnorm_l
    rows[T_CQ] = jnp.tile(c_qnorm_l, rep)
    rows[T_CK] = jnp.tile(c_knorm_l, rep)
    return jnp.stack(rows)[:, None, :]


def kernel(x, c, w_mod, b_mod, w_in, w_out, a_qnorm, a_knorm, a_lambda, a_subln, b_vnorm, b_ws,
           b_bias, c_qnorm, c_knorm, w_ff1, w_ff2):
    B, S, D = x.shape
    L = w_mod.shape[0]
    assert D == D_MODEL and S % Q_BLOCK == 0 and w_in.shape[-1] == N_IN

    rope = _rope_tables(S)
    w_in_b = jnp.pad(w_in.astype(BF16), ((0, 0), (0, 0), (0, N_PROJ - N_IN)))
    w_out_b = w_out.astype(BF16)
    w_ff1_b = w_ff1.astype(BF16)
    w_ff2_b = w_ff2.astype(BF16)

    mod = _modulation(c, w_mod, b_mod).reshape(L, B, 6, D)
    x2 = x.reshape(B * S, D)
    for l in range(L):
        lambda_init = 0.8 - 0.6 * math.exp(-0.3 * l)
        gains = _gain_rows(a_qnorm[l], a_knorm[l], b_vnorm[l], c_qnorm[l], c_knorm[l])
        pb = _inproj(x2, mod[l], w_in_b[l], gains, rope, S)
        oa = _diffattn(pb, a_lambda[l], a_subln[l], lambda_init, B, S)
        ob = _gmlp(pb, b_ws[l], b_bias[l])
        oc = _dsa(pb, B, S)
        x2 = _outproj(x2, mod[l], oa, ob, oc, w_out_b[l], S)
        x2 = _ffn(x2, mod[l], w_ff1_b[l], w_ff2_b[l], S)
    return x2.reshape(B, S, D)
```

```python
import functools
import math

import jax
import jax.numpy as jnp
import numpy as np
from jax import lax
from jax.experimental import pallas as pl
from jax.experimental.pallas import tpu as pltpu

D_MODEL = 2048
CHUNK = 64
ROPE_THETA = 10000.0
EPS = 1e-6
NEG = -1e30
D_FF = 4 * D_MODEL
A_HEADS = 4
A_DQK = D_MODEL // 16
A_DV = 2 * A_DQK
A_WIDTH = A_HEADS * A_DV
B_GROUPS = 4
B_CH = D_MODEL // 16
B_WIDTH = B_GROUPS * B_CH
B_BLOCK = 128
C_HEADS = 4
C_DH = D_MODEL // 16
C_WIDTH = C_HEADS * C_DH
IDX_HEADS = 8
IDX_DH = D_MODEL // 32
TOPK_MAX = 256
Q_BLOCK = 128

LANES = 128
VMEM_LIMIT_BYTES = 56 * 1024 * 1024

PROJ_TILE = 512
T_AQ, T_AK, T_AV, T_BU, T_BV, T_CQ, T_CK, T_CV, T_IQ, T_MISC = 0, 2, 4, 6, 7, 8, 9, 10, 11, 12
N_PROJ_TILES = 13
N_IN = 3 * A_WIDTH + 2 * B_WIDTH + 3 * C_WIDTH + IDX_HEADS * IDX_DH + IDX_DH + IDX_HEADS
N_PROJ = N_PROJ_TILES * PROJ_TILE
IW_LANE0 = IDX_DH
IW_SCALE = IDX_HEADS ** -0.5 * IDX_DH ** -0.5

BF16 = jnp.bfloat16
F32 = jnp.float32
NT_DIMS = (((1,), (1,)), ((), ()))


def _cparams(*sem):
    return pltpu.CompilerParams(dimension_semantics=sem, vmem_limit_bytes=VMEM_LIMIT_BYTES)


def _rms(v, gain=None):
    y = v * lax.rsqrt(jnp.mean(v * v, axis=-1, keepdims=True) + EPS)
    return y if gain is None else y * gain


def _mod_kernel(c_ref, w_ref, b_ref, o_ref):
    ca = jax.nn.silu(c_ref[...]).astype(BF16)
    o_ref[...] = jnp.dot(ca, w_ref[...].astype(BF16), preferred_element_type=F32) + b_ref[...]


def _modulation(c, w_mod, b_mod):
    L, D, E = w_mod.shape
    B = c.shape[0]
    tn = 1024
    return pl.pallas_call(
        _mod_kernel,
        grid=(L, E // tn),
        in_specs=[
            pl.BlockSpec((B, D), lambda l, j: (0, 0)),
            pl.BlockSpec((None, D, tn), lambda l, j: (l, 0, j)),
            pl.BlockSpec((None, 1, tn), lambda l, j: (l, 0, j)),
        ],
        out_specs=pl.BlockSpec((None, B, tn), lambda l, j: (l, 0, j)),
        out_shape=jax.ShapeDtypeStruct((L, B, E), F32),
        compiler_params=_cparams("parallel", "parallel"),
        name="modulation",
    )(c, w_mod, b_mod.reshape(L, 1, E))


def _inproj_kernel(x_ref, mod_ref, w_ref, gain_ref, rope_ref, o_ref, h_ref):
    y = _rms(x_ref[...])
    h_ref[...] = (y * (1.0 + mod_ref[0, 1:2, :]) + mod_ref[0, 0:1, :]).astype(BF16)
    groups = PROJ_TILE // LANES

    def rope128(v):
        return v * rope_ref[0] + pltpu.roll(v, LANES // 2, 1) * rope_ref[1]

    def rope64(v):
        return (v * rope_ref[2] + pltpu.roll(v, 96, 1) * rope_ref[3]
                + pltpu.roll(v, 32, 1) * rope_ref[4])

    def misc(v, g):
        if g > 0:
            return v
        lane = lax.broadcasted_iota(jnp.int32, v.shape, 1)
        return jnp.where(lane < IDX_DH, rope64(v), v * IW_SCALE)

    def gain(t, g):
        return gain_ref[t, :, g * LANES:(g + 1) * LANES]

    norm_rope = lambda v, t, g: rope128(_rms(v, gain(t, g)))
    epilogue = {t: (lambda v, t, g: v) for t in range(N_PROJ_TILES)}
    epilogue.update({t: norm_rope for t in (T_AQ, T_AQ + 1, T_AK, T_AK + 1, T_CQ, T_CK)})
    epilogue[T_BU] = lambda v, t, g: jax.nn.gelu(v)
    epilogue[T_BV] = lambda v, t, g: _rms(jax.nn.gelu(v), gain(t, g))
    epilogue[T_IQ] = lambda v, t, g: rope64(v)
    epilogue[T_MISC] = lambda v, t, g: misc(v, g)

    for t in range(N_PROJ_TILES):
        acc = jnp.dot(h_ref[...], w_ref[:, t * PROJ_TILE:(t + 1) * PROJ_TILE],
                      preferred_element_type=F32)
        for g in range(groups):
            lo = t * PROJ_TILE + g * LANES
            o_ref[:, lo:lo + LANES] = epilogue[t](
                acc[:, g * LANES:(g + 1) * LANES], t, g).astype(o_ref.dtype)


def _inproj(x2, mod_l, w_all, layer, gains, rope, S):
    M, D = x2.shape
    tm = min(256, S)
    nrow = S // tm
    return pl.pallas_call(
        _inproj_kernel,
        grid=(M // tm,),
        in_specs=[
            pl.BlockSpec((tm, D), lambda i: (i, 0)),
            pl.BlockSpec((1, 6, D), lambda i: (i // nrow, 0, 0)),
            pl.BlockSpec((None, D, N_PROJ), lambda i: (layer, 0, 0), pipeline_mode=pl.Buffered(1)),
            pl.BlockSpec((N_PROJ_TILES, 1, PROJ_TILE), lambda i: (0, 0, 0)),
            pl.BlockSpec((5, tm, LANES), lambda i: (0, i % nrow, 0)),
        ],
        out_specs=pl.BlockSpec((tm, N_PROJ), lambda i: (i, 0)),
        out_shape=jax.ShapeDtypeStruct((M, N_PROJ), BF16),
        scratch_shapes=[pltpu.VMEM((tm, D), BF16)],
        compiler_params=_cparams("parallel"),
        name="inproj",
    )(x2, mod_l, w_all, gains, rope)


def _chunk_mask(row0, rows, cols):
    r = row0 + lax.broadcasted_iota(jnp.int32, (rows, cols), 0)
    c = lax.broadcasted_iota(jnp.int32, (rows, cols), 1)
    return (c // CHUNK) <= (r // CHUNK)


def _diffattn_kernel(lam_ref, subln_ref, q_ref, k_ref, v_ref, o_ref, *, lambda_init, widths):
    tq = q_ref.shape[0]
    scale = A_DQK ** -0.5
    step = pl.program_id(2)
    lp = lam_ref[...]
    lam = (jnp.exp(jnp.sum(lp[0:1] * lp[1:2], axis=-1, keepdims=True))
           - jnp.exp(jnp.sum(lp[2:3] * lp[3:4], axis=-1, keepdims=True)) + lambda_init)

    def body(W):
        mask = _chunk_mask(step * tq, tq, W)

        def softmax_parts(m):
            s = lax.dot_general(q_ref[:, m * A_DQK:(m + 1) * A_DQK],
                                k_ref[0:W, m * A_DQK:(m + 1) * A_DQK],
                                NT_DIMS, preferred_element_type=F32) * scale
            s = jnp.where(mask, s, NEG)
            e = jnp.exp(s - jnp.max(s, axis=-1, keepdims=True))
            return e, jnp.sum(e, axis=-1, keepdims=True)

        e1, l1 = softmax_parts(0)
        e2, l2 = softmax_parts(1)
        a = e1 * (1.0 / l1) - e2 * (lam / l2)
        o = jnp.dot(a.astype(BF16), v_ref[0:W, :], preferred_element_type=F32)
        o_ref[...] = (_rms(o, subln_ref[...]) * (1.0 - lambda_init)).astype(o_ref.dtype)

    steps_per_class = pl.num_programs(2) // len(widths)
    for ci, W in enumerate(widths):
        pl.when(step // steps_per_class == ci)(functools.partial(body, W))


def _causal_widths(S, nq):
    n_classes = min(4, nq)
    widths = tuple(S * (ci + 1) // n_classes for ci in range(n_classes))
    assert nq % n_classes == 0 and all(w % LANES == 0 for w in widths)
    return widths


def _diffattn(pb, a_lambda_l, a_subln_l, lambda_init, B, S):
    tq = min(256, S)
    nq = S // tq
    return pl.pallas_call(
        functools.partial(_diffattn_kernel, lambda_init=lambda_init, widths=_causal_widths(S, nq)),
        grid=(B, A_HEADS, nq),
        in_specs=[
            pl.BlockSpec((4, A_DQK), lambda b, h, i: (0, 0)),
            pl.BlockSpec((1, A_DV), lambda b, h, i: (0, 0)),
            pl.BlockSpec((tq, A_DV), lambda b, h, i: (b * nq + i, h)),
            pl.BlockSpec((S, A_DV), lambda b, h, i: (b, A_HEADS + h)),
            pl.BlockSpec((S, A_DV), lambda b, h, i: (b, 2 * A_HEADS + h)),
        ],
        out_specs=pl.BlockSpec((tq, A_DV), lambda b, h, i: (b * nq + i, h)),
        out_shape=jax.ShapeDtypeStruct((B * S, A_WIDTH), BF16),
        compiler_params=_cparams("parallel", "parallel", "arbitrary"),
        name="diffattn",
    )(a_lambda_l, a_subln_l.reshape(1, A_DV), pb, pb, pb)


def _gmlp_kernel(u_ref, v_ref, ws_ref, bias_ref, o_ref):
    mask = _chunk_mask(0, B_BLOCK, B_BLOCK)
    for g in range(B_GROUPS):
        w = jnp.where(mask, ws_ref[g], 0.0).astype(BF16)
        z = jnp.dot(w, v_ref[:, g * B_CH:(g + 1) * B_CH], preferred_element_type=F32)
        z = z + bias_ref[:, g:g + 1]
        o_ref[:, g * B_CH:(g + 1) * B_CH] = (
            u_ref[:, g * B_CH:(g + 1) * B_CH].astype(F32) * z).astype(o_ref.dtype)


def _gmlp(pb, ws_l, bias_l):
    M = pb.shape[0]
    return pl.pallas_call(
        _gmlp_kernel,
        grid=(M // B_BLOCK,),
        in_specs=[
            pl.BlockSpec((B_BLOCK, B_WIDTH), lambda i: (i, T_BU)),
            pl.BlockSpec((B_BLOCK, B_WIDTH), lambda i: (i, T_BV)),
            pl.BlockSpec((B_GROUPS, B_BLOCK, B_BLOCK), lambda i: (0, 0, 0)),
            pl.BlockSpec((B_BLOCK, B_GROUPS), lambda i: (0, 0)),
        ],
        out_specs=pl.BlockSpec((B_BLOCK, B_WIDTH), lambda i: (i, 0)),
        out_shape=jax.ShapeDtypeStruct((M, B_WIDTH), BF16),
        compiler_params=_cparams("parallel"),
        name="gmlp",
    )(pb, pb, ws_l, bias_l.T)


def _key_to_float(key):
    return lax.bitcast_convert_type(jnp.where(key < 0, key ^ jnp.int32(0x7FFFFFFF), key), F32)


def _dsa_kernel(q_ref, iq_ref, mq_ref, k_ref, v_ref, mk_ref, o_ref, isc_ref, p_ref, *, topk, widths):
    tq = q_ref.shape[0]
    S = k_ref.shape[0]
    scale = C_DH ** -0.5
    int_min = jnp.int32(-2 ** 31)
    step = pl.program_id(1)
    ones = jnp.ones((LANES, LANES), BF16)
    lane = lax.broadcasted_iota(jnp.int32, (tq, LANES), 1)

    def lane_total(acc):
        return jnp.dot(acc.astype(BF16), ones, preferred_element_type=F32)

    def body(W):
        nchunk = W // LANES
        chunk = lambda c: isc_ref[:, c * LANES:(c + 1) * LANES]

        ik = mk_ref[0:W, 0:IDX_DH]
        isc = jnp.zeros((tq, W), F32)
        for h in range(IDX_HEADS):
            d = lax.dot_general(iq_ref[:, h * IDX_DH:(h + 1) * IDX_DH], ik, NT_DIMS,
                                preferred_element_type=F32)
            isc = isc + jnp.maximum(d, 0.0) * mq_ref[:, IW_LANE0 + h:IW_LANE0 + h + 1].astype(F32)
        isc_ref[:, 0:W] = jnp.where(_chunk_mask(step * tq, tq, W), isc, NEG)

        def count(pred):
            acc = jnp.zeros((tq, LANES), F32)
            for c in range(nchunk):
                acc = acc + jnp.where(pred(chunk(c), c), 1.0, 0.0)
            return lane_total(acc)

        def value_step(i, t_u):
            cand_u = t_u | lax.shift_left(jnp.int32(1), jnp.int32(31) - i)
            cand_f = _key_to_float(cand_u ^ int_min)
            cnt = count(lambda ch, c: ch >= cand_f) + jnp.where(cand_f <= NEG, float(S - W), 0.0)
            return jnp.where(cnt >= topk, cand_u, t_u)

        t_u = lax.fori_loop(0, 32, value_step, jnp.zeros((tq, LANES), jnp.int32))
        thr = _key_to_float(t_u ^ int_min)
        need = topk - count(lambda ch, c: ch > thr)
        n_tie = count(lambda ch, c: ch == thr)

        p_ref[...] = jnp.full((tq, LANES), S - 1, jnp.int32)

        @pl.when(jnp.max(jnp.where(n_tie > need, 1.0, 0.0)) > 0.0)
        def _():
            def index_step(i, p):
                cand = p | lax.shift_right_logical(jnp.int32(S // 2), i)
                lt = count(lambda ch, c: (ch == thr) & (lane + c * LANES < cand)) < need
                return jnp.where(lt, cand, p)
            p_ref[...] = lax.fori_loop(0, int(math.log2(S)), index_step,
                                       jnp.zeros((tq, LANES), jnp.int32))

        p_last = p_ref[...]
        for c in range(nchunk):
            ch = chunk(c)
            sel = ((ch > thr) | ((ch == thr) & (lane + c * LANES <= p_last))) & (ch > 0.5 * NEG)
            isc_ref[:, c * LANES:(c + 1) * LANES] = jnp.where(sel, 0.0, NEG)

        for h in range(C_HEADS):
            sl = slice(h * C_DH, (h + 1) * C_DH)
            s = lax.dot_general(q_ref[:, sl], k_ref[0:W, sl], NT_DIMS, preferred_element_type=F32)
            s = s * scale + isc_ref[:, 0:W]
            e = jnp.exp(s - jnp.max(s, axis=-1, keepdims=True))
            p = e * (1.0 / jnp.sum(e, axis=-1, keepdims=True))
            o_ref[:, sl] = jnp.dot(p.astype(BF16), v_ref[0:W, sl],
                                   preferred_element_type=F32).astype(o_ref.dtype)

    steps_per_class = pl.num_programs(1) // len(widths)
    for ci, W in enumerate(widths):
        pl.when(step // steps_per_class == ci)(functools.partial(body, W))


def _dsa(pb, B, S):
    tq = min(256, S)
    nq = S // tq
    topk = min(TOPK_MAX, S // 4)
    widths = _causal_widths(S, nq)
    qspec = lambda t: pl.BlockSpec((tq, PROJ_TILE), lambda b, i: (b * nq + i, t))
    kspec = lambda t: pl.BlockSpec((S, PROJ_TILE), lambda b, i: (b, t))
    return pl.pallas_call(
        functools.partial(_dsa_kernel, topk=topk, widths=widths),
        grid=(B, nq),
        in_specs=[qspec(T_CQ), qspec(T_IQ), qspec(T_MISC), kspec(T_CK), kspec(T_CV), kspec(T_MISC)],
        out_specs=pl.BlockSpec((tq, C_WIDTH), lambda b, i: (b * nq + i, 0)),
        out_shape=jax.ShapeDtypeStruct((B * S, C_WIDTH), BF16),
        scratch_shapes=[pltpu.VMEM((tq, S), F32), pltpu.VMEM((tq, LANES), jnp.int32)],
        compiler_params=_cparams("parallel", "arbitrary"),
        name="dsa",
    )(pb, pb, pb, pb, pb, pb)


def _outproj_kernel(x_ref, mod_ref, oa_ref, ob_ref, oc_ref, w_ref, o_ref):
    y = jnp.dot(oa_ref[...], w_ref[0:A_WIDTH, :], preferred_element_type=F32)
    y = y + jnp.dot(ob_ref[...], w_ref[A_WIDTH:A_WIDTH + B_WIDTH, :], preferred_element_type=F32)
    y = y + jnp.dot(oc_ref[...], w_ref[A_WIDTH + B_WIDTH:, :], preferred_element_type=F32)
    o_ref[...] = x_ref[...] + mod_ref[0, 2:3, :] * y


def _outproj(x2, mod_l, oa, ob, oc, w_all, layer, S):
    M, D = x2.shape
    tm = min(512, S)
    nrow = S // tm
    row = lambda width: pl.BlockSpec((tm, width), lambda i: (i, 0))
    return pl.pallas_call(
        _outproj_kernel,
        grid=(M // tm,),
        in_specs=[
            row(D),
            pl.BlockSpec((1, 6, D), lambda i: (i // nrow, 0, 0)),
            row(A_WIDTH), row(B_WIDTH), row(C_WIDTH),
            pl.BlockSpec((None, D, D), lambda i: (layer, 0, 0)),
        ],
        out_specs=row(D),
        out_shape=jax.ShapeDtypeStruct((M, D), F32),
        compiler_params=_cparams("parallel"),
        name="outproj",
    )(x2, mod_l, oa, ob, oc, w_all)


def _ffn_kernel(x_ref, mod_ref, w1_ref, w2_ref, o_ref, h_ref, acc_ref):
    j = pl.program_id(1)

    @pl.when(j == 0)
    def _():
        y = _rms(x_ref[...])
        h_ref[...] = (y * (1.0 + mod_ref[0, 4:5, :]) + mod_ref[0, 3:4, :]).astype(BF16)
        acc_ref[...] = jnp.zeros_like(acc_ref)

    hid = jnp.square(jnp.maximum(jnp.dot(h_ref[...], w1_ref[...], preferred_element_type=F32), 0.0))
    acc_ref[...] += jnp.dot(hid.astype(BF16), w2_ref[...], preferred_element_type=F32)

    @pl.when(j == pl.num_programs(1) - 1)
    def _():
        o_ref[...] = x_ref[...] + mod_ref[0, 5:6, :] * acc_ref[...]


def _ffn(x2, mod_l, w1_all, w2_all, layer, S):
    M, D = x2.shape
    F = w1_all.shape[-1]
    tm = min(512, S)
    tf = 1024
    nrow = S // tm
    return pl.pallas_call(
        _ffn_kernel,
        grid=(M // tm, F // tf),
        in_specs=[
            pl.BlockSpec((tm, D), lambda i, j: (i, 0)),
            pl.BlockSpec((1, 6, D), lambda i, j: (i // nrow, 0, 0)),
            pl.BlockSpec((None, D, tf), lambda i, j: (layer, 0, j)),
            pl.BlockSpec((None, tf, D), lambda i, j: (layer, j, 0)),
        ],
        out_specs=pl.BlockSpec((tm, D), lambda i, j: (i, 0)),
        out_shape=jax.ShapeDtypeStruct((M, D), F32),
        scratch_shapes=[pltpu.VMEM((tm, D), BF16), pltpu.VMEM((tm, D), F32)],
        compiler_params=_cparams("parallel", "arbitrary"),
        name="ffn",
    )(x2, mod_l, w1_all, w2_all)


def _rope_tables(S):
    pos = jnp.arange(S, dtype=F32)[:, None]

    def angles(dim):
        inv = 1.0 / (ROPE_THETA ** (jnp.arange(0, dim, 2, dtype=F32) / dim))
        ang = pos * inv[None, :]
        return jnp.concatenate([ang, ang], axis=-1)

    a128 = angles(A_DQK)
    half = jnp.arange(LANES) < LANES // 2
    cos128, sin128 = jnp.cos(a128), jnp.sin(a128)
    a64 = jnp.tile(angles(IDX_DH), (1, 2))
    cos64, sin64 = jnp.cos(a64), jnp.sin(a64)
    low = (jnp.arange(LANES) % IDX_DH) < IDX_DH // 2
    return jnp.stack([
        cos128, jnp.where(half, -sin128, sin128),
        cos64, jnp.where(low, -sin64, 0.0), jnp.where(low, 0.0, sin64),
    ])


def _gain_rows(a_qnorm_l, a_knorm_l, b_vnorm_l, c_qnorm_l, c_knorm_l):
    rep = PROJ_TILE // LANES
    one = jnp.ones((PROJ_TILE,), F32)
    rows = [one] * N_PROJ_TILES
    rows[T_AQ] = rows[T_AQ + 1] = jnp.tile(a_qnorm_l, rep)
    rows[T_AK] = rows[T_AK + 1] = jnp.tile(a_knorm_l, rep)
    rows[T_BV] = b_vnorm_l
    rows[T_CQ] = jnp.tile(c_qnorm_l, rep)
    rows[T_CK] = jnp.tile(c_knorm_l, rep)
    return jnp.stack(rows)[:, None, :]


def kernel(x, c, w_mod, b_mod, w_in, w_out, a_qnorm, a_knorm, a_lambda, a_subln, b_vnorm, b_ws,
           b_bias, c_qnorm, c_knorm, w_ff1, w_ff2):
    B, S, D = x.shape
    L = w_mod.shape[0]
    assert D == D_MODEL and S % Q_BLOCK == 0 and w_in.shape[-1] == N_IN

    rope = _rope_tables(S)
    w_in_b = jnp.pad(w_in.astype(BF16), ((0, 0), (0, 0), (0, N_PROJ - N_IN)))
    w_out_b = w_out.astype(BF16)
    w_ff1_b = w_ff1.astype(BF16)
    w_ff2_b = w_ff2.astype(BF16)

    mod = _modulation(c, w_mod, b_mod).reshape(L, B, 6, D)
    x2 = x.reshape(B * S, D)
    for l in range(L):
        lambda_init = 0.8 - 0.6 * math.exp(-0.3 * l)
        gains = _gain_rows(a_qnorm[l], a_knorm[l], b_vnorm[l], c_qnorm[l], c_knorm[l])
        pb = _inproj(x2, mod[l], w_in_b, l, gains, rope, S)
        oa = _diffattn(pb, a_lambda[l], a_subln[l], lambda_init, B, S)
        ob = _gmlp(pb, b_ws[l], b_bias[l])
        oc = _dsa(pb, B, S)
        x2 = _outproj(x2, mod[l], oa, ob, oc, w_out_b, l, S)
        x2 = _ffn(x2, mod[l], w_ff1_b, w_ff2_b, l, S)
    return x2.reshape(B, S, D)
```

```python
import functools
import math

import jax
import jax.numpy as jnp
import numpy as np
from jax import lax
from jax.experimental import pallas as pl
from jax.experimental.pallas import tpu as pltpu

D_MODEL = 2048
CHUNK = 64
ROPE_THETA = 10000.0
EPS = 1e-6
NEG = -1e30
D_FF = 4 * D_MODEL
A_HEADS = 4
A_DQK = D_MODEL // 16
A_DV = 2 * A_DQK
A_WIDTH = A_HEADS * A_DV
B_GROUPS = 4
B_CH = D_MODEL // 16
B_WIDTH = B_GROUPS * B_CH
B_BLOCK = 128
C_HEADS = 4
C_DH = D_MODEL // 16
C_WIDTH = C_HEADS * C_DH
IDX_HEADS = 8
IDX_DH = D_MODEL // 32
TOPK_MAX = 256
Q_BLOCK = 128

LANES = 128
VMEM_LIMIT_BYTES = 56 * 1024 * 1024

PROJ_TILE = 512
T_AQ, T_AK, T_AV, T_BU, T_BV, T_CQ, T_CK, T_CV, T_IQ, T_MISC = 0, 2, 4, 6, 7, 8, 9, 10, 11, 12
N_PROJ_TILES = 13
N_IN = 3 * A_WIDTH + 2 * B_WIDTH + 3 * C_WIDTH + IDX_HEADS * IDX_DH + IDX_DH + IDX_HEADS
N_PROJ = N_PROJ_TILES * PROJ_TILE
IW_LANE0 = IDX_DH
IW_SCALE = IDX_HEADS ** -0.5 * IDX_DH ** -0.5

BF16 = jnp.bfloat16
F32 = jnp.float32
NT_DIMS = (((1,), (1,)), ((), ()))
LOG2_E = math.log2(math.e)


def _cparams(*sem):
    return pltpu.CompilerParams(dimension_semantics=sem, vmem_limit_bytes=VMEM_LIMIT_BYTES)


def _rms(v, gain=None):
    y = v * lax.rsqrt(jnp.mean(v * v, axis=-1, keepdims=True) + EPS)
    return y if gain is None else y * gain


def _mod_kernel(c_ref, w_ref, b_ref, o_ref):
    ca = jax.nn.silu(c_ref[...]).astype(BF16)
    o_ref[...] = jnp.dot(ca, w_ref[...].astype(BF16), preferred_element_type=F32) + b_ref[...]


def _modulation(c, w_mod, b_mod):
    L, D, E = w_mod.shape
    B = c.shape[0]
    tn = 2048
    return pl.pallas_call(
        _mod_kernel,
        grid=(L, E // tn),
        in_specs=[
            pl.BlockSpec((B, D), lambda l, j: (0, 0)),
            pl.BlockSpec((None, D, tn), lambda l, j: (l, 0, j)),
            pl.BlockSpec((None, 1, tn), lambda l, j: (l, 0, j)),
        ],
        out_specs=pl.BlockSpec((None, B, tn), lambda l, j: (l, 0, j)),
        out_shape=jax.ShapeDtypeStruct((L, B, E), F32),
        compiler_params=_cparams("parallel", "parallel"),
        name="modulation",
    )(c, w_mod, b_mod.reshape(L, 1, E))


def _inproj_kernel(x_ref, mod_ref, w_ref, gain_ref, rope_ref, o_ref, h_ref):
    y = _rms(x_ref[...])
    h_ref[...] = (y * (1.0 + mod_ref[0, 1:2, :]) + mod_ref[0, 0:1, :]).astype(BF16)
    groups = PROJ_TILE // LANES

    def rope128(v):
        return v * rope_ref[0] + pltpu.roll(v, LANES // 2, 1) * rope_ref[1]

    def rope64(v):
        return (v * rope_ref[2] + pltpu.roll(v, 96, 1) * rope_ref[3]
                + pltpu.roll(v, 32, 1) * rope_ref[4])

    def misc(v, g):
        if g > 0:
            return v
        lane = lax.broadcasted_iota(jnp.int32, v.shape, 1)
        return jnp.where(lane < IDX_DH, rope64(v), v * IW_SCALE)

    def gain(t, g):
        return gain_ref[t, :, g * LANES:(g + 1) * LANES]

    norm_rope = lambda v, t, g: rope128(_rms(v, gain(t, g)))
    epilogue = {t: (lambda v, t, g: v) for t in range(N_PROJ_TILES)}
    epilogue.update({t: norm_rope for t in (T_AQ, T_AQ + 1, T_AK, T_AK + 1, T_CQ, T_CK)})
    epilogue[T_BU] = lambda v, t, g: jax.nn.gelu(v)
    epilogue[T_BV] = lambda v, t, g: _rms(jax.nn.gelu(v), gain(t, g))
    epilogue[T_IQ] = lambda v, t, g: rope64(v)
    epilogue[T_MISC] = lambda v, t, g: misc(v, g)

    for t in range(N_PROJ_TILES):
        acc = jnp.dot(h_ref[...], w_ref[:, t * PROJ_TILE:(t + 1) * PROJ_TILE],
                      preferred_element_type=F32)
        for g in range(groups):
            lo = t * PROJ_TILE + g * LANES
            o_ref[:, lo:lo + LANES] = epilogue[t](
                acc[:, g * LANES:(g + 1) * LANES], t, g).astype(o_ref.dtype)


def _inproj(x2, mod_l, w_all, layer, gains, rope, S):
    M, D = x2.shape
    tm = min(256, S)
    nrow = S // tm
    return pl.pallas_call(
        _inproj_kernel,
        grid=(M // tm,),
        in_specs=[
            pl.BlockSpec((tm, D), lambda i: (i, 0)),
            pl.BlockSpec((1, 6, D), lambda i: (i // nrow, 0, 0)),
            pl.BlockSpec((None, D, N_PROJ), lambda i: (layer, 0, 0), pipeline_mode=pl.Buffered(1)),
            pl.BlockSpec((N_PROJ_TILES, 1, PROJ_TILE), lambda i: (0, 0, 0)),
            pl.BlockSpec((5, tm, LANES), lambda i: (0, i % nrow, 0)),
        ],
        out_specs=pl.BlockSpec((tm, N_PROJ), lambda i: (i, 0)),
        out_shape=jax.ShapeDtypeStruct((M, N_PROJ), BF16),
        scratch_shapes=[pltpu.VMEM((tm, D), BF16)],
        compiler_params=_cparams("parallel"),
        name="inproj",
    )(x2, mod_l, w_all, gains, rope)


def _chunk_mask(row0, rows, cols):
    r = row0 + lax.broadcasted_iota(jnp.int32, (rows, cols), 0)
    c = lax.broadcasted_iota(jnp.int32, (rows, cols), 1)
    return (c // CHUNK) <= (r // CHUNK)


def _diffattn_kernel(lam_ref, subln_ref, q_ref, k_ref, v_ref, o_ref, *, lambda_init, widths):
    tq = q_ref.shape[0]
    scale = A_DQK ** -0.5
    step = pl.program_id(2)
    lp = lam_ref[...]
    lam = (jnp.exp(jnp.sum(lp[0:1] * lp[1:2], axis=-1, keepdims=True))
           - jnp.exp(jnp.sum(lp[2:3] * lp[3:4], axis=-1, keepdims=True)) + lambda_init)

    diag_mask = _chunk_mask(0, tq, tq)

    def body(W):
        def unnormalised(m):
            s = lax.dot_general(q_ref[:, m * A_DQK:(m + 1) * A_DQK],
                                k_ref[0:W, m * A_DQK:(m + 1) * A_DQK],
                                NT_DIMS, preferred_element_type=F32)
            diag = jnp.where(diag_mask, s[:, W - tq:], NEG)
            s = diag if W == tq else jnp.concatenate([s[:, :W - tq], diag], axis=1)
            e = jnp.exp2((s - jnp.max(s, axis=-1, keepdims=True)) * (scale * LOG2_E))
            o = jnp.dot(e.astype(BF16), v_ref[0:W, :], preferred_element_type=F32)
            return o, jnp.sum(e, axis=-1, keepdims=True)

        o1, l1 = unnormalised(0)
        o2, l2 = unnormalised(1)
        o = o1 * (1.0 / l1) - o2 * (lam / l2)
        o_ref[...] = (_rms(o, subln_ref[...]) * (1.0 - lambda_init)).astype(o_ref.dtype)

    for ci, W in enumerate(widths):
        pl.when(step == ci)(functools.partial(body, W))


def _causal_widths(S, nq):
    n_classes = min(4, nq)
    widths = tuple(S * (ci + 1) // n_classes for ci in range(n_classes))
    assert nq % n_classes == 0 and all(w % LANES == 0 for w in widths)
    return widths


def _diffattn(pb, a_lambda_l, a_subln_l, lambda_init, B, S):
    tq = min(256, S)
    nq = S // tq
    return pl.pallas_call(
        functools.partial(_diffattn_kernel, lambda_init=lambda_init,
                          widths=tuple(tq * (i + 1) for i in range(nq))),
        grid=(B, A_HEADS, nq),
        in_specs=[
            pl.BlockSpec((4, A_DQK), lambda b, h, i: (0, 0)),
            pl.BlockSpec((1, A_DV), lambda b, h, i: (0, 0)),
            pl.BlockSpec((tq, A_DV), lambda b, h, i: (b * nq + i, h)),
            pl.BlockSpec((S, A_DV), lambda b, h, i: (b, A_HEADS + h)),
            pl.BlockSpec((S, A_DV), lambda b, h, i: (b, 2 * A_HEADS + h)),
        ],
        out_specs=pl.BlockSpec((tq, A_DV), lambda b, h, i: (b * nq + i, h)),
        out_shape=jax.ShapeDtypeStruct((B * S, A_WIDTH), BF16),
        compiler_params=_cparams("parallel", "parallel", "arbitrary"),
        name="diffattn",
    )(a_lambda_l, a_subln_l.reshape(1, A_DV), pb, pb, pb)


def _gmlp_kernel(u_ref, v_ref, ws_ref, bias_ref, o_ref):
    mask = _chunk_mask(0, B_BLOCK, B_BLOCK)
    for g in range(B_GROUPS):
        w = jnp.where(mask, ws_ref[g], 0.0).astype(BF16)
        cols = slice(g * B_CH, (g + 1) * B_CH)
        for n in range(u_ref.shape[0] // B_BLOCK):
            rows = slice(n * B_BLOCK, (n + 1) * B_BLOCK)
            z = jnp.dot(w, v_ref[rows, cols], preferred_element_type=F32) + bias_ref[:, g:g + 1]
            o_ref[rows, cols] = (u_ref[rows, cols].astype(F32) * z).astype(o_ref.dtype)


def _gmlp(pb, ws_l, bias_l, S):
    M = pb.shape[0]
    tm = min(512, S)
    return pl.pallas_call(
        _gmlp_kernel,
        grid=(M // tm,),
        in_specs=[
            pl.BlockSpec((tm, B_WIDTH), lambda i: (i, T_BU)),
            pl.BlockSpec((tm, B_WIDTH), lambda i: (i, T_BV)),
            pl.BlockSpec((B_GROUPS, B_BLOCK, B_BLOCK), lambda i: (0, 0, 0)),
            pl.BlockSpec((B_BLOCK, B_GROUPS), lambda i: (0, 0)),
        ],
        out_specs=pl.BlockSpec((tm, B_WIDTH), lambda i: (i, 0)),
        out_shape=jax.ShapeDtypeStruct((M, B_WIDTH), BF16),
        compiler_params=_cparams("parallel"),
        name="gmlp",
    )(pb, pb, ws_l, bias_l.T)


def _key_to_float(key):
    return lax.bitcast_convert_type(jnp.where(key < 0, key ^ jnp.int32(0x7FFFFFFF), key), F32)


def _dsa_kernel(q_ref, iq_ref, mq_ref, k_ref, v_ref, mk_ref, o_ref, isc_ref, p_ref, *, topk, widths):
    tq = q_ref.shape[0]
    S = k_ref.shape[0]
    scale = C_DH ** -0.5
    int_min = jnp.int32(-2 ** 31)
    step = pl.program_id(1)
    ones = jnp.ones((LANES, LANES), BF16)
    lane = lax.broadcasted_iota(jnp.int32, (tq, LANES), 1)

    def lane_total(acc):
        return jnp.dot(acc.astype(BF16), ones, preferred_element_type=F32)

    def body(W):
        nchunk = W // LANES
        chunk = lambda c: isc_ref[:, c * LANES:(c + 1) * LANES]

        ik = mk_ref[0:W, 0:IDX_DH]
        isc = jnp.zeros((tq, W), F32)
        for h in range(IDX_HEADS):
            d = lax.dot_general(iq_ref[:, h * IDX_DH:(h + 1) * IDX_DH], ik, NT_DIMS,
                                preferred_element_type=F32)
            isc = isc + jnp.maximum(d, 0.0) * mq_ref[:, IW_LANE0 + h:IW_LANE0 + h + 1].astype(F32)
        isc_ref[:, 0:W] = jnp.where(_chunk_mask(step * tq, tq, W), isc, NEG)

        def count(pred):
            acc = jnp.zeros((tq, LANES), F32)
            for c in range(nchunk):
                acc = acc + jnp.where(pred(chunk(c), c), 1.0, 0.0)
            return lane_total(acc)

        def value_step(i, t_u):
            cand_u = t_u | lax.shift_left(jnp.int32(1), jnp.int32(31) - i)
            cand_f = _key_to_float(cand_u ^ int_min)
            cnt = count(lambda ch, c: ch >= cand_f) + jnp.where(cand_f <= NEG, float(S - W), 0.0)
            return jnp.where(cnt >= topk, cand_u, t_u)

        t_u = lax.fori_loop(0, 32, value_step, jnp.zeros((tq, LANES), jnp.int32))
        thr = _key_to_float(t_u ^ int_min)
        need = topk - count(lambda ch, c: ch > thr)
        n_tie = count(lambda ch, c: ch == thr)

        p_ref[...] = jnp.full((tq, LANES), S - 1, jnp.int32)

        @pl.when(jnp.max(jnp.where(n_tie > need, 1.0, 0.0)) > 0.0)
        def _():
            def index_step(i, p):
                cand = p | lax.shift_right_logical(jnp.int32(S // 2), i)
                lt = count(lambda ch, c: (ch == thr) & (lane + c * LANES < cand)) < need
                return jnp.where(lt, cand, p)
            p_ref[...] = lax.fori_loop(0, int(math.log2(S)), index_step,
                                       jnp.zeros((tq, LANES), jnp.int32))

        p_last = p_ref[...]
        for c in range(nchunk):
            ch = chunk(c)
            sel = ((ch > thr) | ((ch == thr) & (lane + c * LANES <= p_last))) & (ch > 0.5 * NEG)
            isc_ref[:, c * LANES:(c + 1) * LANES] = jnp.where(sel, 0.0, NEG)

        for h in range(C_HEADS):
            sl = slice(h * C_DH, (h + 1) * C_DH)
            s = lax.dot_general(q_ref[:, sl], k_ref[0:W, sl], NT_DIMS, preferred_element_type=F32)
            s = s + isc_ref[:, 0:W]
            e = jnp.exp2((s - jnp.max(s, axis=-1, keepdims=True)) * (scale * LOG2_E))
            o = jnp.dot(e.astype(BF16), v_ref[0:W, sl], preferred_element_type=F32)
            o_ref[:, sl] = (o * (1.0 / jnp.sum(e, axis=-1, keepdims=True))).astype(o_ref.dtype)

    steps_per_class = pl.num_programs(1) // len(widths)
    for ci, W in enumerate(widths):
        pl.when(step // steps_per_class == ci)(functools.partial(body, W))


def _dsa(pb, B, S):
    tq = min(256, S)
    nq = S // tq
    topk = min(TOPK_MAX, S // 4)
    widths = _causal_widths(S, nq)
    qspec = lambda t: pl.BlockSpec((tq, PROJ_TILE), lambda b, i: (b * nq + i, t))
    kspec = lambda t: pl.BlockSpec((S, PROJ_TILE), lambda b, i: (b, t))
    return pl.pallas_call(
        functools.partial(_dsa_kernel, topk=topk, widths=widths),
        grid=(B, nq),
        in_specs=[qspec(T_CQ), qspec(T_IQ), qspec(T_MISC), kspec(T_CK), kspec(T_CV), kspec(T_MISC)],
        out_specs=pl.BlockSpec((tq, C_WIDTH), lambda b, i: (b * nq + i, 0)),
        out_shape=jax.ShapeDtypeStruct((B * S, C_WIDTH), BF16),
        scratch_shapes=[pltpu.VMEM((tq, S), F32), pltpu.VMEM((tq, LANES), jnp.int32)],
        compiler_params=_cparams("parallel", "arbitrary"),
        name="dsa",
    )(pb, pb, pb, pb, pb, pb)


def _outproj_kernel(x_ref, mod_ref, oa_ref, ob_ref, oc_ref, w_ref, o_ref):
    y = jnp.dot(oa_ref[...], w_ref[0:A_WIDTH, :], preferred_element_type=F32)
    y = y + jnp.dot(ob_ref[...], w_ref[A_WIDTH:A_WIDTH + B_WIDTH, :], preferred_element_type=F32)
    y = y + jnp.dot(oc_ref[...], w_ref[A_WIDTH + B_WIDTH:, :], preferred_element_type=F32)
    o_ref[...] = x_ref[...] + mod_ref[0, 2:3, :] * y


def _outproj(x2, mod_l, oa, ob, oc, w_all, layer, S):
    M, D = x2.shape
    tm = min(512, S)
    nrow = S // tm
    row = lambda width: pl.BlockSpec((tm, width), lambda i: (i, 0))
    return pl.pallas_call(
        _outproj_kernel,
        grid=(M // tm,),
        in_specs=[
            row(D),
            pl.BlockSpec((1, 6, D), lambda i: (i // nrow, 0, 0)),
            row(A_WIDTH), row(B_WIDTH), row(C_WIDTH),
            pl.BlockSpec((None, D, D), lambda i: (layer, 0, 0)),
        ],
        out_specs=row(D),
        out_shape=jax.ShapeDtypeStruct((M, D), F32),
        compiler_params=_cparams("parallel"),
        name="outproj",
    )(x2, mod_l, oa, ob, oc, w_all)


def _ffn_kernel(x_ref, mod_ref, w1_ref, w2_ref, o_ref, h_ref, acc_ref):
    j = pl.program_id(1)

    @pl.when(j == 0)
    def _():
        y = _rms(x_ref[...])
        h_ref[...] = (y * (1.0 + mod_ref[0, 4:5, :]) + mod_ref[0, 3:4, :]).astype(BF16)
        acc_ref[...] = jnp.zeros_like(acc_ref)

    hid = jnp.square(jnp.maximum(jnp.dot(h_ref[...], w1_ref[...], preferred_element_type=F32), 0.0))
    acc_ref[...] += jnp.dot(hid.astype(BF16), w2_ref[...], preferred_element_type=F32)

    @pl.when(j == pl.num_programs(1) - 1)
    def _():
        o_ref[...] = x_ref[...] + mod_ref[0, 5:6, :] * acc_ref[...]


def _ffn(x2, mod_l, w1_all, w2_all, layer, S):
    M, D = x2.shape
    F = w1_all.shape[-1]
    tm = min(512, S)
    tf = 1024
    nrow = S // tm
    return pl.pallas_call(
        _ffn_kernel,
        grid=(M // tm, F // tf),
        in_specs=[
            pl.BlockSpec((tm, D), lambda i, j: (i, 0)),
            pl.BlockSpec((1, 6, D), lambda i, j: (i // nrow, 0, 0)),
            pl.BlockSpec((None, D, tf), lambda i, j: (layer, 0, j)),
            pl.BlockSpec((None, tf, D), lambda i, j: (layer, j, 0)),
        ],
        out_specs=pl.BlockSpec((tm, D), lambda i, j: (i, 0)),
        out_shape=jax.ShapeDtypeStruct((M, D), F32),
        scratch_shapes=[pltpu.VMEM((tm, D), BF16), pltpu.VMEM((tm, D), F32)],
        compiler_params=_cparams("parallel", "arbitrary"),
        name="ffn",
    )(x2, mod_l, w1_all, w2_all)


def _rope_tables(S):
    pos = jnp.arange(S, dtype=F32)[:, None]

    def angles(dim):
        inv = 1.0 / (ROPE_THETA ** (jnp.arange(0, dim, 2, dtype=F32) / dim))
        ang = pos * inv[None, :]
        return jnp.concatenate([ang, ang], axis=-1)

    a128 = angles(A_DQK)
    half = jnp.arange(LANES) < LANES // 2
    cos128, sin128 = jnp.cos(a128), jnp.sin(a128)
    a64 = jnp.tile(angles(IDX_DH), (1, 2))
    cos64, sin64 = jnp.cos(a64), jnp.sin(a64)
    low = (jnp.arange(LANES) % IDX_DH) < IDX_DH // 2
    return jnp.stack([
        cos128, jnp.where(half, -sin128, sin128),
        cos64, jnp.where(low, -sin64, 0.0), jnp.where(low, 0.0, sin64),
    ])


def _gain_rows(a_qnorm_l, a_knorm_l, b_vnorm_l, c_qnorm_l, c_knorm_l):
    rep = PROJ_TILE // LANES
    one = jnp.ones((PROJ_TILE,), F32)
    rows = [one] * N_PROJ_TILES
    rows[T_AQ] = rows[T_AQ + 1] = jnp.tile(a_qnorm_l, rep)
    rows[T_AK] = rows[T_AK + 1] = jnp.tile(a_knorm_l, rep)
    rows[T_BV] = b_vnorm_l
    rows[T_CQ] = jnp.tile(c_qnorm_l, rep)
    rows[T_CK] = jnp.tile(c_knorm_l, rep)
    return jnp.stack(rows)[:, None, :]


def kernel(x, c, w_mod, b_mod, w_in, w_out, a_qnorm, a_knorm, a_lambda, a_subln, b_vnorm, b_ws,
           b_bias, c_qnorm, c_knorm, w_ff1, w_ff2):
    B, S, D = x.shape
    L = w_mod.shape[0]
    assert D == D_MODEL and S % Q_BLOCK == 0 and w_in.shape[-1] == N_IN

    rope = _rope_tables(S)
    w_in_b = jnp.pad(w_in, ((0, 0), (0, 0), (0, N_PROJ - N_IN))).astype(BF16)
    w_out_b = w_out.astype(BF16)
    w_ff1_b = w_ff1.astype(BF16)
    w_ff2_b = w_ff2.astype(BF16)

    mod = _modulation(c, w_mod, b_mod).reshape(L, B, 6, D)
    x2 = x.reshape(B * S, D)
    for l in range(L):
        lambda_init = 0.8 - 0.6 * math.exp(-0.3 * l)
        gains = _gain_rows(a_qnorm[l], a_knorm[l], b_vnorm[l], c_qnorm[l], c_knorm[l])
        pb = _inproj(x2, mod[l], w_in_b, l, gains, rope, S)
        oa = _diffattn(pb, a_lambda[l], a_subln[l], lambda_init, B, S)
        ob = _gmlp(pb, b_ws[l], b_bias[l], S)
        oc = _dsa(pb, B, S)
        x2 = _outproj(x2, mod[l], oa, ob, oc, w_out_b, l, S)
        x2 = _ffn(x2, mod[l], w_ff1_b, w_ff2_b, l, S)
    return x2.reshape(B, S, D)
```

```python
import functools
import math

import jax
import jax.numpy as jnp
import numpy as np
from jax import lax
from jax.experimental import pallas as pl
from jax.experimental.pallas import tpu as pltpu

D_MODEL = 2048
CHUNK = 64
ROPE_THETA = 10000.0
EPS = 1e-6
NEG = -1e30
D_FF = 4 * D_MODEL
A_HEADS = 4
A_DQK = D_MODEL // 16
A_DV = 2 * A_DQK
A_WIDTH = A_HEADS * A_DV
B_GROUPS = 4
B_CH = D_MODEL // 16
B_WIDTH = B_GROUPS * B_CH
B_BLOCK = 128
C_HEADS = 4
C_DH = D_MODEL // 16
C_WIDTH = C_HEADS * C_DH
IDX_HEADS = 8
IDX_DH = D_MODEL // 32
TOPK_MAX = 256
Q_BLOCK = 128

LANES = 128
VMEM_LIMIT_BYTES = 56 * 1024 * 1024

PROJ_TILE = 512
T_AQ, T_AK, T_AV, T_BU, T_BV, T_CQ, T_CK, T_CV, T_IQ, T_MISC = 0, 2, 4, 6, 7, 8, 9, 10, 11, 12
N_PROJ_TILES = 13
N_IN = 3 * A_WIDTH + 2 * B_WIDTH + 3 * C_WIDTH + IDX_HEADS * IDX_DH + IDX_DH + IDX_HEADS
N_PROJ = N_PROJ_TILES * PROJ_TILE
IW_LANE0 = IDX_DH
IW_SCALE = IDX_HEADS ** -0.5 * IDX_DH ** -0.5

BF16 = jnp.bfloat16
F32 = jnp.float32
NT_DIMS = (((1,), (1,)), ((), ()))
LOG2_E = math.log2(math.e)


def _cparams(*sem):
    return pltpu.CompilerParams(dimension_semantics=sem, vmem_limit_bytes=VMEM_LIMIT_BYTES)


def _rms(v, gain=None):
    y = v * lax.rsqrt(jnp.mean(v * v, axis=-1, keepdims=True) + EPS)
    return y if gain is None else y * gain


def _mod_kernel(c_ref, w_ref, b_ref, o_ref):
    @pl.when(pl.program_id(1) == 0)
    def _():
        o_ref[...] = jnp.broadcast_to(b_ref[...], o_ref.shape)

    ca = jax.nn.silu(c_ref[...]).astype(BF16)
    o_ref[...] += jnp.dot(ca, w_ref[...].astype(BF16), preferred_element_type=F32)


def _modulation(c, w_mod, b_mod):
    L, D, E = w_mod.shape
    B = c.shape[0]
    td = 256
    return pl.pallas_call(
        _mod_kernel,
        grid=(L, D // td),
        in_specs=[
            pl.BlockSpec((B, td), lambda l, k: (0, k)),
            pl.BlockSpec((None, td, E), lambda l, k: (l, k, 0)),
            pl.BlockSpec((None, 1, E), lambda l, k: (l, 0, 0)),
        ],
        out_specs=pl.BlockSpec((None, B, E), lambda l, k: (l, 0, 0)),
        out_shape=jax.ShapeDtypeStruct((L, B, E), F32),
        compiler_params=_cparams("parallel", "arbitrary"),
        name="modulation",
    )(c, w_mod, b_mod.reshape(L, 1, E))


def _inproj_kernel(x_ref, mod_ref, w_ref, gain_ref, rope_ref, o_ref, h_ref):
    y = _rms(x_ref[...])
    h_ref[...] = (y * (1.0 + mod_ref[0, 1:2, :]) + mod_ref[0, 0:1, :]).astype(BF16)
    groups = PROJ_TILE // LANES

    def rope128(v):
        return v * rope_ref[0] + pltpu.roll(v, LANES // 2, 1) * rope_ref[1]

    def rope64(v):
        return (v * rope_ref[2] + pltpu.roll(v, 96, 1) * rope_ref[3]
                + pltpu.roll(v, 32, 1) * rope_ref[4])

    def misc(v, g):
        if g > 0:
            return v
        lane = lax.broadcasted_iota(jnp.int32, v.shape, 1)
        return jnp.where(lane < IDX_DH, rope64(v), v * IW_SCALE)

    def gain(t, g):
        return gain_ref[t, :, g * LANES:(g + 1) * LANES]

    norm_rope = lambda v, t, g: rope128(_rms(v, gain(t, g)))
    epilogue = {t: (lambda v, t, g: v) for t in range(N_PROJ_TILES)}
    epilogue.update({t: norm_rope for t in (T_AQ, T_AQ + 1, T_AK, T_AK + 1, T_CQ, T_CK)})
    epilogue[T_BU] = lambda v, t, g: jax.nn.gelu(v)
    epilogue[T_BV] = lambda v, t, g: _rms(jax.nn.gelu(v), gain(t, g))
    epilogue[T_IQ] = lambda v, t, g: rope64(v)
    epilogue[T_MISC] = lambda v, t, g: misc(v, g)

    for t in range(N_PROJ_TILES):
        acc = jnp.dot(h_ref[...], w_ref[:, t * PROJ_TILE:(t + 1) * PROJ_TILE],
                      preferred_element_type=F32)
        for g in range(groups):
            lo = t * PROJ_TILE + g * LANES
            o_ref[:, lo:lo + LANES] = epilogue[t](
                acc[:, g * LANES:(g + 1) * LANES], t, g).astype(o_ref.dtype)


def _inproj(x2, mod_l, w_all, layer, gains, rope, S):
    M, D = x2.shape
    tm = min(256, S)
    nrow = S // tm
    return pl.pallas_call(
        _inproj_kernel,
        grid=(M // tm,),
        in_specs=[
            pl.BlockSpec((tm, D), lambda i: (i, 0)),
            pl.BlockSpec((1, 6, D), lambda i: (i // nrow, 0, 0)),
            pl.BlockSpec((None, D, N_PROJ), lambda i: (layer, 0, 0), pipeline_mode=pl.Buffered(1)),
            pl.BlockSpec((N_PROJ_TILES, 1, PROJ_TILE), lambda i: (0, 0, 0)),
            pl.BlockSpec((5, tm, LANES), lambda i: (0, i % nrow, 0)),
        ],
        out_specs=pl.BlockSpec((tm, N_PROJ), lambda i: (i, 0)),
        out_shape=jax.ShapeDtypeStruct((M, N_PROJ), BF16),
        scratch_shapes=[pltpu.VMEM((tm, D), BF16)],
        compiler_params=_cparams("parallel"),
        name="inproj",
    )(x2, mod_l, w_all, gains, rope)


def _chunk_mask(row0, rows, cols):
    r = row0 + lax.broadcasted_iota(jnp.int32, (rows, cols), 0)
    c = lax.broadcasted_iota(jnp.int32, (rows, cols), 1)
    return (c // CHUNK) <= (r // CHUNK)


def _diffattn_kernel(lam_ref, subln_ref, q_ref, k_ref, v_ref, o_ref, *, lambda_init, widths):
    tq = q_ref.shape[0]
    half = tq // 2
    scale = A_DQK ** -0.5
    step = pl.program_id(2)
    lp = lam_ref[...]
    lam = (jnp.exp(jnp.sum(lp[0:1] * lp[1:2], axis=-1, keepdims=True))
           - jnp.exp(jnp.sum(lp[2:3] * lp[3:4], axis=-1, keepdims=True)) + lambda_init)

    diag_mask = _chunk_mask(0, half, half)

    def body(W):
        extent = lambda r: W - (1 - r) * half

        def scores(m, r):
            s = lax.dot_general(q_ref[r * half:(r + 1) * half, m * A_DQK:(m + 1) * A_DQK],
                                k_ref[0:extent(r), m * A_DQK:(m + 1) * A_DQK],
                                NT_DIMS, preferred_element_type=F32)
            diag = jnp.where(diag_mask, s[:, extent(r) - half:], NEG)
            return diag if extent(r) == half else jnp.concatenate([s[:, :extent(r) - half], diag], axis=1)

        def numerator(s):
            e = jnp.exp2((s - jnp.max(s, axis=-1, keepdims=True)) * (scale * LOG2_E))
            return e.astype(BF16), jnp.sum(e, axis=-1, keepdims=True)

        ss = [[scores(m, r) for m in range(2)] for r in range(2)]
        outs = []
        for r in range(2):
            (e1, l1), (e2, l2) = numerator(ss[r][0]), numerator(ss[r][1])
            o1 = jnp.dot(e1, v_ref[0:extent(r), :], preferred_element_type=F32)
            o2 = jnp.dot(e2, v_ref[0:extent(r), :], preferred_element_type=F32)
            outs.append(o1 * (1.0 / l1) - o2 * (lam / l2))
        o = jnp.concatenate(outs, axis=0)
        o_ref[...] = (_rms(o, subln_ref[...]) * (1.0 - lambda_init)).astype(o_ref.dtype)

    for ci, W in enumerate(widths):
        pl.when(step == ci)(functools.partial(body, W))


def _diffattn(pb, a_lambda_l, a_subln_l, lambda_init, B, S):
    tq = min(512, S)
    nq = S // tq
    return pl.pallas_call(
        functools.partial(_diffattn_kernel, lambda_init=lambda_init,
                          widths=tuple(tq * (i + 1) for i in range(nq))),
        grid=(B, A_HEADS, nq),
        in_specs=[
            pl.BlockSpec((4, A_DQK), lambda b, h, i: (0, 0)),
            pl.BlockSpec((1, A_DV), lambda b, h, i: (0, 0)),
            pl.BlockSpec((tq, A_DV), lambda b, h, i: (b * nq + i, h)),
            pl.BlockSpec((S, A_DV), lambda b, h, i: (b, A_HEADS + h)),
            pl.BlockSpec((S, A_DV), lambda b, h, i: (b, 2 * A_HEADS + h)),
        ],
        out_specs=pl.BlockSpec((tq, A_DV), lambda b, h, i: (b * nq + i, h)),
        out_shape=jax.ShapeDtypeStruct((B * S, A_WIDTH), BF16),
        compiler_params=_cparams("parallel", "parallel", "arbitrary"),
        name="diffattn",
    )(a_lambda_l, a_subln_l.reshape(1, A_DV), pb, pb, pb)


def _gmlp_kernel(u_ref, v_ref, ws_ref, bias_ref, o_ref):
    mask = _chunk_mask(0, B_BLOCK, B_BLOCK)
    for g in range(B_GROUPS):
        w = jnp.where(mask, ws_ref[g], 0.0).astype(BF16)
        cols = slice(g * B_CH, (g + 1) * B_CH)
        for n in range(u_ref.shape[0] // B_BLOCK):
            rows = slice(n * B_BLOCK, (n + 1) * B_BLOCK)
            z = jnp.dot(w, v_ref[rows, cols], preferred_element_type=F32) + bias_ref[:, g:g + 1]
            o_ref[rows, cols] = (u_ref[rows, cols].astype(F32) * z).astype(o_ref.dtype)


def _gmlp(pb, ws_l, bias_l, S):
    M = pb.shape[0]
    tm = min(512, S)
    return pl.pallas_call(
        _gmlp_kernel,
        grid=(M // tm,),
        in_specs=[
            pl.BlockSpec((tm, B_WIDTH), lambda i: (i, T_BU)),
            pl.BlockSpec((tm, B_WIDTH), lambda i: (i, T_BV)),
            pl.BlockSpec((B_GROUPS, B_BLOCK, B_BLOCK), lambda i: (0, 0, 0)),
            pl.BlockSpec((B_BLOCK, B_GROUPS), lambda i: (0, 0)),
        ],
        out_specs=pl.BlockSpec((tm, B_WIDTH), lambda i: (i, 0)),
        out_shape=jax.ShapeDtypeStruct((M, B_WIDTH), BF16),
        compiler_params=_cparams("parallel"),
        name="gmlp",
    )(pb, pb, ws_l, bias_l.T)


def _key_to_float(key):
    return lax.bitcast_convert_type(jnp.where(key < 0, key ^ jnp.int32(0x7FFFFFFF), key), F32)


def _dsa_kernel(q_ref, iq_ref, mq_ref, k_ref, v_ref, mk_ref, o_ref, isc_ref, p_ref, *, topk, widths):
    tq = q_ref.shape[0]
    half = tq // 2
    S = k_ref.shape[0]
    scale = C_DH ** -0.5
    int_min = jnp.int32(-2 ** 31)
    step = pl.program_id(1)
    lane = lax.broadcasted_iota(jnp.int32, (half, LANES), 1)
    zero = jnp.zeros((half, LANES), jnp.int32)

    def body(W):
        subs = [(slice(0, half), W - half), (slice(half, tq), W)]

        def chunk(rows, c):
            return isc_ref[rows, c * LANES:(c + 1) * LANES]

        def count(rows, extent, pred):
            acc = jnp.zeros((half, LANES), F32)
            for c in range(extent // LANES):
                acc = acc + jnp.where(pred(chunk(rows, c), c), 1.0, 0.0)
            return jnp.broadcast_to(jnp.sum(acc, axis=-1, keepdims=True), acc.shape)

        for r, (rows, extent) in enumerate(subs):
            ik = mk_ref[0:extent, 0:IDX_DH]
            isc = jnp.zeros((half, extent), F32)
            for h in range(IDX_HEADS):
                d = lax.dot_general(iq_ref[rows, h * IDX_DH:(h + 1) * IDX_DH], ik, NT_DIMS,
                                    preferred_element_type=F32)
                isc = isc + (jnp.maximum(d, 0.0)
                             * mq_ref[rows, IW_LANE0 + h:IW_LANE0 + h + 1].astype(F32))
            isc_ref[rows, 0:extent] = jnp.where(
                _chunk_mask(step * tq + r * half, half, extent), isc, NEG)

        def value_step(i, t_us):
            out = []
            for (rows, extent), t_u in zip(subs, t_us):
                cand_u = t_u | lax.shift_left(jnp.int32(1), jnp.int32(31) - i)
                cand_f = _key_to_float(cand_u ^ int_min)
                cnt = (count(rows, extent, lambda ch, c: ch >= cand_f)
                       + jnp.where(cand_f <= NEG, float(S - extent), 0.0))
                out.append(jnp.where(cnt >= topk, cand_u, t_u))
            return tuple(out)

        thrs = [_key_to_float(t_u ^ int_min) for t_u in lax.fori_loop(0, 32, value_step, (zero, zero))]
        needs = [topk - count(rows, extent, lambda ch, c: ch > thr)
                 for (rows, extent), thr in zip(subs, thrs)]
        n_ties = [count(rows, extent, lambda ch, c: ch == thr)
                  for (rows, extent), thr in zip(subs, thrs)]

        p_ref[...] = jnp.full((tq, LANES), S - 1, jnp.int32)
        surplus = jnp.concatenate(
            [jnp.where(n > need, 1.0, 0.0) for n, need in zip(n_ties, needs)], axis=0)

        @pl.when(jnp.max(surplus) > 0.0)
        def _():
            for (rows, extent), thr, need in zip(subs, thrs, needs):
                def index_step(i, p):
                    cand = p | lax.shift_right_logical(jnp.int32(S // 2), i)
                    ties_below = count(
                        rows, extent, lambda ch, c: (ch == thr) & (lane + c * LANES < cand))
                    return jnp.where(ties_below < need, cand, p)
                p_ref[rows, :] = lax.fori_loop(0, int(math.log2(S)), index_step, zero)

        for (rows, extent), thr in zip(subs, thrs):
            p_last = p_ref[rows, :]
            for c in range(extent // LANES):
                ch = chunk(rows, c)
                sel = ((ch > thr) | ((ch == thr) & (lane + c * LANES <= p_last))) & (ch > 0.5 * NEG)
                isc_ref[rows, c * LANES:(c + 1) * LANES] = jnp.where(sel, 0.0, NEG)

        for h in range(C_HEADS):
            sl = slice(h * C_DH, (h + 1) * C_DH)
            for rows, extent in subs:
                s = lax.dot_general(q_ref[rows, sl], k_ref[0:extent, sl], NT_DIMS,
                                    preferred_element_type=F32)
                s = s + isc_ref[rows, 0:extent]
                e = jnp.exp2((s - jnp.max(s, axis=-1, keepdims=True)) * (scale * LOG2_E))
                o = jnp.dot(e.astype(BF16), v_ref[0:extent, sl], preferred_element_type=F32)
                o_ref[rows, sl] = (o * (1.0 / jnp.sum(e, axis=-1, keepdims=True))).astype(o_ref.dtype)

    for ci, W in enumerate(widths):
        pl.when(step == ci)(functools.partial(body, W))


def _dsa(pb, B, S):
    tq = min(512, S)
    nq = S // tq
    topk = min(TOPK_MAX, S // 4)
    widths = tuple(tq * (i + 1) for i in range(nq))
    qspec = lambda t: pl.BlockSpec((tq, PROJ_TILE), lambda b, i: (b * nq + i, t))
    kspec = lambda t: pl.BlockSpec((S, PROJ_TILE), lambda b, i: (b, t))
    return pl.pallas_call(
        functools.partial(_dsa_kernel, topk=topk, widths=widths),
        grid=(B, nq),
        in_specs=[qspec(T_CQ), qspec(T_IQ), qspec(T_MISC), kspec(T_CK), kspec(T_CV), kspec(T_MISC)],
        out_specs=pl.BlockSpec((tq, C_WIDTH), lambda b, i: (b * nq + i, 0)),
        out_shape=jax.ShapeDtypeStruct((B * S, C_WIDTH), BF16),
        scratch_shapes=[pltpu.VMEM((tq, S), F32), pltpu.VMEM((tq, LANES), jnp.int32)],
        compiler_params=_cparams("parallel", "arbitrary"),
        name="dsa",
    )(pb, pb, pb, pb, pb, pb)


def _outproj_kernel(x_ref, mod_ref, oa_ref, ob_ref, oc_ref, w_ref, o_ref):
    y = jnp.dot(oa_ref[...], w_ref[0:A_WIDTH, :], preferred_element_type=F32)
    y = y + jnp.dot(ob_ref[...], w_ref[A_WIDTH:A_WIDTH + B_WIDTH, :], preferred_element_type=F32)
    y = y + jnp.dot(oc_ref[...], w_ref[A_WIDTH + B_WIDTH:, :], preferred_element_type=F32)
    o_ref[...] = x_ref[...] + mod_ref[0, 2:3, :] * y


def _outproj(x2, mod_l, oa, ob, oc, w_all, layer, S):
    M, D = x2.shape
    tm = min(512, S)
    nrow = S // tm
    row = lambda width: pl.BlockSpec((tm, width), lambda i: (i, 0))
    return pl.pallas_call(
        _outproj_kernel,
        grid=(M // tm,),
        in_specs=[
            row(D),
            pl.BlockSpec((1, 6, D), lambda i: (i // nrow, 0, 0)),
            row(A_WIDTH), row(B_WIDTH), row(C_WIDTH),
            pl.BlockSpec((None, D, D), lambda i: (layer, 0, 0)),
        ],
        out_specs=row(D),
        out_shape=jax.ShapeDtypeStruct((M, D), F32),
        compiler_params=_cparams("parallel"),
        name="outproj",
    )(x2, mod_l, oa, ob, oc, w_all)


def _ffn_kernel(x_ref, mod_ref, w1_ref, w2_ref, o_ref, h_ref, acc_ref):
    j = pl.program_id(1)

    @pl.when(j == 0)
    def _():
        y = _rms(x_ref[...])
        h_ref[...] = (y * (1.0 + mod_ref[0, 4:5, :]) + mod_ref[0, 3:4, :]).astype(BF16)
        acc_ref[...] = jnp.zeros_like(acc_ref)

    hid = jnp.square(jnp.maximum(jnp.dot(h_ref[...], w1_ref[...], preferred_element_type=F32), 0.0))
    acc_ref[...] += jnp.dot(hid.astype(BF16), w2_ref[...], preferred_element_type=F32)

    @pl.when(j == pl.num_programs(1) - 1)
    def _():
        o_ref[...] = x_ref[...] + mod_ref[0, 5:6, :] * acc_ref[...]


def _ffn(x2, mod_l, w1_all, w2_all, layer, S):
    M, D = x2.shape
    F = w1_all.shape[-1]
    tm = min(512, S)
    tf = 1024
    nrow = S // tm
    return pl.pallas_call(
        _ffn_kernel,
        grid=(M // tm, F // tf),
        in_specs=[
            pl.BlockSpec((tm, D), lambda i, j: (i, 0)),
            pl.BlockSpec((1, 6, D), lambda i, j: (i // nrow, 0, 0)),
            pl.BlockSpec((None, D, tf), lambda i, j: (layer, 0, j)),
            pl.BlockSpec((None, tf, D), lambda i, j: (layer, j, 0)),
        ],
        out_specs=pl.BlockSpec((tm, D), lambda i, j: (i, 0)),
        out_shape=jax.ShapeDtypeStruct((M, D), F32),
        scratch_shapes=[pltpu.VMEM((tm, D), BF16), pltpu.VMEM((tm, D), F32)],
        compiler_params=_cparams("parallel", "arbitrary"),
        name="ffn",
    )(x2, mod_l, w1_all, w2_all)


def _rope_tables(S):
    pos = jnp.arange(S, dtype=F32)[:, None]

    def angles(dim):
        inv = 1.0 / (ROPE_THETA ** (jnp.arange(0, dim, 2, dtype=F32) / dim))
        ang = pos * inv[None, :]
        return jnp.concatenate([ang, ang], axis=-1)

    a128 = angles(A_DQK)
    half = jnp.arange(LANES) < LANES // 2
    cos128, sin128 = jnp.cos(a128), jnp.sin(a128)
    a64 = jnp.tile(angles(IDX_DH), (1, 2))
    cos64, sin64 = jnp.cos(a64), jnp.sin(a64)
    low = (jnp.arange(LANES) % IDX_DH) < IDX_DH // 2
    return jnp.stack([
        cos128, jnp.where(half, -sin128, sin128),
        cos64, jnp.where(low, -sin64, 0.0), jnp.where(low, 0.0, sin64),
    ])


def _gain_rows(a_qnorm_l, a_knorm_l, b_vnorm_l, c_qnorm_l, c_knorm_l):
    rep = PROJ_TILE // LANES
    one = jnp.ones((PROJ_TILE,), F32)
    rows = [one] * N_PROJ_TILES
    rows[T_AQ] = rows[T_AQ + 1] = jnp.tile(a_qnorm_l, rep)
    rows[T_AK] = rows[T_AK + 1] = jnp.tile(a_knorm_l, rep)
    rows[T_BV] = b_vnorm_l
    rows[T_CQ] = jnp.tile(c_qnorm_l, rep)
    rows[T_CK] = jnp.tile(c_knorm_l, rep)
    return jnp.stack(rows)[:, None, :]


def kernel(x, c, w_mod, b_mod, w_in, w_out, a_qnorm, a_knorm, a_lambda, a_subln, b_vnorm, b_ws,
           b_bias, c_qnorm, c_knorm, w_ff1, w_ff2):
    B, S, D = x.shape
    L = w_mod.shape[0]
    assert D == D_MODEL and S % Q_BLOCK == 0 and w_in.shape[-1] == N_IN

    rope = _rope_tables(S)
    w_in_b = jnp.pad(w_in, ((0, 0), (0, 0), (0, N_PROJ - N_IN))).astype(BF16)
    w_out_b = w_out.astype(BF16)
    w_ff1_b = w_ff1.astype(BF16)
    w_ff2_b = w_ff2.astype(BF16)

    mod = _modulation(c, w_mod, b_mod).reshape(L, B, 6, D)
    x2 = x.reshape(B * S, D)
    for l in range(L):
        lambda_init = 0.8 - 0.6 * math.exp(-0.3 * l)
        gains = _gain_rows(a_qnorm[l], a_knorm[l], b_vnorm[l], c_qnorm[l], c_knorm[l])
        pb = _inproj(x2, mod[l], w_in_b, l, gains, rope, S)
        oa = _diffattn(pb, a_lambda[l], a_subln[l], lambda_init, B, S)
        ob = _gmlp(pb, b_ws[l], b_bias[l], S)
        oc = _dsa(pb, B, S)
        x2 = _outproj(x2, mod[l], oa, ob, oc, w_out_b, l, S)
        x2 = _ffn(x2, mod[l], w_ff1_b, w_ff2_b, l, S)
    return x2.reshape(B, S, D)
```

```python
import functools
import math

import jax
import jax.numpy as jnp
import numpy as np
from jax import lax
from jax.experimental import pallas as pl
from jax.experimental.pallas import tpu as pltpu

D_MODEL = 2048
CHUNK = 64
ROPE_THETA = 10000.0
EPS = 1e-6
NEG = -1e30
D_FF = 4 * D_MODEL
A_HEADS = 4
A_DQK = D_MODEL // 16
A_DV = 2 * A_DQK
A_WIDTH = A_HEADS * A_DV
B_GROUPS = 4
B_CH = D_MODEL // 16
B_WIDTH = B_GROUPS * B_CH
B_BLOCK = 128
C_HEADS = 4
C_DH = D_MODEL // 16
C_WIDTH = C_HEADS * C_DH
IDX_HEADS = 8
IDX_DH = D_MODEL // 32
TOPK_MAX = 256
Q_BLOCK = 128

LANES = 128
VMEM_LIMIT_BYTES = 56 * 1024 * 1024

PROJ_TILE = 512
T_AQ, T_AK, T_AV, T_BU, T_BV, T_CQ, T_CK, T_CV, T_IQ, T_MISC = 0, 2, 4, 6, 7, 8, 9, 10, 11, 12
N_PROJ_TILES = 13
N_IN = 3 * A_WIDTH + 2 * B_WIDTH + 3 * C_WIDTH + IDX_HEADS * IDX_DH + IDX_DH + IDX_HEADS
N_PROJ = N_PROJ_TILES * PROJ_TILE
IW_LANE0 = IDX_DH
IW_SCALE = IDX_HEADS ** -0.5 * IDX_DH ** -0.5

BF16 = jnp.bfloat16
F32 = jnp.float32
NT_DIMS = (((1,), (1,)), ((), ()))
LOG2_E = math.log2(math.e)


def _cparams(*sem):
    return pltpu.CompilerParams(dimension_semantics=sem, vmem_limit_bytes=VMEM_LIMIT_BYTES)


def _rms(v, gain=None):
    y = v * lax.rsqrt(jnp.mean(v * v, axis=-1, keepdims=True) + EPS)
    return y if gain is None else y * gain


def _mod_kernel(c_ref, w_ref, b_ref, o_ref):
    @pl.when(pl.program_id(1) == 0)
    def _():
        o_ref[...] = jnp.broadcast_to(b_ref[...], o_ref.shape)

    ca = jax.nn.silu(c_ref[...]).astype(BF16)
    o_ref[...] += jnp.dot(ca, w_ref[...].astype(BF16), preferred_element_type=F32)


def _modulation(c, w_mod, b_mod):
    L, D, E = w_mod.shape
    B = c.shape[0]
    td = 256
    return pl.pallas_call(
        _mod_kernel,
        grid=(L, D // td),
        in_specs=[
            pl.BlockSpec((B, td), lambda l, k: (0, k)),
            pl.BlockSpec((None, td, E), lambda l, k: (l, k, 0)),
            pl.BlockSpec((None, 1, E), lambda l, k: (l, 0, 0)),
        ],
        out_specs=pl.BlockSpec((None, B, E), lambda l, k: (l, 0, 0)),
        out_shape=jax.ShapeDtypeStruct((L, B, E), F32),
        compiler_params=_cparams("parallel", "arbitrary"),
        name="modulation",
    )(c, w_mod, b_mod.reshape(L, 1, E))


def _inproj_kernel(x_ref, mod_ref, w_ref, gain_ref, rope_ref, o_ref, h_ref):
    y = _rms(x_ref[...])
    h_ref[...] = (y * (1.0 + mod_ref[0, 1:2, :]) + mod_ref[0, 0:1, :]).astype(BF16)
    groups = PROJ_TILE // LANES

    def rope128(v):
        return v * rope_ref[0] + pltpu.roll(v, LANES // 2, 1) * rope_ref[1]

    def rope64(v):
        return (v * rope_ref[2] + pltpu.roll(v, 96, 1) * rope_ref[3]
                + pltpu.roll(v, 32, 1) * rope_ref[4])

    def misc(v, g):
        if g > 0:
            return v
        lane = lax.broadcasted_iota(jnp.int32, v.shape, 1)
        return jnp.where(lane < IDX_DH, rope64(v), v * IW_SCALE)

    def gain(t, g):
        return gain_ref[t, :, g * LANES:(g + 1) * LANES]

    norm_rope = lambda v, t, g: rope128(_rms(v, gain(t, g)))
    epilogue = {t: (lambda v, t, g: v) for t in range(N_PROJ_TILES)}
    epilogue.update({t: norm_rope for t in (T_AQ, T_AQ + 1, T_AK, T_AK + 1, T_CQ, T_CK)})
    epilogue[T_BU] = lambda v, t, g: jax.nn.gelu(v)
    epilogue[T_BV] = lambda v, t, g: _rms(jax.nn.gelu(v), gain(t, g))
    epilogue[T_IQ] = lambda v, t, g: rope64(v)
    epilogue[T_MISC] = lambda v, t, g: misc(v, g)

    for t in range(N_PROJ_TILES):
        acc = jnp.dot(h_ref[...], w_ref[:, t * PROJ_TILE:(t + 1) * PROJ_TILE],
                      preferred_element_type=F32)
        for g in range(groups):
            lo = t * PROJ_TILE + g * LANES
            o_ref[:, lo:lo + LANES] = epilogue[t](
                acc[:, g * LANES:(g + 1) * LANES], t, g).astype(o_ref.dtype)


def _inproj(x2, mod_l, w_all, layer, gains, rope, S):
    M, D = x2.shape
    tm = min(256, S)
    nrow = S // tm
    return pl.pallas_call(
        _inproj_kernel,
        grid=(M // tm,),
        in_specs=[
            pl.BlockSpec((tm, D), lambda i: (i, 0)),
            pl.BlockSpec((1, 6, D), lambda i: (i // nrow, 0, 0)),
            pl.BlockSpec((None, D, N_PROJ), lambda i: (layer, 0, 0), pipeline_mode=pl.Buffered(1)),
            pl.BlockSpec((N_PROJ_TILES, 1, PROJ_TILE), lambda i: (0, 0, 0)),
            pl.BlockSpec((5, tm, LANES), lambda i: (0, i % nrow, 0)),
        ],
        out_specs=pl.BlockSpec((tm, N_PROJ), lambda i: (i, 0)),
        out_shape=jax.ShapeDtypeStruct((M, N_PROJ), BF16),
        scratch_shapes=[pltpu.VMEM((tm, D), BF16)],
        compiler_params=_cparams("parallel"),
        name="inproj",
    )(x2, mod_l, w_all, gains, rope)


def _chunk_mask(row0, rows, cols):
    r = row0 + lax.broadcasted_iota(jnp.int32, (rows, cols), 0)
    c = lax.broadcasted_iota(jnp.int32, (rows, cols), 1)
    return (c // CHUNK) <= (r // CHUNK)


def _diffattn_kernel(lam_ref, subln_ref, q_ref, k_ref, v_ref, o_ref, *, lambda_init, widths):
    tq = q_ref.shape[0]
    half = tq // 2
    scale = A_DQK ** -0.5
    step = pl.program_id(2)
    lp = lam_ref[...]
    lam = (jnp.exp(jnp.sum(lp[0:1] * lp[1:2], axis=-1, keepdims=True))
           - jnp.exp(jnp.sum(lp[2:3] * lp[3:4], axis=-1, keepdims=True)) + lambda_init)

    diag_mask = _chunk_mask(0, half, half)

    def body(W):
        extent = lambda r: W - (1 - r) * half

        def scores(m, r):
            s = lax.dot_general(q_ref[r * half:(r + 1) * half, m * A_DQK:(m + 1) * A_DQK],
                                k_ref[0:extent(r), m * A_DQK:(m + 1) * A_DQK],
                                NT_DIMS, preferred_element_type=F32)
            diag = jnp.where(diag_mask, s[:, extent(r) - half:], NEG)
            return diag if extent(r) == half else jnp.concatenate([s[:, :extent(r) - half], diag], axis=1)

        def numerator(s):
            e = jnp.exp2((s - jnp.max(s, axis=-1, keepdims=True)) * (scale * LOG2_E))
            return e.astype(BF16), jnp.sum(e, axis=-1, keepdims=True)

        ss = [[scores(m, r) for m in range(2)] for r in range(2)]
        outs = []
        for r in range(2):
            (e1, l1), (e2, l2) = numerator(ss[r][0]), numerator(ss[r][1])
            o1 = jnp.dot(e1, v_ref[0:extent(r), :], preferred_element_type=F32)
            o2 = jnp.dot(e2, v_ref[0:extent(r), :], preferred_element_type=F32)
            outs.append(o1 * (1.0 / l1) - o2 * (lam / l2))
        o = jnp.concatenate(outs, axis=0)
        o_ref[...] = (_rms(o, subln_ref[...]) * (1.0 - lambda_init)).astype(o_ref.dtype)

    for ci, W in enumerate(widths):
        pl.when(step == ci)(functools.partial(body, W))


def _diffattn(pb, a_lambda_l, a_subln_l, lambda_init, B, S):
    tq = min(512, S)
    nq = S // tq
    return pl.pallas_call(
        functools.partial(_diffattn_kernel, lambda_init=lambda_init,
                          widths=tuple(tq * (i + 1) for i in range(nq))),
        grid=(B, A_HEADS, nq),
        in_specs=[
            pl.BlockSpec((4, A_DQK), lambda b, h, i: (0, 0)),
            pl.BlockSpec((1, A_DV), lambda b, h, i: (0, 0)),
            pl.BlockSpec((tq, A_DV), lambda b, h, i: (b * nq + i, h)),
            pl.BlockSpec((S, A_DV), lambda b, h, i: (b, A_HEADS + h)),
            pl.BlockSpec((S, A_DV), lambda b, h, i: (b, 2 * A_HEADS + h)),
        ],
        out_specs=pl.BlockSpec((tq, A_DV), lambda b, h, i: (b * nq + i, h)),
        out_shape=jax.ShapeDtypeStruct((B * S, A_WIDTH), BF16),
        compiler_params=_cparams("parallel", "parallel", "arbitrary"),
        name="diffattn",
    )(a_lambda_l, a_subln_l.reshape(1, A_DV), pb, pb, pb)


def _gmlp_kernel(u_ref, v_ref, ws_ref, bias_ref, o_ref):
    mask = _chunk_mask(0, B_BLOCK, B_BLOCK)
    for g in range(B_GROUPS):
        w = jnp.where(mask, ws_ref[g], 0.0).astype(BF16)
        cols = slice(g * B_CH, (g + 1) * B_CH)
        for n in range(u_ref.shape[0] // B_BLOCK):
            rows = slice(n * B_BLOCK, (n + 1) * B_BLOCK)
            z = jnp.dot(w, v_ref[rows, cols], preferred_element_type=F32) + bias_ref[:, g:g + 1]
            o_ref[rows, cols] = (u_ref[rows, cols].astype(F32) * z).astype(o_ref.dtype)


def _gmlp(pb, ws_l, bias_l, S):
    M = pb.shape[0]
    tm = min(512, S)
    return pl.pallas_call(
        _gmlp_kernel,
        grid=(M // tm,),
        in_specs=[
            pl.BlockSpec((tm, B_WIDTH), lambda i: (i, T_BU)),
            pl.BlockSpec((tm, B_WIDTH), lambda i: (i, T_BV)),
            pl.BlockSpec((B_GROUPS, B_BLOCK, B_BLOCK), lambda i: (0, 0, 0)),
            pl.BlockSpec((B_BLOCK, B_GROUPS), lambda i: (0, 0)),
        ],
        out_specs=pl.BlockSpec((tm, B_WIDTH), lambda i: (i, 0)),
        out_shape=jax.ShapeDtypeStruct((M, B_WIDTH), BF16),
        compiler_params=_cparams("parallel"),
        name="gmlp",
    )(pb, pb, ws_l, bias_l.T)


def _key_to_float(key):
    return lax.bitcast_convert_type(jnp.where(key < 0, key ^ jnp.int32(0x7FFFFFFF), key), F32)


def _dsa_kernel(q_ref, iq_ref, mq_ref, k_ref, v_ref, mk_ref, o_ref, isc_ref, p_ref, *, topk, widths):
    tq = q_ref.shape[0]
    half = tq // 2
    S = k_ref.shape[0]
    scale = C_DH ** -0.5
    int_min = jnp.int32(-2 ** 31)
    step = pl.program_id(1)
    lane = lax.broadcasted_iota(jnp.int32, (half, LANES), 1)
    zero = jnp.zeros((half, LANES), jnp.int32)
    ones = jnp.ones((LANES, LANES), BF16)

    def body(W):
        subs =[(slice(0, half), W - half), (slice(half, tq), W)]

        def chunk(rows, c):
            return isc_ref[rows, c * LANES:(c + 1) * LANES]

        def count(rows, extent, pred):
            acc = jnp.zeros((half, LANES), F32)
            for c in range(extent // LANES):
                acc = acc + jnp.where(pred(chunk(rows, c), c), 1.0, 0.0)
            return jnp.dot(acc.astype(BF16), ones, preferred_element_type=F32)

        for r, (rows, extent) in enumerate(subs):
            ik = mk_ref[0:extent, 0:IDX_DH]
            isc = jnp.zeros((half, extent), F32)
            for h in range(IDX_HEADS):
                d = lax.dot_general(iq_ref[rows, h * IDX_DH:(h + 1) * IDX_DH], ik, NT_DIMS,
                                    preferred_element_type=F32)
                isc = isc + (jnp.maximum(d, 0.0)
                             * mq_ref[rows, IW_LANE0 + h:IW_LANE0 + h + 1].astype(F32))
            isc_ref[rows, 0:extent] = jnp.where(
                _chunk_mask(step * tq + r * half, half, extent), isc, NEG)

        def value_step(i, t_us):
            out = []
            for (rows, extent), t_u in zip(subs, t_us):
                cand_u = t_u | lax.shift_left(jnp.int32(1), jnp.int32(31) - i)
                cand_f = _key_to_float(cand_u ^ int_min)
                cnt = (count(rows, extent, lambda ch, c: ch >= cand_f)
                       + jnp.where(cand_f <= NEG, float(S - extent), 0.0))
                out.append(jnp.where(cnt >= topk, cand_u, t_u))
            return tuple(out)

        thrs = [_key_to_float(t_u ^ int_min) for t_u in lax.fori_loop(0, 32, value_step, (zero, zero))]
        needs = [topk - count(rows, extent, lambda ch, c: ch > thr)
                 for (rows, extent), thr in zip(subs, thrs)]
        n_ties = [count(rows, extent, lambda ch, c: ch == thr)
                  for (rows, extent), thr in zip(subs, thrs)]

        p_ref[...] = jnp.full((tq, LANES), S - 1, jnp.int32)
        surplus = jnp.concatenate(
            [jnp.where(n > need, 1.0, 0.0) for n, need in zip(n_ties, needs)], axis=0)

        @pl.when(jnp.max(surplus) > 0.0)
        def _():
            for (rows, extent), thr, need in zip(subs, thrs, needs):
                def index_step(i, p):
                    cand = p | lax.shift_right_logical(jnp.int32(S // 2), i)
                    ties_below = count(
                        rows, extent, lambda ch, c: (ch == thr) & (lane + c * LANES < cand))
                    return jnp.where(ties_below < need, cand, p)
                p_ref[rows, :] = lax.fori_loop(0, int(math.log2(S)), index_step, zero)

        for (rows, extent), thr in zip(subs, thrs):
            p_last = p_ref[rows, :]
            for c in range(extent // LANES):
                ch = chunk(rows, c)
                sel = ((ch > thr) | ((ch == thr) & (lane + c * LANES <= p_last))) & (ch > 0.5 * NEG)
                isc_ref[rows, c * LANES:(c + 1) * LANES] = jnp.where(sel, 0.0, NEG)

        for h in range(C_HEADS):
            sl = slice(h * C_DH, (h + 1) * C_DH)
            for rows, extent in subs:
                s = lax.dot_general(q_ref[rows, sl], k_ref[0:extent, sl], NT_DIMS,
                                    preferred_element_type=F32)
                s = s + isc_ref[rows, 0:extent]
                e = jnp.exp2((s - jnp.max(s, axis=-1, keepdims=True)) * (scale * LOG2_E))
                o = jnp.dot(e.astype(BF16), v_ref[0:extent, sl], preferred_element_type=F32)
                o_ref[rows, sl] = (o * (1.0 / jnp.sum(e, axis=-1, keepdims=True))).astype(o_ref.dtype)

    for ci, W in enumerate(widths):
        pl.when(step == ci)(functools.partial(body, W))


def _dsa(pb, B, S):
    tq = min(512, S)
    nq = S // tq
    topk = min(TOPK_MAX, S // 4)
    widths = tuple(tq * (i + 1) for i in range(nq))
    qspec = lambda t: pl.BlockSpec((tq, PROJ_TILE), lambda b, i: (b * nq + i, t))
    kspec = lambda t: pl.BlockSpec((S, PROJ_TILE), lambda b, i: (b, t))
    return pl.pallas_call(
        functools.partial(_dsa_kernel, topk=topk, widths=widths),
        grid=(B, nq),
        in_specs=[qspec(T_CQ), qspec(T_IQ), qspec(T_MISC), kspec(T_CK), kspec(T_CV), kspec(T_MISC)],
        out_specs=pl.BlockSpec((tq, C_WIDTH), lambda b, i: (b * nq + i, 0)),
        out_shape=jax.ShapeDtypeStruct((B * S, C_WIDTH), BF16),
        scratch_shapes=[pltpu.VMEM((tq, S), F32), pltpu.VMEM((tq, LANES), jnp.int32)],
        compiler_params=_cparams("parallel", "arbitrary"),
        name="dsa",
    )(pb, pb, pb, pb, pb, pb)


def _outproj_kernel(x_ref, mod_ref, oa_ref, ob_ref, oc_ref, w_ref, o_ref):
    y = jnp.dot(oa_ref[...], w_ref[0:A_WIDTH, :], preferred_element_type=F32)
    y = y + jnp.dot(ob_ref[...], w_ref[A_WIDTH:A_WIDTH + B_WIDTH, :], preferred_element_type=F32)
    y = y + jnp.dot(oc_ref[...], w_ref[A_WIDTH + B_WIDTH:, :], preferred_element_type=F32)
    o_ref[...] = x_ref[...] + mod_ref[0, 2:3, :] * y


def _outproj(x2, mod_l, oa, ob, oc, w_all, layer, S):
    M, D = x2.shape
    tm = min(512, S)
    nrow = S // tm
    row = lambda width: pl.BlockSpec((tm, width), lambda i: (i, 0))
    return pl.pallas_call(
        _outproj_kernel,
        grid=(M // tm,),
        in_specs=[
            row(D),
            pl.BlockSpec((1, 6, D), lambda i: (i // nrow, 0, 0)),
            row(A_WIDTH), row(B_WIDTH), row(C_WIDTH),
            pl.BlockSpec((None, D, D), lambda i: (layer, 0, 0)),
        ],
        out_specs=row(D),
        out_shape=jax.ShapeDtypeStruct((M, D), F32),
        compiler_params=_cparams("parallel"),
        name="outproj",
    )(x2, mod_l, oa, ob, oc, w_all)


def _ffn_kernel(x_ref, mod_ref, w1_ref, w2_ref, o_ref, h_ref, acc_ref):
    j = pl.program_id(1)

    @pl.when(j == 0)
    def _():
        y = _rms(x_ref[...])
        h_ref[...] = (y * (1.0 + mod_ref[0, 4:5, :]) + mod_ref[0, 3:4, :]).astype(BF16)
        acc_ref[...] = jnp.zeros_like(acc_ref)

    hid = jnp.square(jnp.maximum(jnp.dot(h_ref[...], w1_ref[...], preferred_element_type=F32), 0.0))
    acc_ref[...] += jnp.dot(hid.astype(BF16), w2_ref[...], preferred_element_type=F32)

    @pl.when(j == pl.num_programs(1) - 1)
    def _():
        o_ref[...] = x_ref[...] + mod_ref[0, 5:6, :] * acc_ref[...]


def _ffn(x2, mod_l, w1_all, w2_all, layer, S):
    M, D = x2.shape
    F = w1_all.shape[-1]
    tm = min(512, S)
    tf = 1024
    nrow = S // tm
    return pl.pallas_call(
        _ffn_kernel,
        grid=(M // tm, F // tf),
        in_specs=[
            pl.BlockSpec((tm, D), lambda i, j: (i, 0)),
            pl.BlockSpec((1, 6, D), lambda i, j: (i // nrow, 0, 0)),
            pl.BlockSpec((None, D, tf), lambda i, j: (layer, 0, j)),
            pl.BlockSpec((None, tf, D), lambda i, j: (layer, j, 0)),
        ],
        out_specs=pl.BlockSpec((tm, D), lambda i, j: (i, 0)),
        out_shape=jax.ShapeDtypeStruct((M, D), F32),
        scratch_shapes=[pltpu.VMEM((tm, D), BF16), pltpu.VMEM((tm, D), F32)],
        compiler_params=_cparams("parallel", "arbitrary"),
        name="ffn",
    )(x2, mod_l, w1_all, w2_all)


def _rope_tables(S):
    pos = jnp.arange(S, dtype=F32)[:, None]

    def angles(dim):
        inv = 1.0 / (ROPE_THETA ** (jnp.arange(0, dim, 2, dtype=F32) / dim))
        ang = pos * inv[None, :]
        return jnp.concatenate([ang, ang], axis=-1)

    a128 = angles(A_DQK)
    half = jnp.arange(LANES) < LANES // 2
    cos128, sin128 = jnp.cos(a128), jnp.sin(a128)
    a64 = jnp.tile(angles(IDX_DH), (1, 2))
    cos64, sin64 = jnp.cos(a64), jnp.sin(a64)
    low = (jnp.arange(LANES) % IDX_DH) < IDX_DH // 2
    return jnp.stack([
        cos128, jnp.where(half, -sin128, sin128),
        cos64, jnp.where(low, -sin64, 0.0), jnp.where(low, 0.0, sin64),
    ])


def _gain_rows(a_qnorm_l, a_knorm_l, b_vnorm_l, c_qnorm_l, c_knorm_l):
    rep = PROJ_TILE // LANES
    one = jnp.ones((PROJ_TILE,), F32)
    rows = [one] * N_PROJ_TILES
    rows[T_AQ] = rows[T_AQ + 1] = jnp.tile(a_qnorm_l, rep)
    rows[T_AK] = rows[T_AK + 1] = jnp.tile(a_knorm_l, rep)
    rows[T_BV] = b_vnorm_l
    rows[T_CQ] = jnp.tile(c_qnorm_l, rep)
    rows[T_CK] = jnp.tile(c_knorm_l, rep)
    return jnp.stack(rows)[:, None, :]


def kernel(x, c, w_mod, b_mod, w_in, w_out, a_qnorm, a_knorm, a_lambda, a_subln, b_vnorm, b_ws,
           b_bias, c_qnorm, c_knorm, w_ff1, w_ff2):
    B, S, D = x.shape
    L = w_mod.shape[0]
    assert D == D_MODEL and S % Q_BLOCK == 0 and w_in.shape[-1] == N_IN

    rope = _rope_tables(S)
    w_in_b = jnp.pad(w_in, ((0, 0), (0, 0), (0, N_PROJ - N_IN))).astype(BF16)
    w_out_b = w_out.astype(BF16)
    w_ff1_b = w_ff1.astype(BF16)
    w_ff2_b = w_ff2.astype(BF16)

    mod = _modulation(c, w_mod, b_mod).reshape(L, B, 6, D)
    x2 = x.reshape(B * S, D)
    for l in range(L):
        lambda_init = 0.8 - 0.6 * math.exp(-0.3 * l)
        gains = _gain_rows(a_qnorm[l], a_knorm[l], b_vnorm[l], c_qnorm[l], c_knorm[l])
        pb = _inproj(x2, mod[l], w_in_b, l, gains, rope, S)
        oa = _diffattn(pb, a_lambda[l], a_subln[l], lambda_init, B, S)
        ob = _gmlp(pb, b_ws[l], b_bias[l], S)
        oc = _dsa(pb, B, S)
        x2 = _outproj(x2, mod[l], oa, ob, oc, w_out_b, l, S)
        x2 = _ffn(x2, mod[l], w_ff1_b, w_ff2_b, l, S)
    return x2.reshape(B, S, D)
```

```python
import functools
import math

import jax
import jax.numpy as jnp
import numpy as np
from jax import lax
from jax.experimental import pallas as pl
from jax.experimental.pallas import tpu as pltpu

D_MODEL = 2048
CHUNK = 64
ROPE_THETA = 10000.0
EPS = 1e-6
NEG = -1e30
D_FF = 4 * D_MODEL
A_HEADS = 4
A_DQK = D_MODEL // 16
A_DV = 2 * A_DQK
A_WIDTH = A_HEADS * A_DV
B_GROUPS = 4
B_CH = D_MODEL // 16
B_WIDTH = B_GROUPS * B_CH
B_BLOCK = 128
C_HEADS = 4
C_DH = D_MODEL // 16
C_WIDTH = C_HEADS * C_DH
IDX_HEADS = 8
IDX_DH = D_MODEL // 32
TOPK_MAX = 256
Q_BLOCK = 128

LANES = 128
VMEM_LIMIT_BYTES = 56 * 1024 * 1024

PROJ_TILE = 512
T_AQ, T_AK, T_AV, T_BU, T_BV, T_CQ, T_CK, T_CV, T_IQ, T_MISC = 0, 2, 4, 6, 7, 8, 9, 10, 11, 12
N_PROJ_TILES = 13
N_IN = 3 * A_WIDTH + 2 * B_WIDTH + 3 * C_WIDTH + IDX_HEADS * IDX_DH + IDX_DH + IDX_HEADS
N_PROJ = N_PROJ_TILES * PROJ_TILE
IW_LANE0 = IDX_DH
IW_SCALE = IDX_HEADS ** -0.5 * IDX_DH ** -0.5

BF16 = jnp.bfloat16
F32 = jnp.float32
NT_DIMS = (((1,), (1,)), ((), ()))
LOG2_E = math.log2(math.e)


def _cparams(*sem):
    return pltpu.CompilerParams(dimension_semantics=sem, vmem_limit_bytes=VMEM_LIMIT_BYTES)


def _rms(v, gain=None):
    y = v * lax.rsqrt(jnp.mean(v * v, axis=-1, keepdims=True) + EPS)
    return y if gain is None else y * gain


def _mod_kernel(c_ref, w_ref, b_ref, o_ref):
    @pl.when(pl.program_id(1) == 0)
    def _():
        o_ref[...] = jnp.broadcast_to(b_ref[...], o_ref.shape)

    ca = jax.nn.silu(c_ref[...]).astype(BF16)
    o_ref[...] += jnp.dot(ca, w_ref[...].astype(BF16), preferred_element_type=F32)


def _modulation(c, w_mod, b_mod):
    L, D, E = w_mod.shape
    B = c.shape[0]
    td = 256
    return pl.pallas_call(
        _mod_kernel,
        grid=(L, D // td),
        in_specs=[
            pl.BlockSpec((B, td), lambda l, k: (0, k)),
            pl.BlockSpec((None, td, E), lambda l, k: (l, k, 0)),
            pl.BlockSpec((None, 1, E), lambda l, k: (l, 0, 0)),
        ],
        out_specs=pl.BlockSpec((None, B, E), lambda l, k: (l, 0, 0)),
        out_shape=jax.ShapeDtypeStruct((L, B, E), F32),
        compiler_params=_cparams("parallel", "arbitrary"),
        name="modulation",
    )(c, w_mod, b_mod.reshape(L, 1, E))


def _inproj_kernel(x_ref, mod_ref, w_ref, gain_ref, rope_ref, o_ref, h_ref):
    y = _rms(x_ref[...])
    h_ref[...] = (y * (1.0 + mod_ref[0, 1:2, :]) + mod_ref[0, 0:1, :]).astype(BF16)
    groups = PROJ_TILE // LANES

    def rope128(v):
        return v * rope_ref[0] + pltpu.roll(v, LANES // 2, 1) * rope_ref[1]

    def rope64(v):
        return (v * rope_ref[2] + pltpu.roll(v, 96, 1) * rope_ref[3]
                + pltpu.roll(v, 32, 1) * rope_ref[4])

    def misc(v, g):
        if g > 0:
            return v
        lane = lax.broadcasted_iota(jnp.int32, v.shape, 1)
        return jnp.where(lane < IDX_DH, rope64(v), v * IW_SCALE)

    def gain(t, g):
        return gain_ref[t, :, g * LANES:(g + 1) * LANES]

    norm_rope = lambda v, t, g: rope128(_rms(v, gain(t, g)))
    epilogue = {t: (lambda v, t, g: v) for t in range(N_PROJ_TILES)}
    epilogue.update({t: norm_rope for t in (T_AQ, T_AQ + 1, T_AK, T_AK + 1, T_CQ, T_CK)})
    epilogue[T_BU] = lambda v, t, g: jax.nn.gelu(v)
    epilogue[T_BV] = lambda v, t, g: _rms(jax.nn.gelu(v), gain(t, g))
    epilogue[T_IQ] = lambda v, t, g: rope64(v)
    epilogue[T_MISC] = lambda v, t, g: misc(v, g)

    for t in range(N_PROJ_TILES):
        acc = jnp.dot(h_ref[...], w_ref[:, t * PROJ_TILE:(t + 1) * PROJ_TILE],
                      preferred_element_type=F32)
        for g in range(groups):
            lo = t * PROJ_TILE + g * LANES
            o_ref[:, lo:lo + LANES] = epilogue[t](
                acc[:, g * LANES:(g + 1) * LANES], t, g).astype(o_ref.dtype)


def _inproj(x2, mod_l, w_all, layer, gains, rope, S):
    M, D = x2.shape
    tm = min(256, S)
    nrow = S // tm
    return pl.pallas_call(
        _inproj_kernel,
        grid=(M // tm,),
        in_specs=[
            pl.BlockSpec((tm, D), lambda i: (i, 0)),
            pl.BlockSpec((1, 6, D), lambda i: (i // nrow, 0, 0)),
            pl.BlockSpec((None, D, N_PROJ), lambda i: (layer, 0, 0), pipeline_mode=pl.Buffered(1)),
            pl.BlockSpec((N_PROJ_TILES, 1, PROJ_TILE), lambda i: (0, 0, 0)),
            pl.BlockSpec((5, tm, LANES), lambda i: (0, i % nrow, 0)),
        ],
        out_specs=pl.BlockSpec((tm, N_PROJ), lambda i: (i, 0)),
        out_shape=jax.ShapeDtypeStruct((M, N_PROJ), BF16),
        scratch_shapes=[pltpu.VMEM((tm, D), BF16)],
        compiler_params=_cparams("parallel"),
        name="inproj",
    )(x2, mod_l, w_all, gains, rope)


def _chunk_mask(row0, rows, cols):
    r = row0 + lax.broadcasted_iota(jnp.int32, (rows, cols), 0)
    c = lax.broadcasted_iota(jnp.int32, (rows, cols), 1)
    return (c // CHUNK) <= (r // CHUNK)


def _diffattn_kernel(lam_ref, subln_ref, q_ref, k_ref, v_ref, o_ref, *, lambda_init, widths):
    tq = q_ref.shape[0]
    half = tq // 2
    scale = A_DQK ** -0.5
    step = pl.program_id(0)
    lp = lam_ref[...]
    lam = (jnp.exp(jnp.sum(lp[0:1] * lp[1:2], axis=-1, keepdims=True))
           - jnp.exp(jnp.sum(lp[2:3] * lp[3:4], axis=-1, keepdims=True)) + lambda_init)

    diag_mask = _chunk_mask(0, half, half)

    def body(W):
        extent = lambda r: W - (1 - r) * half

        def scores(m, r):
            s = lax.dot_general(q_ref[r * half:(r + 1) * half, m * A_DQK:(m + 1) * A_DQK],
                                k_ref[0:extent(r), m * A_DQK:(m + 1) * A_DQK],
                                NT_DIMS, preferred_element_type=F32)
            diag = jnp.where(diag_mask, s[:, extent(r) - half:], NEG)
            return diag if extent(r) == half else jnp.concatenate([s[:, :extent(r) - half], diag], axis=1)

        def numerator(s):
            e = jnp.exp2((s - jnp.max(s, axis=-1, keepdims=True)) * (scale * LOG2_E))
            return e.astype(BF16), jnp.sum(e, axis=-1, keepdims=True)

        ss = [[scores(m, r) for m in range(2)] for r in range(2)]
        outs = []
        for r in range(2):
            (e1, l1), (e2, l2) = numerator(ss[r][0]), numerator(ss[r][1])
            o1 = jnp.dot(e1, v_ref[0:extent(r), :], preferred_element_type=F32)
            o2 = jnp.dot(e2, v_ref[0:extent(r), :], preferred_element_type=F32)
            outs.append(o1 * (1.0 / l1) - o2 * (lam / l2))
        o = jnp.concatenate(outs, axis=0)
        o_ref[...] = (_rms(o, subln_ref[...]) * (1.0 - lambda_init)).astype(o_ref.dtype)

    for ci, W in enumerate(widths):
        pl.when(step == ci)(functools.partial(body, W))


def _diffattn(pb, a_lambda_l, a_subln_l, lambda_init, B, S):
    tq = min(512, S)
    nq = S // tq
    return pl.pallas_call(
        functools.partial(_diffattn_kernel, lambda_init=lambda_init,
                          widths=tuple(tq * (i + 1) for i in range(nq))),
        grid=(nq, B, A_HEADS),
        in_specs=[
            pl.BlockSpec((4, A_DQK), lambda i, b, h: (0, 0)),
            pl.BlockSpec((1, A_DV), lambda i, b, h: (0, 0)),
            pl.BlockSpec((tq, A_DV), lambda i, b, h: (b * nq + i, h)),
            pl.BlockSpec((S, A_DV), lambda i, b, h: (b, A_HEADS + h)),
            pl.BlockSpec((S, A_DV), lambda i, b, h: (b, 2 * A_HEADS + h)),
        ],
        out_specs=pl.BlockSpec((tq, A_DV), lambda i, b, h: (b * nq + i, h)),
        out_shape=jax.ShapeDtypeStruct((B * S, A_WIDTH), BF16),
        compiler_params=_cparams("parallel", "parallel", "arbitrary"),
        name="diffattn",
    )(a_lambda_l, a_subln_l.reshape(1, A_DV), pb, pb, pb)


def _gmlp_kernel(u_ref, v_ref, ws_ref, bias_ref, o_ref):
    mask = _chunk_mask(0, B_BLOCK, B_BLOCK)
    for g in range(B_GROUPS):
        w = jnp.where(mask, ws_ref[g], 0.0).astype(BF16)
        cols = slice(g * B_CH, (g + 1) * B_CH)
        for n in range(u_ref.shape[0] // B_BLOCK):
            rows = slice(n * B_BLOCK, (n + 1) * B_BLOCK)
            z = jnp.dot(w, v_ref[rows, cols], preferred_element_type=F32) + bias_ref[:, g:g + 1]
            o_ref[rows, cols] = (u_ref[rows, cols].astype(F32) * z).astype(o_ref.dtype)


def _gmlp(pb, ws_l, bias_l, S):
    M = pb.shape[0]
    tm = min(512, S)
    return pl.pallas_call(
        _gmlp_kernel,
        grid=(M // tm,),
        in_specs=[
            pl.BlockSpec((tm, B_WIDTH), lambda i: (i, T_BU)),
            pl.BlockSpec((tm, B_WIDTH), lambda i: (i, T_BV)),
            pl.BlockSpec((B_GROUPS, B_BLOCK, B_BLOCK), lambda i: (0, 0, 0)),
            pl.BlockSpec((B_BLOCK, B_GROUPS), lambda i: (0, 0)),
        ],
        out_specs=pl.BlockSpec((tm, B_WIDTH), lambda i: (i, 0)),
        out_shape=jax.ShapeDtypeStruct((M, B_WIDTH), BF16),
        compiler_params=_cparams("parallel"),
        name="gmlp",
    )(pb, pb, ws_l, bias_l.T)


def _key_to_float(key):
    return lax.bitcast_convert_type(jnp.where(key < 0, key ^ jnp.int32(0x7FFFFFFF), key), F32)


def _dsa_kernel(q_ref, iq_ref, mq_ref, k_ref, v_ref, mk_ref, o_ref, isc_ref, p_ref, *, topk, widths):
    tq = q_ref.shape[0]
    half = tq // 2
    S = k_ref.shape[0]
    scale = C_DH ** -0.5
    int_min = jnp.int32(-2 ** 31)
    step = pl.program_id(0)
    lane = lax.broadcasted_iota(jnp.int32, (half, LANES), 1)
    zero = jnp.zeros((half, LANES), jnp.int32)
    ones = jnp.ones((LANES, LANES), BF16)

    def body(W):
        subs = [(slice(0, half), W - half), (slice(half, tq), W)]

        def chunk(rows, c):
            return isc_ref[rows, c * LANES:(c + 1) * LANES]

        def count(rows, extent, pred):
            acc = jnp.zeros((half, LANES), F32)
            for c in range(extent // LANES):
                acc = acc + jnp.where(pred(chunk(rows, c), c), 1.0, 0.0)
            return jnp.dot(acc.astype(BF16), ones, preferred_element_type=F32)

        for r, (rows, extent) in enumerate(subs):
            ik = mk_ref[0:extent, 0:IDX_DH]
            isc = jnp.zeros((half, extent), F32)
            for h in range(IDX_HEADS):
                d = lax.dot_general(iq_ref[rows, h * IDX_DH:(h + 1) * IDX_DH], ik, NT_DIMS,
                                    preferred_element_type=F32)
                isc = isc + (jnp.maximum(d, 0.0)
                             * mq_ref[rows, IW_LANE0 + h:IW_LANE0 + h + 1].astype(F32))
            isc_ref[rows, 0:extent] = jnp.where(
                _chunk_mask(step * tq + r * half, half, extent), isc, NEG)

        def value_step(i, t_us):
            out = []
            for (rows, extent), t_u in zip(subs, t_us):
                cand_u = t_u | lax.shift_left(jnp.int32(1), jnp.int32(31) - i)
                cand_f = _key_to_float(cand_u ^ int_min)
                cnt = (count(rows, extent, lambda ch, c: ch >= cand_f)
                       + jnp.where(cand_f <= NEG, float(S - extent), 0.0))
                out.append(jnp.where(cnt >= topk, cand_u, t_u))
            return tuple(out)

        thrs = [_key_to_float(t_u ^ int_min) for t_u in lax.fori_loop(0, 32, value_step, (zero, zero))]
        needs = [topk - count(rows, extent, lambda ch, c: ch > thr)
                 for (rows, extent), thr in zip(subs, thrs)]
        n_ties = [count(rows, extent, lambda ch, c: ch == thr)
                  for (rows, extent), thr in zip(subs, thrs)]

        p_ref[...] = jnp.full((tq, LANES), S - 1, jnp.int32)
        surplus = jnp.concatenate(
            [jnp.where(n > need, 1.0, 0.0) for n, need in zip(n_ties, needs)], axis=0)

        @pl.when(jnp.max(surplus) > 0.0)
        def _():
            for (rows, extent), thr, need in zip(subs, thrs, needs):
                def index_step(i, p):
                    cand = p | lax.shift_right_logical(jnp.int32(S // 2), i)
                    ties_below = count(
                        rows, extent, lambda ch, c: (ch == thr) & (lane + c * LANES < cand))
                    return jnp.where(ties_below < need, cand, p)
                p_ref[rows, :] = lax.fori_loop(0, int(math.log2(S)), index_step, zero)

        for (rows, extent), thr in zip(subs, thrs):
            p_last = p_ref[rows, :]
            for c in range(extent // LANES):
                ch = chunk(rows, c)
                sel = ((ch > thr) | ((ch == thr) & (lane + c * LANES <= p_last))) & (ch > 0.5 * NEG)
                isc_ref[rows, c * LANES:(c + 1) * LANES] = jnp.where(sel, 0.0, NEG)

        for h in range(C_HEADS):
            sl = slice(h * C_DH, (h + 1) * C_DH)
            for rows, extent in subs:
                s = lax.dot_general(q_ref[rows, sl], k_ref[0:extent, sl], NT_DIMS,
                                    preferred_element_type=F32)
                s = s + isc_ref[rows, 0:extent]
                e = jnp.exp2((s - jnp.max(s, axis=-1, keepdims=True)) * (scale * LOG2_E))
                o = jnp.dot(e.astype(BF16), v_ref[0:extent, sl], preferred_element_type=F32)
                o_ref[rows, sl] = (o * (1.0 / jnp.sum(e, axis=-1, keepdims=True))).astype(o_ref.dtype)

    for ci, W in enumerate(widths):
        pl.when(step == ci)(functools.partial(body, W))


def _dsa(pb, B, S):
    tq = min(512, S)
    nq = S // tq
    topk = min(TOPK_MAX, S // 4)
    widths = tuple(tq * (i + 1) for i in range(nq))
    qspec = lambda t: pl.BlockSpec((tq, PROJ_TILE), lambda i, b: (b * nq + i, t))
    kspec = lambda t: pl.BlockSpec((S, PROJ_TILE), lambda i, b: (b, t))
    return pl.pallas_call(
        functools.partial(_dsa_kernel, topk=topk, widths=widths),
        grid=(nq, B),
        in_specs=[qspec(T_CQ), qspec(T_IQ), qspec(T_MISC), kspec(T_CK), kspec(T_CV), kspec(T_MISC)],
        out_specs=pl.BlockSpec((tq, C_WIDTH), lambda i, b: (b * nq + i, 0)),
        out_shape=jax.ShapeDtypeStruct((B * S, C_WIDTH), BF16),
        scratch_shapes=[pltpu.VMEM((tq, S), F32), pltpu.VMEM((tq, LANES), jnp.int32)],
        compiler_params=_cparams("parallel", "arbitrary"),
        name="dsa",
    )(pb, pb, pb, pb, pb, pb)


def _outproj_kernel(x_ref, mod_ref, oa_ref, ob_ref, oc_ref, w_ref, o_ref):
    y = jnp.dot(oa_ref[...], w_ref[0:A_WIDTH, :], preferred_element_type=F32)
    y = y + jnp.dot(ob_ref[...], w_ref[A_WIDTH:A_WIDTH + B_WIDTH, :], preferred_element_type=F32)
    y = y + jnp.dot(oc_ref[...], w_ref[A_WIDTH + B_WIDTH:, :], preferred_element_type=F32)
    o_ref[...] = x_ref[...] + mod_ref[0, 2:3, :] * y


def _outproj(x2, mod_l, oa, ob, oc, w_all, layer, S):
    M, D = x2.shape
    tm = min(512, S)
    nrow = S // tm
    row = lambda width: pl.BlockSpec((tm, width), lambda i: (i, 0))
    return pl.pallas_call(
        _outproj_kernel,
        grid=(M // tm,),
        in_specs=[
            row(D),
            pl.BlockSpec((1, 6, D), lambda i: (i // nrow, 0, 0)),
            row(A_WIDTH), row(B_WIDTH), row(C_WIDTH),
            pl.BlockSpec((None, D, D), lambda i: (layer, 0, 0)),
        ],
        out_specs=row(D),
        out_shape=jax.ShapeDtypeStruct((M, D), F32),
        compiler_params=_cparams("parallel"),
        name="outproj",
    )(x2, mod_l, oa, ob, oc, w_all)


def _ffn_kernel(x_ref, mod_ref, w1_ref, w2_ref, o_ref, h_ref, acc_ref):
    j = pl.program_id(1)

    @pl.when(j == 0)
    def _():
        y = _rms(x_ref[...])
        h_ref[...] = (y * (1.0 + mod_ref[0, 4:5, :]) + mod_ref[0, 3:4, :]).astype(BF16)
        acc_ref[...] = jnp.zeros_like(acc_ref)

    hid = jnp.square(jnp.maximum(jnp.dot(h_ref[...], w1_ref[...], preferred_element_type=F32), 0.0))
    acc_ref[...] += jnp.dot(hid.astype(BF16), w2_ref[...], preferred_element_type=F32)

    @pl.when(j == pl.num_programs(1) - 1)
    def _():
        o_ref[...] = x_ref[...] + mod_ref[0, 5:6, :] * acc_ref[...]


def _ffn(x2, mod_l, w1_all, w2_all, layer, S):
    M, D = x2.shape
    F = w1_all.shape[-1]
    tm = min(512, S)
    tf = 1024
    nrow = S // tm
    return pl.pallas_call(
        _ffn_kernel,
        grid=(M // tm, F // tf),
        in_specs=[
            pl.BlockSpec((tm, D), lambda i, j: (i, 0)),
            pl.BlockSpec((1, 6, D), lambda i, j: (i // nrow, 0, 0)),
            pl.BlockSpec((None, D, tf), lambda i, j: (layer, 0, j)),
            pl.BlockSpec((None, tf, D), lambda i, j: (layer, j, 0)),
        ],
        out_specs=pl.BlockSpec((tm, D), lambda i, j: (i, 0)),
        out_shape=jax.ShapeDtypeStruct((M, D), F32),
        scratch_shapes=[pltpu.VMEM((tm, D), BF16), pltpu.VMEM((tm, D), F32)],
        compiler_params=_cparams("parallel", "arbitrary"),
        name="ffn",
    )(x2, mod_l, w1_all, w2_all)


def _rope_tables(S):
    pos = jnp.arange(S, dtype=F32)[:, None]

    def angles(dim):
        inv = 1.0 / (ROPE_THETA ** (jnp.arange(0, dim, 2, dtype=F32) / dim))
        ang = pos * inv[None, :]
        return jnp.concatenate([ang, ang], axis=-1)

    a128 = angles(A_DQK)
    half = jnp.arange(LANES) < LANES // 2
    cos128, sin128 = jnp.cos(a128), jnp.sin(a128)
    a64 = jnp.tile(angles(IDX_DH), (1, 2))
    cos64, sin64 = jnp.cos(a64), jnp.sin(a64)
    low = (jnp.arange(LANES) % IDX_DH) < IDX_DH // 2
    return jnp.stack([
        cos128, jnp.where(half, -sin128, sin128),
        cos64, jnp.where(low, -sin64, 0.0), jnp.where(low, 0.0, sin64),
    ])


def _gain_rows(a_qnorm_l, a_knorm_l, b_vnorm_l, c_qnorm_l, c_knorm_l):
    rep = PROJ_TILE // LANES
    one = jnp.ones((PROJ_TILE,), F32)
    rows = [one] * N_PROJ_TILES
    rows[T_AQ] = rows[T_AQ + 1] = jnp.tile(a_qnorm_l, rep)
    rows[T_AK] = rows[T_AK + 1] = jnp.tile(a_knorm_l, rep)
    rows[T_BV] = b_vnorm_l
    rows[T_CQ] = jnp.tile(c_qnorm_l, rep)
    rows[T_CK] = jnp.tile(c_knorm_l, rep)
    return jnp.stack(rows)[:, None, :]


def kernel(x, c, w_mod, b_mod, w_in, w_out, a_qnorm, a_knorm, a_lambda, a_subln, b_vnorm, b_ws,
           b_bias, c_qnorm, c_knorm, w_ff1, w_ff2):
    B, S, D = x.shape
    L = w_mod.shape[0]
    assert D == D_MODEL and S % Q_BLOCK == 0 and w_in.shape[-1] == N_IN

    rope = _rope_tables(S)
    w_in_b = jnp.pad(w_in, ((0, 0), (0, 0), (0, N_PROJ - N_IN))).astype(BF16)
    w_out_b = w_out.astype(BF16)
    w_ff1_b = w_ff1.astype(BF16)
    w_ff2_b = w_ff2.astype(BF16)

    mod = _modulation(c, w_mod, b_mod).reshape(L, B, 6, D)
    x2 = x.reshape(B * S, D)
    for l in range(L):
        lambda_init = 0.8 - 0.6 * math.exp(-0.3 * l)
        gains = _gain_rows(a_qnorm[l], a_knorm[l], b_vnorm[l], c_qnorm[l], c_knorm[l])
        pb = _inproj(x2, mod[l], w_in_b, l, gains, rope, S)
        oa = _diffattn(pb, a_lambda[l], a_subln[l], lambda_init, B, S)
        ob = _gmlp(pb, b_ws[l], b_bias[l], S)
        oc = _dsa(pb, B, S)
        x2 = _outproj(x2, mod[l], oa, ob, oc, w_out_b, l, S)
        x2 = _ffn(x2, mod[l], w_ff1_b, w_ff2_b, l, S)
    return x2.reshape(B, S, D)
```

```python
import functools
import math

import jax
import jax.numpy as jnp
import numpy as np
from jax import lax
from jax.experimental import pallas as pl
from jax.experimental.pallas import tpu as pltpu

D_MODEL = 2048
CHUNK = 64
ROPE_THETA = 10000.0
EPS = 1e-6
NEG = -1e30
D_FF = 4 * D_MODEL
A_HEADS = 4
A_DQK = D_MODEL // 16
A_DV = 2 * A_DQK
A_WIDTH = A_HEADS * A_DV
B_GROUPS = 4
B_CH = D_MODEL // 16
B_WIDTH = B_GROUPS * B_CH
B_BLOCK = 128
C_HEADS = 4
C_DH = D_MODEL // 16
C_WIDTH = C_HEADS * C_DH
IDX_HEADS = 8
IDX_DH = D_MODEL // 32
TOPK_MAX = 256
Q_BLOCK = 128

LANES = 128
VMEM_LIMIT_BYTES = 56 * 1024 * 1024

PROJ_TILE = 512
T_AQ, T_AK, T_AV, T_BU, T_BV, T_CQ, T_CK, T_CV, T_IQ, T_MISC = 0, 2, 4, 6, 7, 8, 9, 10, 11, 12
N_PROJ_TILES = 13
N_IN = 3 * A_WIDTH + 2 * B_WIDTH + 3 * C_WIDTH + IDX_HEADS * IDX_DH + IDX_DH + IDX_HEADS
N_PROJ = N_PROJ_TILES * PROJ_TILE
IW_LANE0 = IDX_DH
IW_SCALE = IDX_HEADS ** -0.5 * IDX_DH ** -0.5

BF16 = jnp.bfloat16
F32 = jnp.float32
NT_DIMS = (((1,), (1,)), ((), ()))
LOG2_E = math.log2(math.e)


def _cparams(*sem):
    return pltpu.CompilerParams(dimension_semantics=sem, vmem_limit_bytes=VMEM_LIMIT_BYTES)


def _rms(v, gain=None):
    y = v * lax.rsqrt(jnp.mean(v * v, axis=-1, keepdims=True) + EPS)
    return y if gain is None else y * gain


def _mod_kernel(c_ref, w_ref, b_ref, o_ref):
    @pl.when(pl.program_id(1) == 0)
    def _():
        o_ref[...] = jnp.broadcast_to(b_ref[...], o_ref.shape)

    ca = jax.nn.silu(c_ref[...]).astype(BF16)
    o_ref[...] += jnp.dot(ca, w_ref[...].astype(BF16), preferred_element_type=F32)


def _modulation(c, w_mod, b_mod):
    L, D, E = w_mod.shape
    B = c.shape[0]
    td = 256
    return pl.pallas_call(
        _mod_kernel,
        grid=(L, D // td),
        in_specs=[
            pl.BlockSpec((B, td), lambda l, k: (0, k)),
            pl.BlockSpec((None, td, E), lambda l, k: (l, k, 0)),
            pl.BlockSpec((None, 1, E), lambda l, k: (l, 0, 0)),
        ],
        out_specs=pl.BlockSpec((None, B, E), lambda l, k: (l, 0, 0)),
        out_shape=jax.ShapeDtypeStruct((L, B, E), F32),
        compiler_params=_cparams("parallel", "arbitrary"),
        name="modulation",
    )(c, w_mod, b_mod.reshape(L, 1, E))


def _inproj_kernel(x_ref, mod_ref, w_ref, gain_ref, rope_ref, o_ref, h_ref):
    y = _rms(x_ref[...])
    h_ref[...] = (y * (1.0 + mod_ref[0, 1:2, :]) + mod_ref[0, 0:1, :]).astype(BF16)
    groups = PROJ_TILE // LANES

    def rope128(v):
        return v * rope_ref[0] + pltpu.roll(v, LANES // 2, 1) * rope_ref[1]

    def rope64(v):
        return (v * rope_ref[2] + pltpu.roll(v, 96, 1) * rope_ref[3]
                + pltpu.roll(v, 32, 1) * rope_ref[4])

    def misc(v, g):
        if g > 0:
            return v
        lane = lax.broadcasted_iota(jnp.int32, v.shape, 1)
        return jnp.where(lane < IDX_DH, rope64(v), v * IW_SCALE)

    def gain(t, g):
        return gain_ref[t, :, g * LANES:(g + 1) * LANES]

    norm_rope = lambda v, t, g: rope128(_rms(v, gain(t, g)))
    epilogue = {t: (lambda v, t, g: v) for t in range(N_PROJ_TILES)}
    epilogue.update({t: norm_rope for t in (T_AQ, T_AQ + 1, T_AK, T_AK + 1, T_CQ, T_CK)})
    epilogue[T_BU] = lambda v, t, g: jax.nn.gelu(v)
    epilogue[T_BV] = lambda v, t, g: _rms(jax.nn.gelu(v), gain(t, g))
    epilogue[T_IQ] = lambda v, t, g: rope64(v)
    epilogue[T_MISC] = lambda v, t, g: misc(v, g)

    for t in range(N_PROJ_TILES):
        acc = jnp.dot(h_ref[...], w_ref[:, t * PROJ_TILE:(t + 1) * PROJ_TILE],
                      preferred_element_type=F32)
        for g in range(groups):
            lo = t * PROJ_TILE + g * LANES
            o_ref[:, lo:lo + LANES] = epilogue[t](
                acc[:, g * LANES:(g + 1) * LANES], t, g).astype(o_ref.dtype)


def _inproj(x2, mod_l, w_all, layer, gains, rope, S):
    M, D = x2.shape
    tm = min(256, S)
    nrow = S // tm
    return pl.pallas_call(
        _inproj_kernel,
        grid=(M // tm,),
        in_specs=[
            pl.BlockSpec((tm, D), lambda i: (i, 0)),
            pl.BlockSpec((1, 6, D), lambda i: (i // nrow, 0, 0)),
            pl.BlockSpec((None, D, N_PROJ), lambda i: (layer, 0, 0), pipeline_mode=pl.Buffered(1)),
            pl.BlockSpec((N_PROJ_TILES, 1, PROJ_TILE), lambda i: (0, 0, 0)),
            pl.BlockSpec((5, tm, LANES), lambda i: (0, i % nrow, 0)),
        ],
        out_specs=pl.BlockSpec((tm, N_PROJ), lambda i: (i, 0)),
        out_shape=jax.ShapeDtypeStruct((M, N_PROJ), BF16),
        scratch_shapes=[pltpu.VMEM((tm, D), BF16)],
        compiler_params=_cparams("parallel"),
        name="inproj",
    )(x2, mod_l, w_all, gains, rope)


def _chunk_mask(row0, rows, cols):
    r = row0 + lax.broadcasted_iota(jnp.int32, (rows, cols), 0)
    c = lax.broadcasted_iota(jnp.int32, (rows, cols), 1)
    return (c // CHUNK) <= (r // CHUNK)


def _diffattn_kernel(lam_ref, subln_ref, q_ref, k_ref, v_ref, o_ref, *, lambda_init, widths):
    tq = q_ref.shape[0]
    half = tq // 2
    scale = A_DQK ** -0.5
    step = pl.program_id(0)
    lp = lam_ref[...]
    lam = (jnp.exp(jnp.sum(lp[0:1] * lp[1:2], axis=-1, keepdims=True))
           - jnp.exp(jnp.sum(lp[2:3] * lp[3:4], axis=-1, keepdims=True)) + lambda_init)

    diag_mask = _chunk_mask(0, half, half)

    def body(W):
        extent = lambda r: W - (1 - r) * half

        def scores(m, r):
            s = lax.dot_general(q_ref[r * half:(r + 1) * half, m * A_DQK:(m + 1) * A_DQK],
                                k_ref[0:extent(r), m * A_DQK:(m + 1) * A_DQK],
                                NT_DIMS, preferred_element_type=F32)
            diag = jnp.where(diag_mask, s[:, extent(r) - half:], NEG)
            return diag if extent(r) == half else jnp.concatenate([s[:, :extent(r) - half], diag], axis=1)

        def numerator(s):
            e = jnp.exp2((s - jnp.max(s, axis=-1, keepdims=True)) * (scale * LOG2_E))
            return e.astype(BF16), jnp.sum(e, axis=-1, keepdims=True)

        ss = [[scores(m, r) for m in range(2)] for r in range(2)]
        outs = []
        for r in range(2):
            (e1, l1), (e2, l2) = numerator(ss[r][0]), numerator(ss[r][1])
            o1 = jnp.dot(e1, v_ref[0:extent(r), :], preferred_element_type=F32)
            o2 = jnp.dot(e2, v_ref[0:extent(r), :], preferred_element_type=F32)
            outs.append(o1 * (1.0 / l1) - o2 * (lam / l2))
        o = jnp.concatenate(outs, axis=0)
        o_ref[...] = (_rms(o, subln_ref[...]) * (1.0 - lambda_init)).astype(o_ref.dtype)

    for ci, W in enumerate(widths):
        pl.when(step == ci)(functools.partial(body, W))


def _diffattn(pb, a_lambda_l, a_subln_l, lambda_init, B, S):
    tq = min(512, S)
    nq = S // tq
    return pl.pallas_call(
        functools.partial(_diffattn_kernel, lambda_init=lambda_init,
                          widths=tuple(tq * (i + 1) for i in range(nq))),
        grid=(nq, B, A_HEADS),
        in_specs=[
            pl.BlockSpec((4, A_DQK), lambda i, b, h: (0, 0)),
            pl.BlockSpec((1, A_DV), lambda i, b, h: (0, 0)),
            pl.BlockSpec((tq, A_DV), lambda i, b, h: (b * nq + i, h)),
            pl.BlockSpec((S, A_DV), lambda i, b, h: (b, A_HEADS + h)),
            pl.BlockSpec((S, A_DV), lambda i, b, h: (b, 2 * A_HEADS + h)),
        ],
        out_specs=pl.BlockSpec((tq, A_DV), lambda i, b, h: (b * nq + i, h)),
        out_shape=jax.ShapeDtypeStruct((B * S, A_WIDTH), BF16),
        compiler_params=_cparams("parallel", "parallel", "arbitrary"),
        name="diffattn",
    )(a_lambda_l, a_subln_l.reshape(1, A_DV), pb, pb, pb)


def _gmlp_kernel(u_ref, v_ref, ws_ref, bias_ref, o_ref):
    mask = _chunk_mask(0, B_BLOCK, B_BLOCK)
    for g in range(B_GROUPS):
        w = jnp.where(mask, ws_ref[g], 0.0).astype(BF16)
        cols = slice(g * B_CH, (g + 1) * B_CH)
        for n in range(u_ref.shape[0] // B_BLOCK):
            rows = slice(n * B_BLOCK, (n + 1) * B_BLOCK)
            z = jnp.dot(w, v_ref[rows, cols], preferred_element_type=F32) + bias_ref[:, g:g + 1]
            o_ref[rows, cols] = (u_ref[rows, cols].astype(F32) * z).astype(o_ref.dtype)


def _gmlp(pb, ws_l, bias_l, S):
    M = pb.shape[0]
    tm = min(512, S)
    return pl.pallas_call(
        _gmlp_kernel,
        grid=(M // tm,),
        in_specs=[
            pl.BlockSpec((tm, B_WIDTH), lambda i: (i, T_BU)),
            pl.BlockSpec((tm, B_WIDTH), lambda i: (i, T_BV)),
            pl.BlockSpec((B_GROUPS, B_BLOCK, B_BLOCK), lambda i: (0, 0, 0)),
            pl.BlockSpec((B_BLOCK, B_GROUPS), lambda i: (0, 0)),
        ],
        out_specs=pl.BlockSpec((tm, B_WIDTH), lambda i: (i, 0)),
        out_shape=jax.ShapeDtypeStruct((M, B_WIDTH), BF16),
        compiler_params=_cparams("parallel"),
        name="gmlp",
    )(pb, pb, ws_l, bias_l.T)


def _key_to_float(key):
    return lax.bitcast_convert_type(jnp.where(key < 0, key ^ jnp.int32(0x7FFFFFFF), key), F32)


def _dsa_kernel(q_ref, iq_ref, mq_ref, k_ref, v_ref, mk_ref, o_ref, isc_ref, p_ref, *, topk, widths):
    tq = q_ref.shape[0]
    half = tq // 2
    S = k_ref.shape[0]
    scale = C_DH ** -0.5
    int_min = jnp.int32(-2 ** 31)
    step = pl.program_id(0)
    lane = lax.broadcasted_iota(jnp.int32, (half, LANES), 1)
    zero = jnp.zeros((half, LANES), jnp.int32)
    ones = jnp.ones((LANES, LANES), BF16)

    def body(W):
        subs = [(slice(0, half), W - half), (slice(half, tq), W)]

        def chunk(rows, c):
            return isc_ref[c, rows, :]

        def count(rows, extent, pred):
            acc = jnp.zeros((half, LANES), F32)
            for c in range(extent // LANES):
                acc = acc + jnp.where(pred(chunk(rows, c), c), 1.0, 0.0)
            return jnp.dot(acc.astype(BF16), ones, preferred_element_type=F32)

        for r, (rows, extent) in enumerate(subs):
            ik = mk_ref[0:extent, 0:IDX_DH]
            isc = jnp.zeros((half, extent), F32)
            for h in range(IDX_HEADS):
                d = lax.dot_general(iq_ref[rows, h * IDX_DH:(h + 1) * IDX_DH], ik, NT_DIMS,
                                    preferred_element_type=F32)
                isc = isc + (jnp.maximum(d, 0.0)
                             * mq_ref[rows, IW_LANE0 + h:IW_LANE0 + h + 1].astype(F32))
            isc = jnp.where(_chunk_mask(step * tq + r * half, half, extent), isc, NEG)
            for c in range(extent // LANES):
                isc_ref[c, rows, :] = isc[:, c * LANES:(c + 1) * LANES]

        def value_step(i, t_us):
            out = []
            for (rows, extent), t_u in zip(subs, t_us):
                cand_u = t_u | lax.shift_left(jnp.int32(1), jnp.int32(31) - i)
                cand_f = _key_to_float(cand_u ^ int_min)
                cnt = (count(rows, extent, lambda ch, c: ch >= cand_f)
                       + jnp.where(cand_f <= NEG, float(S - extent), 0.0))
                out.append(jnp.where(cnt >= topk, cand_u, t_u))
            return tuple(out)

        thrs = [_key_to_float(t_u ^ int_min) for t_u in lax.fori_loop(0, 32, value_step, (zero, zero))]
        needs = [topk - count(rows, extent, lambda ch, c: ch > thr)
                 for (rows, extent), thr in zip(subs, thrs)]
        n_ties = [count(rows, extent, lambda ch, c: ch == thr)
                  for (rows, extent), thr in zip(subs, thrs)]

        p_ref[...] = jnp.full((tq, LANES), S - 1, jnp.int32)
        surplus = jnp.concatenate(
            [jnp.where(n > need, 1.0, 0.0) for n, need in zip(n_ties, needs)], axis=0)

        @pl.when(jnp.max(surplus) > 0.0)
        def _():
            for (rows, extent), thr, need in zip(subs, thrs, needs):
                def index_step(i, p):
                    cand = p | lax.shift_right_logical(jnp.int32(S // 2), i)
                    ties_below = count(
                        rows, extent, lambda ch, c: (ch == thr) & (lane + c * LANES < cand))
                    return jnp.where(ties_below < need, cand, p)
                p_ref[rows, :] = lax.fori_loop(0, int(math.log2(S)), index_step, zero)

        for (rows, extent), thr in zip(subs, thrs):
            p_last = p_ref[rows, :]
            for c in range(extent // LANES):
                ch = chunk(rows, c)
                sel = ((ch > thr) | ((ch == thr) & (lane + c * LANES <= p_last))) & (ch > 0.5 * NEG)
                isc_ref[c, rows, :] = jnp.where(sel, 0.0, NEG)

        for h in range(C_HEADS):
            sl = slice(h * C_DH, (h + 1) * C_DH)
            for rows, extent in subs:
                s = lax.dot_general(q_ref[rows, sl], k_ref[0:extent, sl], NT_DIMS,
                                    preferred_element_type=F32)
                s = s + jnp.concatenate(
                    [chunk(rows, c) for c in range(extent // LANES)], axis=1)
                e = jnp.exp2((s - jnp.max(s, axis=-1, keepdims=True)) * (scale * LOG2_E))
                o = jnp.dot(e.astype(BF16), v_ref[0:extent, sl], preferred_element_type=F32)
                o_ref[rows, sl] = (o * (1.0 / jnp.sum(e, axis=-1, keepdims=True))).astype(o_ref.dtype)

    for ci, W in enumerate(widths):
        pl.when(step == ci)(functools.partial(body, W))


def _dsa(pb, B, S):
    tq = min(512, S)
    nq = S // tq
    topk = min(TOPK_MAX, S // 4)
    widths = tuple(tq * (i + 1) for i in range(nq))
    qspec = lambda t: pl.BlockSpec((tq, PROJ_TILE), lambda i, b: (b * nq + i, t))
    kspec = lambda t: pl.BlockSpec((S, PROJ_TILE), lambda i, b: (b, t))
    return pl.pallas_call(
        functools.partial(_dsa_kernel, topk=topk, widths=widths),
        grid=(nq, B),
        in_specs=[qspec(T_CQ), qspec(T_IQ), qspec(T_MISC), kspec(T_CK), kspec(T_CV), kspec(T_MISC)],
        out_specs=pl.BlockSpec((tq, C_WIDTH), lambda i, b: (b * nq + i, 0)),
        out_shape=jax.ShapeDtypeStruct((B * S, C_WIDTH), BF16),
        scratch_shapes=[pltpu.VMEM((S // LANES, tq, LANES), F32),
                        pltpu.VMEM((tq, LANES), jnp.int32)],
        compiler_params=_cparams("parallel", "arbitrary"),
        name="dsa",
    )(pb, pb, pb, pb, pb, pb)


def _outproj_kernel(x_ref, mod_ref, oa_ref, ob_ref, oc_ref, w_ref, o_ref):
    y = jnp.dot(oa_ref[...], w_ref[0:A_WIDTH, :], preferred_element_type=F32)
    y = y + jnp.dot(ob_ref[...], w_ref[A_WIDTH:A_WIDTH + B_WIDTH, :], preferred_element_type=F32)
    y = y + jnp.dot(oc_ref[...], w_ref[A_WIDTH + B_WIDTH:, :], preferred_element_type=F32)
    o_ref[...] = x_ref[...] + mod_ref[0, 2:3, :] * y


def _outproj(x2, mod_l, oa, ob, oc, w_all, layer, S):
    M, D = x2.shape
    tm = min(512, S)
    nrow = S // tm
    row = lambda width: pl.BlockSpec((tm, width), lambda i: (i, 0))
    return pl.pallas_call(
        _outproj_kernel,
        grid=(M // tm,),
        in_specs=[
            row(D),
            pl.BlockSpec((1, 6, D), lambda i: (i // nrow, 0, 0)),
            row(A_WIDTH), row(B_WIDTH), row(C_WIDTH),
            pl.BlockSpec((None, D, D), lambda i: (layer, 0, 0)),
        ],
        out_specs=row(D),
        out_shape=jax.ShapeDtypeStruct((M, D), F32),
        compiler_params=_cparams("parallel"),
        name="outproj",
    )(x2, mod_l, oa, ob, oc, w_all)


def _ffn_kernel(x_ref, mod_ref, w1_ref, w2_ref, o_ref, h_ref, acc_ref):
    j = pl.program_id(1)

    @pl.when(j == 0)
    def _():
        y = _rms(x_ref[...])
        h_ref[...] = (y * (1.0 + mod_ref[0, 4:5, :]) + mod_ref[0, 3:4, :]).astype(BF16)
        acc_ref[...] = jnp.zeros_like(acc_ref)

    hid = jnp.square(jnp.maximum(jnp.dot(h_ref[...], w1_ref[...], preferred_element_type=F32), 0.0))
    acc_ref[...] += jnp.dot(hid.astype(BF16), w2_ref[...], preferred_element_type=F32)

    @pl.when(j == pl.num_programs(1) - 1)
    def _():
        o_ref[...] = x_ref[...] + mod_ref[0, 5:6, :] * acc_ref[...]


def _ffn(x2, mod_l, w1_all, w2_all, layer, S):
    M, D = x2.shape
    F = w1_all.shape[-1]
    tm = min(512, S)
    tf = 1024
    nrow = S // tm
    return pl.pallas_call(
        _ffn_kernel,
        grid=(M // tm, F // tf),
        in_specs=[
            pl.BlockSpec((tm, D), lambda i, j: (i, 0)),
            pl.BlockSpec((1, 6, D), lambda i, j: (i // nrow, 0, 0)),
            pl.BlockSpec((None, D, tf), lambda i, j: (layer, 0, j)),
            pl.BlockSpec((None, tf, D), lambda i, j: (layer, j, 0)),
        ],
        out_specs=pl.BlockSpec((tm, D), lambda i, j: (i, 0)),
        out_shape=jax.ShapeDtypeStruct((M, D), F32),
        scratch_shapes=[pltpu.VMEM((tm, D), BF16), pltpu.VMEM((tm, D), F32)],
        compiler_params=_cparams("parallel", "arbitrary"),
        name="ffn",
    )(x2, mod_l, w1_all, w2_all)


def _rope_tables(S):
    pos = jnp.arange(S, dtype=F32)[:, None]

    def angles(dim):
        inv = 1.0 / (ROPE_THETA ** (jnp.arange(0, dim, 2, dtype=F32) / dim))
        ang = pos * inv[None, :]
        return jnp.concatenate([ang, ang], axis=-1)

    a128 = angles(A_DQK)
    half = jnp.arange(LANES) < LANES // 2
    cos128, sin128 = jnp.cos(a128), jnp.sin(a128)
    a64 = jnp.tile(angles(IDX_DH), (1, 2))
    cos64, sin64 = jnp.cos(a64), jnp.sin(a64)
    low = (jnp.arange(LANES) % IDX_DH) < IDX_DH // 2
    return jnp.stack([
        cos128, jnp.where(half, -sin128, sin128),
        cos64, jnp.where(low, -sin64, 0.0), jnp.where(low, 0.0, sin64),
    ])


def _gain_rows(a_qnorm_l, a_knorm_l, b_vnorm_l, c_qnorm_l, c_knorm_l):
    rep = PROJ_TILE // LANES
    one = jnp.ones((PROJ_TILE,), F32)
    rows = [one] * N_PROJ_TILES
    rows[T_AQ] = rows[T_AQ + 1] = jnp.tile(a_qnorm_l, rep)
    rows[T_AK] = rows[T_AK + 1] = jnp.tile(a_knorm_l, rep)
    rows[T_BV] = b_vnorm_l
    rows[T_CQ] = jnp.tile(c_qnorm_l, rep)
    rows[T_CK] = jnp.tile(c_knorm_l, rep)
    return jnp.stack(rows)[:, None, :]


def kernel(x, c, w_mod, b_mod, w_in, w_out, a_qnorm, a_knorm, a_lambda, a_subln, b_vnorm, b_ws,
           b_bias, c_qnorm, c_knorm, w_ff1, w_ff2):
    B, S, D = x.shape
    L = w_mod.shape[0]
    assert D == D_MODEL and S % Q_BLOCK == 0 and w_in.shape[-1] == N_IN

    rope = _rope_tables(S)
    w_in_b = jnp.pad(w_in, ((0, 0), (0, 0), (0, N_PROJ - N_IN))).astype(BF16)
    w_out_b = w_out.astype(BF16)
    w_ff1_b = w_ff1.astype(BF16)
    w_ff2_b = w_ff2.astype(BF16)

    mod = _modulation(c, w_mod, b_mod).reshape(L, B, 6, D)
    x2 = x.reshape(B * S, D)
    for l in range(L):
        lambda_init = 0.8 - 0.6 * math.exp(-0.3 * l)
        gains = _gain_rows(a_qnorm[l], a_knorm[l], b_vnorm[l], c_qnorm[l], c_knorm[l])
        pb = _inproj(x2, mod[l], w_in_b, l, gains, rope, S)
        oa = _diffattn(pb, a_lambda[l], a_subln[l], lambda_init, B, S)
        ob = _gmlp(pb, b_ws[l], b_bias[l], S)
        oc = _dsa(pb, B, S)
        x2 = _outproj(x2, mod[l], oa, ob, oc, w_out_b, l, S)
        x2 = _ffn(x2, mod[l], w_ff1_b, w_ff2_b, l, S)
    return x2.reshape(B, S, D)
```

```python
import functools
import math

import jax
import jax.numpy as jnp
import numpy as np
from jax import lax
from jax.experimental import pallas as pl
from jax.experimental.pallas import tpu as pltpu

D_MODEL = 2048
CHUNK = 64
ROPE_THETA = 10000.0
EPS = 1e-6
NEG = -1e30
D_FF = 4 * D_MODEL
A_HEADS = 4
A_DQK = D_MODEL // 16
A_DV = 2 * A_DQK
A_WIDTH = A_HEADS * A_DV
B_GROUPS = 4
B_CH = D_MODEL // 16
B_WIDTH = B_GROUPS * B_CH
B_BLOCK = 128
C_HEADS = 4
C_DH = D_MODEL // 16
C_WIDTH = C_HEADS * C_DH
IDX_HEADS = 8
IDX_DH = D_MODEL // 32
TOPK_MAX = 256
Q_BLOCK = 128

LANES = 128
VMEM_LIMIT_BYTES = 56 * 1024 * 1024

PROJ_TILE = 512
T_AQ, T_AK, T_AV, T_BU, T_BV, T_CQ, T_CK, T_CV, T_IQ, T_MISC = 0, 2, 4, 6, 7, 8, 9, 10, 11, 12
N_PROJ_TILES = 13
N_IN = 3 * A_WIDTH + 2 * B_WIDTH + 3 * C_WIDTH + IDX_HEADS * IDX_DH + IDX_DH + IDX_HEADS
N_PROJ = N_PROJ_TILES * PROJ_TILE
IW_LANE0 = IDX_DH
IW_SCALE = IDX_HEADS ** -0.5 * IDX_DH ** -0.5

BF16 = jnp.bfloat16
F32 = jnp.float32
NT_DIMS = (((1,), (1,)), ((), ()))
LOG2_E = math.log2(math.e)


def _cparams(*sem):
    return pltpu.CompilerParams(dimension_semantics=sem, vmem_limit_bytes=VMEM_LIMIT_BYTES)


def _rms(v, gain=None):
    y = v * lax.rsqrt(jnp.mean(v * v, axis=-1, keepdims=True) + EPS)
    return y if gain is None else y * gain


def _mod_kernel(c_ref, w_ref, b_ref, o_ref):
    @pl.when(pl.program_id(1) == 0)
    def _():
        o_ref[...] = jnp.broadcast_to(b_ref[...], o_ref.shape)

    ca = jax.nn.silu(c_ref[...]).astype(BF16)
    o_ref[...] += jnp.dot(ca, w_ref[...].astype(BF16), preferred_element_type=F32)


def _modulation(c, w_mod, b_mod):
    L, D, E = w_mod.shape
    B = c.shape[0]
    td = 256
    return pl.pallas_call(
        _mod_kernel,
        grid=(L, D // td),
        in_specs=[
            pl.BlockSpec((B, td), lambda l, k: (0, k)),
            pl.BlockSpec((None, td, E), lambda l, k: (l, k, 0)),
            pl.BlockSpec((None, 1, E), lambda l, k: (l, 0, 0)),
        ],
        out_specs=pl.BlockSpec((None, B, E), lambda l, k: (l, 0, 0)),
        out_shape=jax.ShapeDtypeStruct((L, B, E), F32),
        compiler_params=_cparams("parallel", "arbitrary"),
        name="modulation",
    )(c, w_mod, b_mod.reshape(L, 1, E))


def _inproj_kernel(x_ref, mod_ref, w_ref, gain_ref, rope_ref, o_ref, h_ref):
    y = _rms(x_ref[...])
    h_ref[...] = (y * (1.0 + mod_ref[0, 1:2, :]) + mod_ref[0, 0:1, :]).astype(BF16)
    groups = PROJ_TILE // LANES

    def rope128(v):
        return v * rope_ref[0] + pltpu.roll(v, LANES // 2, 1) * rope_ref[1]

    def rope64(v):
        return (v * rope_ref[2] + pltpu.roll(v, 96, 1) * rope_ref[3]
                + pltpu.roll(v, 32, 1) * rope_ref[4])

    def misc(v, g):
        if g > 0:
            return v
        lane = lax.broadcasted_iota(jnp.int32, v.shape, 1)
        return jnp.where(lane < IDX_DH, rope64(v), v * IW_SCALE)

    def gain(t, g):
        return gain_ref[t, :, g * LANES:(g + 1) * LANES]

    norm_rope = lambda v, t, g: rope128(_rms(v, gain(t, g)))
    epilogue = {t: (lambda v, t, g: v) for t in range(N_PROJ_TILES)}
    epilogue.update({t: norm_rope for t in (T_AQ, T_AQ + 1, T_AK, T_AK + 1, T_CQ, T_CK)})
    epilogue[T_BU] = lambda v, t, g: jax.nn.gelu(v)
    epilogue[T_BV] = lambda v, t, g: _rms(jax.nn.gelu(v), gain(t, g))
    epilogue[T_IQ] = lambda v, t, g: rope64(v)
    epilogue[T_MISC] = lambda v, t, g: misc(v, g)

    for t in range(N_PROJ_TILES):
        acc = jnp.dot(h_ref[...], w_ref[:, t * PROJ_TILE:(t + 1) * PROJ_TILE],
                      preferred_element_type=F32)
        for g in range(groups):
            lo = t * PROJ_TILE + g * LANES
            o_ref[:, lo:lo + LANES] = epilogue[t](
                acc[:, g * LANES:(g + 1) * LANES], t, g).astype(o_ref.dtype)


def _inproj(x2, mod_l, w_all, layer, gains, rope, S):
    M, D = x2.shape
    tm = min(256, S)
    nrow = S // tm
    return pl.pallas_call(
        _inproj_kernel,
        grid=(M // tm,),
        in_specs=[
            pl.BlockSpec((tm, D), lambda i: (i, 0)),
            pl.BlockSpec((1, 6, D), lambda i: (i // nrow, 0, 0)),
            pl.BlockSpec((None, D, N_PROJ), lambda i: (layer, 0, 0), pipeline_mode=pl.Buffered(1)),
            pl.BlockSpec((N_PROJ_TILES, 1, PROJ_TILE), lambda i: (0, 0, 0)),
            pl.BlockSpec((5, tm, LANES), lambda i: (0, i % nrow, 0)),
        ],
        out_specs=pl.BlockSpec((tm, N_PROJ), lambda i: (i, 0)),
        out_shape=jax.ShapeDtypeStruct((M, N_PROJ), BF16),
        scratch_shapes=[pltpu.VMEM((tm, D), BF16)],
        compiler_params=_cparams("parallel"),
        name="inproj",
    )(x2, mod_l, w_all, gains, rope)


def _chunk_mask(row0, rows, cols):
    r = row0 + lax.broadcasted_iota(jnp.int32, (rows, cols), 0)
    c = lax.broadcasted_iota(jnp.int32, (rows, cols), 1)
    return (c // CHUNK) <= (r // CHUNK)


def _diffattn_kernel(lam_ref, subln_ref, q_ref, k_ref, v_ref, o_ref, *, lambda_init, widths):
    tq = q_ref.shape[0]
    half = tq // 2
    scale = A_DQK ** -0.5
    step = pl.program_id(0)
    lp = lam_ref[...]
    lam = (jnp.exp(jnp.sum(lp[0:1] * lp[1:2], axis=-1, keepdims=True))
           - jnp.exp(jnp.sum(lp[2:3] * lp[3:4], axis=-1, keepdims=True)) + lambda_init)

    diag_mask = _chunk_mask(0, half, half)

    def body(W):
        extent = lambda r: W - (1 - r) * half

        def scores(m, r):
            s = lax.dot_general(q_ref[r * half:(r + 1) * half, m * A_DQK:(m + 1) * A_DQK],
                                k_ref[0:extent(r), m * A_DQK:(m + 1) * A_DQK],
                                NT_DIMS, preferred_element_type=F32)
            diag = jnp.where(diag_mask, s[:, extent(r) - half:], NEG)
            return diag if extent(r) == half else jnp.concatenate([s[:, :extent(r) - half], diag], axis=1)

        def numerator(s):
            e = jnp.exp2((s - jnp.max(s, axis=-1, keepdims=True)) * (scale * LOG2_E))
            return e.astype(BF16), jnp.sum(e, axis=-1, keepdims=True)

        ss = [[scores(m, r) for m in range(2)] for r in range(2)]
        outs = []
        for r in range(2):
            (e1, l1), (e2, l2) = numerator(ss[r][0]), numerator(ss[r][1])
            o1 = jnp.dot(e1, v_ref[0:extent(r), :], preferred_element_type=F32)
            o2 = jnp.dot(e2, v_ref[0:extent(r), :], preferred_element_type=F32)
            outs.append(o1 * (1.0 / l1) - o2 * (lam / l2))
        o = jnp.concatenate(outs, axis=0)
        o_ref[...] = (_rms(o, subln_ref[...]) * (1.0 - lambda_init)).astype(o_ref.dtype)

    for ci, W in enumerate(widths):
        pl.when(step == ci)(functools.partial(body, W))


def _diffattn(pb, a_lambda_l, a_subln_l, lambda_init, B, S):
    tq = min(512, S)
    nq = S // tq
    return pl.pallas_call(
        functools.partial(_diffattn_kernel, lambda_init=lambda_init,
                          widths=tuple(tq * (i + 1) for i in range(nq))),
        grid=(nq, B, A_HEADS),
        in_specs=[
            pl.BlockSpec((4, A_DQK), lambda i, b, h: (0, 0)),
            pl.BlockSpec((1, A_DV), lambda i, b, h: (0, 0)),
            pl.BlockSpec((tq, A_DV), lambda i, b, h: (b * nq + i, h)),
            pl.BlockSpec((S, A_DV), lambda i, b, h: (b, A_HEADS + h)),
            pl.BlockSpec((S, A_DV), lambda i, b, h: (b, 2 * A_HEADS + h)),
        ],
        out_specs=pl.BlockSpec((tq, A_DV), lambda i, b, h: (b * nq + i, h)),
        out_shape=jax.ShapeDtypeStruct((B * S, A_WIDTH), BF16),
        compiler_params=_cparams("parallel", "parallel", "arbitrary"),
        name="diffattn",
    )(a_lambda_l, a_subln_l.reshape(1, A_DV), pb, pb, pb)


def _gmlp_kernel(u_ref, v_ref, ws_ref, bias_ref, o_ref):
    mask = _chunk_mask(0, B_BLOCK, B_BLOCK)
    for g in range(B_GROUPS):
        w = jnp.where(mask, ws_ref[g], 0.0).astype(BF16)
        cols = slice(g * B_CH, (g + 1) * B_CH)
        for n in range(u_ref.shape[0] // B_BLOCK):
            rows = slice(n * B_BLOCK, (n + 1) * B_BLOCK)
            z = jnp.dot(w, v_ref[rows, cols], preferred_element_type=F32) + bias_ref[:, g:g + 1]
            o_ref[rows, cols] = (u_ref[rows, cols].astype(F32) * z).astype(o_ref.dtype)


def _gmlp(pb, ws_l, bias_l, S):
    M = pb.shape[0]
    tm = min(512, S)
    return pl.pallas_call(
        _gmlp_kernel,
        grid=(M // tm,),
        in_specs=[
            pl.BlockSpec((tm, B_WIDTH), lambda i: (i, T_BU)),
            pl.BlockSpec((tm, B_WIDTH), lambda i: (i, T_BV)),
            pl.BlockSpec((B_GROUPS, B_BLOCK, B_BLOCK), lambda i: (0, 0, 0)),
            pl.BlockSpec((B_BLOCK, B_GROUPS), lambda i: (0, 0)),
        ],
        out_specs=pl.BlockSpec((tm, B_WIDTH), lambda i: (i, 0)),
        out_shape=jax.ShapeDtypeStruct((M, B_WIDTH), BF16),
        compiler_params=_cparams("parallel"),
        name="gmlp",
    )(pb, pb, ws_l, bias_l.T)


def _key_to_float(key):
    return lax.bitcast_convert_type(jnp.where(key < 0, key ^ jnp.int32(0x7FFFFFFF), key), F32)


def _dsa_kernel(q_ref, iq_ref, mq_ref, k_ref, v_ref, mk_ref, o_ref, isc_ref, p_ref, iqh_ref, iw_ref,
                *, topk, widths):
    tq = q_ref.shape[0]
    half = tq // 2
    S = k_ref.shape[0]
    scale = C_DH ** -0.5
    int_min = jnp.int32(-2 ** 31)
    step = pl.program_id(1)
    lane = lax.broadcasted_iota(jnp.int32, (half, LANES), 1)
    zero = jnp.zeros((half, LANES), jnp.int32)

    for h in range(IDX_HEADS):
        iqh_ref[h] = iq_ref[:, h * IDX_DH:(h + 1) * IDX_DH]
        iw_ref[h] = jnp.broadcast_to(
            mq_ref[:, IW_LANE0 + h:IW_LANE0 + h + 1].astype(F32), (tq, LANES))

    def body(W):
        subs = [(slice(0, half), W - half), (slice(half, tq), W)]

        def chunk(rows, c):
            return isc_ref[rows, c * LANES:(c + 1) * LANES]

        def count(rows, extent, pred):
            acc = jnp.zeros((half, LANES), F32)
            for c in range(extent // LANES):
                acc = acc + jnp.where(pred(chunk(rows, c), c), 1.0, 0.0)
            return jnp.broadcast_to(jnp.sum(acc, axis=-1, keepdims=True), acc.shape)

        isc_ref[:, 0:W] = jnp.zeros((tq, W), F32)

        def indexer_head(h, carry):
            for rows, extent in subs:
                d = lax.dot_general(iqh_ref[h, rows, :], mk_ref[0:extent, 0:IDX_DH], NT_DIMS,
                                    preferred_element_type=F32)
                w = iw_ref[h, rows, :]
                for c in range(extent // LANES):
                    isc_ref[rows, c * LANES:(c + 1) * LANES] = (
                        chunk(rows, c) + jnp.maximum(d[:, c * LANES:(c + 1) * LANES], 0.0) * w)
            return carry

        lax.fori_loop(0, IDX_HEADS, indexer_head, 0)
        for r, (rows, extent) in enumerate(subs):
            isc_ref[rows, 0:extent] = jnp.where(
                _chunk_mask(step * tq + r * half, half, extent), isc_ref[rows, 0:extent], NEG)

        def value_step(i, t_us):
            out = []
            for (rows, extent), t_u in zip(subs, t_us):
                cand_u = t_u | lax.shift_left(jnp.int32(1), jnp.int32(31) - i)
                cand_f = _key_to_float(cand_u ^ int_min)
                cnt = (count(rows, extent, lambda ch, c: ch >= cand_f)
                       + jnp.where(cand_f <= NEG, float(S - extent), 0.0))
                out.append(jnp.where(cnt >= topk, cand_u, t_u))
            return tuple(out)

        thrs = [_key_to_float(t_u ^ int_min) for t_u in lax.fori_loop(0, 32, value_step, (zero, zero))]
        needs = [topk - count(rows, extent, lambda ch, c: ch > thr)
                 for (rows, extent), thr in zip(subs, thrs)]
        n_ties = [count(rows, extent, lambda ch, c: ch == thr)
                  for (rows, extent), thr in zip(subs, thrs)]

        p_ref[...] = jnp.full((tq, LANES), S - 1, jnp.int32)
        surplus = jnp.concatenate(
            [jnp.where(n > need, 1.0, 0.0) for n, need in zip(n_ties, needs)], axis=0)

        @pl.when(jnp.max(surplus) > 0.0)
        def _():
            for (rows, extent), thr, need in zip(subs, thrs, needs):
                def index_step(i, p):
                    cand = p | lax.shift_right_logical(jnp.int32(S // 2), i)
                    ties_below = count(
                        rows, extent, lambda ch, c: (ch == thr) & (lane + c * LANES < cand))
                    return jnp.where(ties_below < need, cand, p)
                p_ref[rows, :] = lax.fori_loop(0, int(math.log2(S)), index_step, zero)

        for (rows, extent), thr in zip(subs, thrs):
            p_last = p_ref[rows, :]
            for c in range(extent // LANES):
                ch = chunk(rows, c)
                sel = ((ch > thr) | ((ch == thr) & (lane + c * LANES <= p_last))) & (ch > 0.5 * NEG)
                isc_ref[rows, c * LANES:(c + 1) * LANES] = jnp.where(sel, 0.0, NEG)

        def attention_head(h, carry):
            sl = pl.ds(pl.multiple_of(h * C_DH, C_DH), C_DH)
            for rows, extent in subs:
                s = lax.dot_general(q_ref[rows, sl], k_ref[0:extent, sl], NT_DIMS,
                                    preferred_element_type=F32)
                s = s + isc_ref[rows, 0:extent]
                e = jnp.exp2((s - jnp.max(s, axis=-1, keepdims=True)) * (scale * LOG2_E))
                o = jnp.dot(e.astype(BF16), v_ref[0:extent, sl], preferred_element_type=F32)
                o_ref[rows, sl] = (o * (1.0 / jnp.sum(e, axis=-1, keepdims=True))).astype(o_ref.dtype)
            return carry

        lax.fori_loop(0, C_HEADS, attention_head, 0)

    for ci, W in enumerate(widths):
        pl.when(step == ci)(functools.partial(body, W))


def _dsa(pb, B, S):
    tq = min(512, S)
    nq = S // tq
    topk = min(TOPK_MAX, S // 4)
    widths = tuple(tq * (i + 1) for i in range(nq))
    qspec = lambda t: pl.BlockSpec((tq, PROJ_TILE), lambda b, i: (b * nq + i, t))
    kspec = lambda t: pl.BlockSpec((S, PROJ_TILE), lambda b, i: (b, t))
    return pl.pallas_call(
        functools.partial(_dsa_kernel, topk=topk, widths=widths),
        grid=(B, nq),
        in_specs=[qspec(T_CQ), qspec(T_IQ), qspec(T_MISC), kspec(T_CK), kspec(T_CV), kspec(T_MISC)],
        out_specs=pl.BlockSpec((tq, C_WIDTH), lambda b, i: (b * nq + i, 0)),
        out_shape=jax.ShapeDtypeStruct((B * S, C_WIDTH), BF16),
        scratch_shapes=[pltpu.VMEM((tq, S), F32), pltpu.VMEM((tq, LANES), jnp.int32),
                        pltpu.VMEM((IDX_HEADS, tq, IDX_DH), BF16),
                        pltpu.VMEM((IDX_HEADS, tq, LANES), F32)],
        compiler_params=_cparams("parallel", "arbitrary"),
        name="dsa",
    )(pb, pb, pb, pb, pb, pb)


def _outproj_kernel(x_ref, mod_ref, oa_ref, ob_ref, oc_ref, w_ref, o_ref):
    y = jnp.dot(oa_ref[...], w_ref[0:A_WIDTH, :], preferred_element_type=F32)
    y = y + jnp.dot(ob_ref[...], w_ref[A_WIDTH:A_WIDTH + B_WIDTH, :], preferred_element_type=F32)
    y = y + jnp.dot(oc_ref[...], w_ref[A_WIDTH + B_WIDTH:, :], preferred_element_type=F32)
    o_ref[...] = x_ref[...] + mod_ref[0, 2:3, :] * y


def _outproj(x2, mod_l, oa, ob, oc, w_all, layer, S):
    M, D = x2.shape
    tm = min(512, S)
    nrow = S // tm
    row = lambda width: pl.BlockSpec((tm, width), lambda i: (i, 0))
    return pl.pallas_call(
        _outproj_kernel,
        grid=(M // tm,),
        in_specs=[
            row(D),
            pl.BlockSpec((1, 6, D), lambda i: (i // nrow, 0, 0)),
            row(A_WIDTH), row(B_WIDTH), row(C_WIDTH),
            pl.BlockSpec((None, D, D), lambda i: (layer, 0, 0)),
        ],
        out_specs=row(D),
        out_shape=jax.ShapeDtypeStruct((M, D), F32),
        compiler_params=_cparams("parallel"),
        name="outproj",
    )(x2, mod_l, oa, ob, oc, w_all)


def _ffn_kernel(x_ref, mod_ref, w1_ref, w2_ref, o_ref, h_ref, acc_ref):
    j = pl.program_id(1)

    @pl.when(j == 0)
    def _():
        y = _rms(x_ref[...])
        h_ref[...] = (y * (1.0 + mod_ref[0, 4:5, :]) + mod_ref[0, 3:4, :]).astype(BF16)
        acc_ref[...] = jnp.zeros_like(acc_ref)

    hid = jnp.square(jnp.maximum(jnp.dot(h_ref[...], w1_ref[...], preferred_element_type=F32), 0.0))
    acc_ref[...] += jnp.dot(hid.astype(BF16), w2_ref[...], preferred_element_type=F32)

    @pl.when(j == pl.num_programs(1) - 1)
    def _():
        o_ref[...] = x_ref[...] + mod_ref[0, 5:6, :] * acc_ref[...]


def _ffn(x2, mod_l, w1_all, w2_all, layer, S):
    M, D = x2.shape
    F = w1_all.shape[-1]
    tm = min(512, S)
    tf = 1024
    nrow = S // tm
    return pl.pallas_call(
        _ffn_kernel,
        grid=(M // tm, F // tf),
        in_specs=[
            pl.BlockSpec((tm, D), lambda i, j: (i, 0)),
            pl.BlockSpec((1, 6, D), lambda i, j: (i // nrow, 0, 0)),
            pl.BlockSpec((None, D, tf), lambda i, j: (layer, 0, j)),
            pl.BlockSpec((None, tf, D), lambda i, j: (layer, j, 0)),
        ],
        out_specs=pl.BlockSpec((tm, D), lambda i, j: (i, 0)),
        out_shape=jax.ShapeDtypeStruct((M, D), F32),
        scratch_shapes=[pltpu.VMEM((tm, D), BF16), pltpu.VMEM((tm, D), F32)],
        compiler_params=_cparams("parallel", "arbitrary"),
        name="ffn",
    )(x2, mod_l, w1_all, w2_all)


def _rope_tables(S):
    pos = jnp.arange(S, dtype=F32)[:, None]

    def angles(dim):
        inv = 1.0 / (ROPE_THETA ** (jnp.arange(0, dim, 2, dtype=F32) / dim))
        ang = pos * inv[None, :]
        return jnp.concatenate([ang, ang], axis=-1)

    a128 = angles(A_DQK)
    half = jnp.arange(LANES) < LANES // 2
    cos128, sin128 = jnp.cos(a128), jnp.sin(a128)
    a64 = jnp.tile(angles(IDX_DH), (1, 2))
    cos64, sin64 = jnp.cos(a64), jnp.sin(a64)
    low = (jnp.arange(LANES) % IDX_DH) < IDX_DH // 2
    return jnp.stack([
        cos128, jnp.where(half, -sin128, sin128),
        cos64, jnp.where(low, -sin64, 0.0), jnp.where(low, 0.0, sin64),
    ])


def _gain_rows(a_qnorm_l, a_knorm_l, b_vnorm_l, c_qnorm_l, c_knorm_l):
    rep = PROJ_TILE // LANES
    one = jnp.ones((PROJ_TILE,), F32)
    rows = [one] * N_PROJ_TILES
    rows[T_AQ] = rows[T_AQ + 1] = jnp.tile(a_qnorm_l, rep)
    rows[T_AK] = rows[T_AK + 1] = jnp.tile(a_knorm_l, rep)
    rows[T_BV] = b_vnorm_l
    rows[T_CQ] = jnp.tile(c_qnorm_l, rep)
    rows[T_CK] = jnp.tile(c_knorm_l, rep)
    return jnp.stack(rows)[:, None, :]


def kernel(x, c, w_mod, b_mod, w_in, w_out, a_qnorm, a_knorm, a_lambda, a_subln, b_vnorm, b_ws,
           b_bias, c_qnorm, c_knorm, w_ff1, w_ff2):
    B, S, D = x.shape
    L = w_mod.shape[0]
    assert D == D_MODEL and S % Q_BLOCK == 0 and w_in.shape[-1] == N_IN

    rope = _rope_tables(S)
    w_in_b = jnp.pad(w_in, ((0, 0), (0, 0), (0, N_PROJ - N_IN))).astype(BF16)
    w_out_b = w_out.astype(BF16)
    w_ff1_b = w_ff1.astype(BF16)
    w_ff2_b = w_ff2.astype(BF16)

    mod = _modulation(c, w_mod, b_mod).reshape(L, B, 6, D)
    x2 = x.reshape(B * S, D)
    for l in range(L):
        lambda_init = 0.8 - 0.6 * math.exp(-0.3 * l)
        gains = _gain_rows(a_qnorm[l], a_knorm[l], b_vnorm[l], c_qnorm[l], c_knorm[l])
        pb = _inproj(x2, mod[l], w_in_b, l, gains, rope, S)
        oa = _diffattn(pb, a_lambda[l], a_subln[l], lambda_init, B, S)
        ob = _gmlp(pb, b_ws[l], b_bias[l], S)
        oc = _dsa(pb, B, S)
        x2 = _outproj(x2, mod[l], oa, ob, oc, w_out_b, l, S)
        x2 = _ffn(x2, mod[l], w_ff1_b, w_ff2_b, l, S)
    return x2.reshape(B, S, D)
```

```python
import functools
import math

import jax
import jax.numpy as jnp
import numpy as np
from jax import lax
from jax.experimental import pallas as pl
from jax.experimental.pallas import tpu as pltpu

D_MODEL = 2048
CHUNK = 64
ROPE_THETA = 10000.0
EPS = 1e-6
NEG = -1e30
D_FF = 4 * D_MODEL
A_HEADS = 4
A_DQK = D_MODEL // 16
A_DV = 2 * A_DQK
A_WIDTH = A_HEADS * A_DV
B_GROUPS = 4
B_CH = D_MODEL // 16
B_WIDTH = B_GROUPS * B_CH
B_BLOCK = 128
C_HEADS = 4
C_DH = D_MODEL // 16
C_WIDTH = C_HEADS * C_DH
IDX_HEADS = 8
IDX_DH = D_MODEL // 32
TOPK_MAX = 256
Q_BLOCK = 128

LANES = 128
VMEM_LIMIT_BYTES = 56 * 1024 * 1024

PROJ_TILE = 512
T_AQ, T_AK, T_AV, T_BU, T_BV, T_CQ, T_CK, T_CV, T_IQ, T_MISC = 0, 2, 4, 6, 7, 8, 9, 10, 11, 12
N_PROJ_TILES = 13
N_IN = 3 * A_WIDTH + 2 * B_WIDTH + 3 * C_WIDTH + IDX_HEADS * IDX_DH + IDX_DH + IDX_HEADS
N_PROJ = N_PROJ_TILES * PROJ_TILE
IW_LANE0 = IDX_DH
IW_SCALE = IDX_HEADS ** -0.5 * IDX_DH ** -0.5

BF16 = jnp.bfloat16
F32 = jnp.float32
NT_DIMS = (((1,), (1,)), ((), ()))
LOG2_E = math.log2(math.e)


def _cparams(*sem):
    return pltpu.CompilerParams(dimension_semantics=sem, vmem_limit_bytes=VMEM_LIMIT_BYTES)


def _rms(v, gain=None):
    y = v * lax.rsqrt(jnp.mean(v * v, axis=-1, keepdims=True) + EPS)
    return y if gain is None else y * gain


def _mod_kernel(c_ref, w_ref, b_ref, o_ref):
    @pl.when(pl.program_id(1) == 0)
    def _():
        o_ref[...] = jnp.broadcast_to(b_ref[...], o_ref.shape)

    ca = jax.nn.silu(c_ref[...]).astype(BF16)
    o_ref[...] += jnp.dot(ca, w_ref[...].astype(BF16), preferred_element_type=F32)


def _modulation(c, w_mod, b_mod):
    L, D, E = w_mod.shape
    B = c.shape[0]
    td = 256
    return pl.pallas_call(
        _mod_kernel,
        grid=(L, D // td),
        in_specs=[
            pl.BlockSpec((B, td), lambda l, k: (0, k)),
            pl.BlockSpec((None, td, E), lambda l, k: (l, k, 0)),
            pl.BlockSpec((None, 1, E), lambda l, k: (l, 0, 0)),
        ],
        out_specs=pl.BlockSpec((None, B, E), lambda l, k: (l, 0, 0)),
        out_shape=jax.ShapeDtypeStruct((L, B, E), F32),
        compiler_params=_cparams("parallel", "arbitrary"),
        name="modulation",
    )(c, w_mod, b_mod.reshape(L, 1, E))


def _inproj_kernel(x_ref, mod_ref, w_ref, gain_ref, rope_ref, o_ref, h_ref):
    y = _rms(x_ref[...])
    h_ref[...] = (y * (1.0 + mod_ref[0, 1:2, :]) + mod_ref[0, 0:1, :]).astype(BF16)
    groups = PROJ_TILE // LANES

    def rope128(v):
        return v * rope_ref[0] + pltpu.roll(v, LANES // 2, 1) * rope_ref[1]

    def rope64(v):
        return (v * rope_ref[2] + pltpu.roll(v, 96, 1) * rope_ref[3]
                + pltpu.roll(v, 32, 1) * rope_ref[4])

    def misc(v, g):
        if g > 0:
            return v
        lane = lax.broadcasted_iota(jnp.int32, v.shape, 1)
        return jnp.where(lane < IDX_DH, rope64(v), v * IW_SCALE)

    def gain(t, g):
        return gain_ref[t, :, g * LANES:(g + 1) * LANES]

    norm_rope = lambda v, t, g: rope128(_rms(v, gain(t, g)))
    epilogue = {t: (lambda v, t, g: v) for t in range(N_PROJ_TILES)}
    epilogue.update({t: norm_rope for t in (T_AQ, T_AQ + 1, T_AK, T_AK + 1, T_CQ, T_CK)})
    epilogue[T_BU] = lambda v, t, g: jax.nn.gelu(v)
    epilogue[T_BV] = lambda v, t, g: _rms(jax.nn.gelu(v), gain(t, g))
    epilogue[T_IQ] = lambda v, t, g: rope64(v)
    epilogue[T_MISC] = lambda v, t, g: misc(v, g)

    for t in range(N_PROJ_TILES):
        acc = jnp.dot(h_ref[...], w_ref[:, t * PROJ_TILE:(t + 1) * PROJ_TILE],
                      preferred_element_type=F32)
        for g in range(groups):
            lo = t * PROJ_TILE + g * LANES
            o_ref[:, lo:lo + LANES] = epilogue[t](
                acc[:, g * LANES:(g + 1) * LANES], t, g).astype(o_ref.dtype)


def _inproj(x2, mod_l, w, gains, rope, S):
    M, D = x2.shape
    tm = min(256, S)
    nrow = S // tm
    return pl.pallas_call(
        _inproj_kernel,
        grid=(M // tm,),
        in_specs=[
            pl.BlockSpec((tm, D), lambda i: (i, 0)),
            pl.BlockSpec((1, 6, D), lambda i: (i // nrow, 0, 0)),
            pl.BlockSpec((D, N_PROJ), lambda i: (0, 0), pipeline_mode=pl.Buffered(1)),
            pl.BlockSpec((N_PROJ_TILES, 1, PROJ_TILE), lambda i: (0, 0, 0)),
            pl.BlockSpec((5, tm, LANES), lambda i: (0, i % nrow, 0)),
        ],
        out_specs=pl.BlockSpec((tm, N_PROJ), lambda i: (i, 0)),
        out_shape=jax.ShapeDtypeStruct((M, N_PROJ), BF16),
        scratch_shapes=[pltpu.VMEM((tm, D), BF16)],
        compiler_params=_cparams("parallel"),
        name="inproj",
    )(x2, mod_l, w, gains, rope)


def _chunk_mask(row0, rows, cols):
    r = row0 + lax.broadcasted_iota(jnp.int32, (rows, cols), 0)
    c = lax.broadcasted_iota(jnp.int32, (rows, cols), 1)
    return (c // CHUNK) <= (r // CHUNK)


def _diffattn_kernel(lam_ref, subln_ref, q_ref, k_ref, v_ref, o_ref, *, lambda_init, widths):
    tq = q_ref.shape[0]
    half = tq // 2
    scale = A_DQK ** -0.5
    step = pl.program_id(0)
    lp = lam_ref[...]
    lam = (jnp.exp(jnp.sum(lp[0:1] * lp[1:2], axis=-1, keepdims=True))
           - jnp.exp(jnp.sum(lp[2:3] * lp[3:4], axis=-1, keepdims=True)) + lambda_init)

    diag_mask = _chunk_mask(0, half, half)

    def body(W):
        extent = lambda r: W - (1 - r) * half

        def scores(m, r):
            s = lax.dot_general(q_ref[r * half:(r + 1) * half, m * A_DQK:(m + 1) * A_DQK],
                                k_ref[0:extent(r), m * A_DQK:(m + 1) * A_DQK],
                                NT_DIMS, preferred_element_type=F32)
            diag = jnp.where(diag_mask, s[:, extent(r) - half:], NEG)
            return diag if extent(r) == half else jnp.concatenate([s[:, :extent(r) - half], diag], axis=1)

        def numerator(s):
            e = jnp.exp2((s - jnp.max(s, axis=-1, keepdims=True)) * (scale * LOG2_E))
            return e.astype(BF16), jnp.sum(e, axis=-1, keepdims=True)

        ss = [[scores(m, r) for m in range(2)] for r in range(2)]
        outs = []
        for r in range(2):
            (e1, l1), (e2, l2) = numerator(ss[r][0]), numerator(ss[r][1])
            o1 = jnp.dot(e1, v_ref[0:extent(r), :], preferred_element_type=F32)
            o2 = jnp.dot(e2, v_ref[0:extent(r), :], preferred_element_type=F32)
            outs.append(o1 * (1.0 / l1) - o2 * (lam / l2))
        o = jnp.concatenate(outs, axis=0)
        o_ref[...] = (_rms(o, subln_ref[...]) * (1.0 - lambda_init)).astype(o_ref.dtype)

    for ci, W in enumerate(widths):
        pl.when(step == ci)(functools.partial(body, W))


def _diffattn(pb, a_lambda_l, a_subln_l, lambda_init, B, S):
    tq = min(512, S)
    nq = S // tq
    return pl.pallas_call(
        functools.partial(_diffattn_kernel, lambda_init=lambda_init,
                          widths=tuple(tq * (i + 1) for i in range(nq))),
        grid=(nq, B, A_HEADS),
        in_specs=[
            pl.BlockSpec((4, A_DQK), lambda i, b, h: (0, 0)),
            pl.BlockSpec((1, A_DV), lambda i, b, h: (0, 0)),
            pl.BlockSpec((tq, A_DV), lambda i, b, h: (b * nq + i, h)),
            pl.BlockSpec((S, A_DV), lambda i, b, h: (b, A_HEADS + h)),
            pl.BlockSpec((S, A_DV), lambda i, b, h: (b, 2 * A_HEADS + h)),
        ],
        out_specs=pl.BlockSpec((tq, A_DV), lambda i, b, h: (b * nq + i, h)),
        out_shape=jax.ShapeDtypeStruct((B * S, A_WIDTH), BF16),
        compiler_params=_cparams("parallel", "parallel", "arbitrary"),
        name="diffattn",
    )(a_lambda_l, a_subln_l.reshape(1, A_DV), pb, pb, pb)


def _gmlp_kernel(u_ref, v_ref, ws_ref, bias_ref, o_ref):
    mask = _chunk_mask(0, B_BLOCK, B_BLOCK)
    for g in range(B_GROUPS):
        w = jnp.where(mask, ws_ref[g], 0.0).astype(BF16)
        cols = slice(g * B_CH, (g + 1) * B_CH)
        for n in range(u_ref.shape[0] // B_BLOCK):
            rows = slice(n * B_BLOCK, (n + 1) * B_BLOCK)
            z = jnp.dot(w, v_ref[rows, cols], preferred_element_type=F32) + bias_ref[:, g:g + 1]
            o_ref[rows, cols] = (u_ref[rows, cols].astype(F32) * z).astype(o_ref.dtype)


def _gmlp(pb, ws_l, bias_l, S):
    M = pb.shape[0]
    tm = min(512, S)
    return pl.pallas_call(
        _gmlp_kernel,
        grid=(M // tm,),
        in_specs=[
            pl.BlockSpec((tm, B_WIDTH), lambda i: (i, T_BU)),
            pl.BlockSpec((tm, B_WIDTH), lambda i: (i, T_BV)),
            pl.BlockSpec((B_GROUPS, B_BLOCK, B_BLOCK), lambda i: (0, 0, 0)),
            pl.BlockSpec((B_BLOCK, B_GROUPS), lambda i: (0, 0)),
        ],
        out_specs=pl.BlockSpec((tm, B_WIDTH), lambda i: (i, 0)),
        out_shape=jax.ShapeDtypeStruct((M, B_WIDTH), BF16),
        compiler_params=_cparams("parallel"),
        name="gmlp",
    )(pb, pb, ws_l, bias_l.T)


def _key_to_float(key):
    return lax.bitcast_convert_type(jnp.where(key < 0, key ^ jnp.int32(0x7FFFFFFF), key), F32)


def _dsa_kernel(q_ref, iq_ref, mq_ref, k_ref, v_ref, mk_ref, o_ref, isc_ref, p_ref, iqh_ref, iw_ref,
                *, topk, widths):
    tq = q_ref.shape[0]
    half = tq // 2
    S = k_ref.shape[0]
    scale = C_DH ** -0.5
    int_min = jnp.int32(-2 ** 31)
    step = pl.program_id(1)
    lane = lax.broadcasted_iota(jnp.int32, (half, LANES), 1)
    zero = jnp.zeros((half, LANES), jnp.int32)

    for h in range(IDX_HEADS):
        iqh_ref[h] = iq_ref[:, h * IDX_DH:(h + 1) * IDX_DH]
        iw_ref[h] = jnp.broadcast_to(
            mq_ref[:, IW_LANE0 + h:IW_LANE0 + h + 1].astype(F32), (tq, LANES))

    def body(W):
        subs = [(slice(0, half), W - half), (slice(half, tq), W)]

        def chunk(rows, c):
            return isc_ref[rows, c * LANES:(c + 1) * LANES]

        def count(rows, extent, pred):
            acc = jnp.zeros((half, LANES), F32)
            for c in range(extent // LANES):
                acc = acc + jnp.where(pred(chunk(rows, c), c), 1.0, 0.0)
            return jnp.broadcast_to(jnp.sum(acc, axis=-1, keepdims=True), acc.shape)

        isc_ref[:, 0:W] = jnp.zeros((tq, W), F32)

        def indexer_head(h, carry):
            for rows, extent in subs:
                d = lax.dot_general(iqh_ref[h, rows, :], mk_ref[0:extent, 0:IDX_DH], NT_DIMS,
                                    preferred_element_type=F32)
                w = iw_ref[h, rows, :]
                for c in range(extent // LANES):
                    isc_ref[rows, c * LANES:(c + 1) * LANES] = (
                        chunk(rows, c) + jnp.maximum(d[:, c * LANES:(c + 1) * LANES], 0.0) * w)
            return carry

        lax.fori_loop(0, IDX_HEADS, indexer_head, 0)
        for r, (rows, extent) in enumerate(subs):
            isc_ref[rows, 0:extent] = jnp.where(
                _chunk_mask(step * tq + r * half, half, extent), isc_ref[rows, 0:extent], NEG)

        def value_step(i, t_us):
            out = []
            for (rows, extent), t_u in zip(subs, t_us):
                cand_u = t_u | lax.shift_left(jnp.int32(1), jnp.int32(31) - i)
                cand_f = _key_to_float(cand_u ^ int_min)
                cnt = (count(rows, extent, lambda ch, c: ch >= cand_f)
                       + jnp.where(cand_f <= NEG, float(S - extent), 0.0))
                out.append(jnp.where(cnt >= topk, cand_u, t_u))
            return tuple(out)

        thrs = [_key_to_float(t_u ^ int_min) for t_u in lax.fori_loop(0, 32, value_step, (zero, zero))]
        needs = [topk - count(rows, extent, lambda ch, c: ch > thr)
                 for (rows, extent), thr in zip(subs, thrs)]
        n_ties = [count(rows, extent, lambda ch, c: ch == thr)
                  for (rows, extent), thr in zip(subs, thrs)]

        p_ref[...] = jnp.full((tq, LANES), S - 1, jnp.int32)
        surplus = jnp.concatenate(
            [jnp.where(n > need, 1.0, 0.0) for n, need in zip(n_ties, needs)], axis=0)

        @pl.when(jnp.max(surplus) > 0.0)
        def _():
            for (rows, extent), thr, need in zip(subs, thrs, needs):
                def index_step(i, p):
                    cand = p | lax.shift_right_logical(jnp.int32(S // 2), i)
                    ties_below = count(
                        rows, extent, lambda ch, c: (ch == thr) & (lane + c * LANES < cand))
                    return jnp.where(ties_below < need, cand, p)
                p_ref[rows, :] = lax.fori_loop(0, int(math.log2(S)), index_step, zero)

        for (rows, extent), thr in zip(subs, thrs):
            p_last = p_ref[rows, :]
            for c in range(extent // LANES):
                ch = chunk(rows, c)
                sel = ((ch > thr) | ((ch == thr) & (lane + c * LANES <= p_last))) & (ch > 0.5 * NEG)
                isc_ref[rows, c * LANES:(c + 1) * LANES] = jnp.where(sel, 0.0, NEG)

        def attention_head(h, carry):
            sl = pl.ds(pl.multiple_of(h * C_DH, C_DH), C_DH)
            for rows, extent in subs:
                s = lax.dot_general(q_ref[rows, sl], k_ref[0:extent, sl], NT_DIMS,
                                    preferred_element_type=F32)
                s = s + isc_ref[rows, 0:extent]
                e = jnp.exp2((s - jnp.max(s, axis=-1, keepdims=True)) * (scale * LOG2_E))
                o = jnp.dot(e.astype(BF16), v_ref[0:extent, sl], preferred_element_type=F32)
                o_ref[rows, sl] = (o * (1.0 / jnp.sum(e, axis=-1, keepdims=True))).astype(o_ref.dtype)
            return carry

        lax.fori_loop(0, C_HEADS, attention_head, 0)

    for ci, W in enumerate(widths):
        pl.when(step == ci)(functools.partial(body, W))


def _dsa(pb, B, S):
    tq = min(512, S)
    nq = S // tq
    topk = min(TOPK_MAX, S // 4)
    widths = tuple(tq * (i + 1) for i in range(nq))
    qspec = lambda t: pl.BlockSpec((tq, PROJ_TILE), lambda b, i: (b * nq + i, t))
    kspec = lambda t: pl.BlockSpec((S, PROJ_TILE), lambda b, i: (b, t))
    return pl.pallas_call(
        functools.partial(_dsa_kernel, topk=topk, widths=widths),
        grid=(B, nq),
        in_specs=[qspec(T_CQ), qspec(T_IQ), qspec(T_MISC), kspec(T_CK), kspec(T_CV), kspec(T_MISC)],
        out_specs=pl.BlockSpec((tq, C_WIDTH), lambda b, i: (b * nq + i, 0)),
        out_shape=jax.ShapeDtypeStruct((B * S, C_WIDTH), BF16),
        scratch_shapes=[pltpu.VMEM((tq, S), F32), pltpu.VMEM((tq, LANES), jnp.int32),
                        pltpu.VMEM((IDX_HEADS, tq, IDX_DH), BF16),
                        pltpu.VMEM((IDX_HEADS, tq, LANES), F32)],
        compiler_params=_cparams("parallel", "arbitrary"),
        name="dsa",
    )(pb, pb, pb, pb, pb, pb)


def _outproj_kernel(x_ref, mod_ref, oa_ref, ob_ref, oc_ref, w_ref, o_ref):
    y = jnp.dot(oa_ref[...], w_ref[0:A_WIDTH, :], preferred_element_type=F32)
    y = y + jnp.dot(ob_ref[...], w_ref[A_WIDTH:A_WIDTH + B_WIDTH, :], preferred_element_type=F32)
    y = y + jnp.dot(oc_ref[...], w_ref[A_WIDTH + B_WIDTH:, :], preferred_element_type=F32)
    o_ref[...] = x_ref[...] + mod_ref[0, 2:3, :] * y


def _outproj(x2, mod_l, oa, ob, oc, w, S):
    M, D = x2.shape
    tm = min(512, S)
    nrow = S // tm
    row = lambda width: pl.BlockSpec((tm, width), lambda i: (i, 0))
    return pl.pallas_call(
        _outproj_kernel,
        grid=(M // tm,),
        in_specs=[
            row(D),
            pl.BlockSpec((1, 6, D), lambda i: (i // nrow, 0, 0)),
            row(A_WIDTH), row(B_WIDTH), row(C_WIDTH),
            pl.BlockSpec((D, D), lambda i: (0, 0)),
        ],
        out_specs=row(D),
        out_shape=jax.ShapeDtypeStruct((M, D), F32),
        compiler_params=_cparams("parallel"),
        name="outproj",
    )(x2, mod_l, oa, ob, oc, w)


def _ffn_kernel(x_ref, mod_ref, w1_ref, w2_ref, *rest, convert_next):
    if convert_next:
        src, (o_ref, *dst), (h_ref, acc_ref) = rest[:4], rest[4:9], rest[9:]
        for s_ref, d_ref in zip(src, dst):
            n = s_ref.shape[-1]
            d_ref[:, 0:n] = s_ref[...].astype(BF16)
            if d_ref.shape[-1] > n:
                d_ref[:, n:] = jnp.zeros((d_ref.shape[0], d_ref.shape[-1] - n), BF16)
    else:
        o_ref, h_ref, acc_ref = rest
    j = pl.program_id(1)

    @pl.when(j == 0)
    def _():
        y = _rms(x_ref[...])
        h_ref[...] = (y * (1.0 + mod_ref[0, 4:5, :]) + mod_ref[0, 3:4, :]).astype(BF16)
        acc_ref[...] = jnp.zeros_like(acc_ref)

    hid = jnp.square(jnp.maximum(jnp.dot(h_ref[...], w1_ref[...], preferred_element_type=F32), 0.0))
    acc_ref[...] += jnp.dot(hid.astype(BF16), w2_ref[...], preferred_element_type=F32)

    @pl.when(j == pl.num_programs(1) - 1)
    def _():
        o_ref[...] = x_ref[...] + mod_ref[0, 5:6, :] * acc_ref[...]


def _ffn(x2, mod_l, w1, w2, S, next_weights=None):
    M, D = x2.shape
    F = w1.shape[-1]
    tm = min(512, S)
    tf = 1024
    nrow = S // tm
    nsteps = (M // tm) * (F // tf)
    in_specs = [
        pl.BlockSpec((tm, D), lambda i, j: (i, 0)),
        pl.BlockSpec((1, 6, D), lambda i, j: (i // nrow, 0, 0)),
        pl.BlockSpec((D, tf), lambda i, j: (0, j)),
        pl.BlockSpec((tf, D), lambda i, j: (j, 0)),
    ]
    out_specs = [pl.BlockSpec((tm, D), lambda i, j: (i, 0))]
    out_shape = [jax.ShapeDtypeStruct((M, D), F32)]
    operands = [x2, mod_l, w1, w2]
    if next_weights is not None:
        stacked, layer = next_weights
        for w in stacked:
            rows, cols = w.shape[1:]
            slab = rows // nsteps
            assert slab * nsteps == rows and slab % 16 == 0
            out_cols = N_PROJ if cols == N_IN else cols
            in_specs.append(pl.BlockSpec(
                (None, slab, cols), lambda i, j: (layer, i * (F // tf) + j, 0)))
            out_specs.append(pl.BlockSpec((slab, out_cols), lambda i, j: (i * (F // tf) + j, 0)))
            out_shape.append(jax.ShapeDtypeStruct((rows, out_cols), BF16))
            operands.append(w)
    outs = pl.pallas_call(
        functools.partial(_ffn_kernel, convert_next=next_weights is not None),
        grid=(M // tm, F // tf),
        in_specs=in_specs,
        out_specs=out_specs,
        out_shape=out_shape,
        scratch_shapes=[pltpu.VMEM((tm, D), BF16), pltpu.VMEM((tm, D), F32)],
        compiler_params=_cparams("arbitrary", "arbitrary"),
        name="ffn",
    )(*operands)
    return outs[0], tuple(outs[1:])


def _rope_tables(S):
    pos = jnp.arange(S, dtype=F32)[:, None]

    def angles(dim):
        inv = 1.0 / (ROPE_THETA ** (jnp.arange(0, dim, 2, dtype=F32) / dim))
        ang = pos * inv[None, :]
        return jnp.concatenate([ang, ang], axis=-1)

    a128 = angles(A_DQK)
    half = jnp.arange(LANES) < LANES // 2
    cos128, sin128 = jnp.cos(a128), jnp.sin(a128)
    a64 = jnp.tile(angles(IDX_DH), (1, 2))
    cos64, sin64 = jnp.cos(a64), jnp.sin(a64)
    low = (jnp.arange(LANES) % IDX_DH) < IDX_DH // 2
    return jnp.stack([
        cos128, jnp.where(half, -sin128, sin128),
        cos64, jnp.where(low, -sin64, 0.0), jnp.where(low, 0.0, sin64),
    ])


def _gain_rows(a_qnorm_l, a_knorm_l, b_vnorm_l, c_qnorm_l, c_knorm_l):
    rep = PROJ_TILE // LANES
    one = jnp.ones((PROJ_TILE,), F32)
    rows = [one] * N_PROJ_TILES
    rows[T_AQ] = rows[T_AQ + 1] = jnp.tile(a_qnorm_l, rep)
    rows[T_AK] = rows[T_AK + 1] = jnp.tile(a_knorm_l, rep)
    rows[T_BV] = b_vnorm_l
    rows[T_CQ] = jnp.tile(c_qnorm_l, rep)
    rows[T_CK] = jnp.tile(c_knorm_l, rep)
    return jnp.stack(rows)[:, None, :]


def kernel(x, c, w_mod, b_mod, w_in, w_out, a_qnorm, a_knorm, a_lambda, a_subln, b_vnorm, b_ws,
           b_bias, c_qnorm, c_knorm, w_ff1, w_ff2):
    B, S, D = x.shape
    L = w_mod.shape[0]
    assert D == D_MODEL and S % Q_BLOCK == 0 and w_in.shape[-1] == N_IN

    rope = _rope_tables(S)
    weights_b = (jnp.pad(w_in[0], ((0, 0), (0, N_PROJ - N_IN))).astype(BF16),
                 w_out[0].astype(BF16), w_ff1[0].astype(BF16), w_ff2[0].astype(BF16))

    mod = _modulation(c, w_mod, b_mod).reshape(L, B, 6, D)
    x2 = x.reshape(B * S, D)
    for l in range(L):
        w_in_b, w_out_b, w_ff1_b, w_ff2_b = weights_b
        lambda_init = 0.8 - 0.6 * math.exp(-0.3 * l)
        gains = _gain_rows(a_qnorm[l], a_knorm[l], b_vnorm[l], c_qnorm[l], c_knorm[l])
        pb = _inproj(x2, mod[l], w_in_b, gains, rope, S)
        oa = _diffattn(pb, a_lambda[l], a_subln[l], lambda_init, B, S)
        ob = _gmlp(pb, b_ws[l], b_bias[l], S)
        oc = _dsa(pb, B, S)
        x2 = _outproj(x2, mod[l], oa, ob, oc, w_out_b, S)
        next_weights = ((w_in, w_out, w_ff1, w_ff2), l + 1) if l + 1 < L else None
        x2, weights_b = _ffn(x2, mod[l], w_ff1_b, w_ff2_b, S, next_weights)
    return x2.reshape(B, S, D)
```

```python
import functools
import math

import jax
import jax.numpy as jnp
import numpy as np
from jax import lax
from jax.experimental import pallas as pl
from jax.experimental.pallas import tpu as pltpu

D_MODEL = 2048
CHUNK = 64
ROPE_THETA = 10000.0
EPS = 1e-6
NEG = -1e30
D_FF = 4 * D_MODEL
A_HEADS = 4
A_DQK = D_MODEL // 16
A_DV = 2 * A_DQK
A_WIDTH = A_HEADS * A_DV
B_GROUPS = 4
B_CH = D_MODEL // 16
B_WIDTH = B_GROUPS * B_CH
B_BLOCK = 128
C_HEADS = 4
C_DH = D_MODEL // 16
C_WIDTH = C_HEADS * C_DH
IDX_HEADS = 8
IDX_DH = D_MODEL // 32
TOPK_MAX = 256
Q_BLOCK = 128

LANES = 128
VMEM_LIMIT_BYTES = 56 * 1024 * 1024

PROJ_TILE = 512
T_AQ, T_AK, T_AV, T_BU, T_BV, T_CQ, T_CK, T_CV, T_IQ, T_MISC = 0, 2, 4, 6, 7, 8, 9, 10, 11, 12
N_PROJ_TILES = 13
N_IN = 3 * A_WIDTH + 2 * B_WIDTH + 3 * C_WIDTH + IDX_HEADS * IDX_DH + IDX_DH + IDX_HEADS
N_PROJ = N_PROJ_TILES * PROJ_TILE
IW_LANE0 = IDX_DH
IW_SCALE = IDX_HEADS ** -0.5 * IDX_DH ** -0.5

BF16 = jnp.bfloat16
F32 = jnp.float32
NT_DIMS = (((1,), (1,)), ((), ()))
LOG2_E = math.log2(math.e)


def _cparams(*sem):
    return pltpu.CompilerParams(dimension_semantics=sem, vmem_limit_bytes=VMEM_LIMIT_BYTES)


def _rms(v, gain=None):
    y = v * lax.rsqrt(jnp.mean(v * v, axis=-1, keepdims=True) + EPS)
    return y if gain is None else y * gain


def _mod_kernel(c_ref, w_ref, b_ref, o_ref):
    @pl.when(pl.program_id(1) == 0)
    def _():
        o_ref[...] = jnp.broadcast_to(b_ref[...], o_ref.shape)

    ca = jax.nn.silu(c_ref[...]).astype(BF16)
    o_ref[...] += jnp.dot(ca, w_ref[...].astype(BF16), preferred_element_type=F32)


def _modulation(c, w_mod, b_mod):
    L, D, E = w_mod.shape
    B = c.shape[0]
    td = 256
    return pl.pallas_call(
        _mod_kernel,
        grid=(L, D // td),
        in_specs=[
            pl.BlockSpec((B, td), lambda l, k: (0, k)),
            pl.BlockSpec((None, td, E), lambda l, k: (l, k, 0)),
            pl.BlockSpec((None, 1, E), lambda l, k: (l, 0, 0)),
        ],
        out_specs=pl.BlockSpec((None, B, E), lambda l, k: (l, 0, 0)),
        out_shape=jax.ShapeDtypeStruct((L, B, E), F32),
        compiler_params=_cparams("parallel", "arbitrary"),
        name="modulation",
    )(c, w_mod, b_mod.reshape(L, 1, E))


def _inproj_kernel(x_ref, mod_ref, w_ref, gain_ref, rope_ref, o_ref, h_ref):
    y = _rms(x_ref[...])
    h_ref[...] = (y * (1.0 + mod_ref[0, 1:2, :]) + mod_ref[0, 0:1, :]).astype(BF16)
    groups = PROJ_TILE // LANES

    def rope128(v):
        return v * rope_ref[0] + pltpu.roll(v, LANES // 2, 1) * rope_ref[1]

    def rope64(v):
        return (v * rope_ref[2] + pltpu.roll(v, 96, 1) * rope_ref[3]
                + pltpu.roll(v, 32, 1) * rope_ref[4])

    def misc(v, g):
        if g > 0:
            return v
        lane = lax.broadcasted_iota(jnp.int32, v.shape, 1)
        return jnp.where(lane < IDX_DH, rope64(v), v * IW_SCALE)

    def gain(t, g):
        return gain_ref[t, :, g * LANES:(g + 1) * LANES]

    norm_rope = lambda v, t, g: rope128(_rms(v, gain(t, g)))
    epilogue = {t: (lambda v, t, g: v) for t in range(N_PROJ_TILES)}
    epilogue.update({t: norm_rope for t in (T_AQ, T_AQ + 1, T_AK, T_AK + 1, T_CQ, T_CK)})
    epilogue[T_BU] = lambda v, t, g: jax.nn.gelu(v)
    epilogue[T_BV] = lambda v, t, g: _rms(jax.nn.gelu(v), gain(t, g))
    epilogue[T_IQ] = lambda v, t, g: rope64(v)
    epilogue[T_MISC] = lambda v, t, g: misc(v, g)

    for t in range(N_PROJ_TILES):
        acc = jnp.dot(h_ref[...], w_ref[:, t * PROJ_TILE:(t + 1) * PROJ_TILE],
                      preferred_element_type=F32)
        for g in range(groups):
            lo = t * PROJ_TILE + g * LANES
            o_ref[:, lo:lo + LANES] = epilogue[t](
                acc[:, g * LANES:(g + 1) * LANES], t, g).astype(o_ref.dtype)


def _inproj(x2, mod_l, w_all, layer, gains, rope, S):
    M, D = x2.shape
    tm = min(256, S)
    nrow = S // tm
    return pl.pallas_call(
        _inproj_kernel,
        grid=(M // tm,),
        in_specs=[
            pl.BlockSpec((tm, D), lambda i: (i, 0)),
            pl.BlockSpec((1, 6, D), lambda i: (i // nrow, 0, 0)),
            pl.BlockSpec((None, D, N_PROJ), lambda i: (layer, 0, 0), pipeline_mode=pl.Buffered(1)),
            pl.BlockSpec((N_PROJ_TILES, 1, PROJ_TILE), lambda i: (0, 0, 0)),
            pl.BlockSpec((5, tm, LANES), lambda i: (0, i % nrow, 0)),
        ],
        out_specs=pl.BlockSpec((tm, N_PROJ), lambda i: (i, 0)),
        out_shape=jax.ShapeDtypeStruct((M, N_PROJ), BF16),
        scratch_shapes=[pltpu.VMEM((tm, D), BF16)],
        compiler_params=_cparams("parallel"),
        name="inproj",
    )(x2, mod_l, w_all, gains, rope)


def _chunk_mask(row0, rows, cols):
    r = row0 + lax.broadcasted_iota(jnp.int32, (rows, cols), 0)
    c = lax.broadcasted_iota(jnp.int32, (rows, cols), 1)
    return (c // CHUNK) <= (r // CHUNK)


def _diffattn_kernel(lam_ref, subln_ref, q_ref, k_ref, v_ref, o_ref, *, lambda_init, widths):
    tq = q_ref.shape[0]
    half = tq // 2
    scale = A_DQK ** -0.5
    step = pl.program_id(0)
    lp = lam_ref[...]
    lam = (jnp.exp(jnp.sum(lp[0:1] * lp[1:2], axis=-1, keepdims=True))
           - jnp.exp(jnp.sum(lp[2:3] * lp[3:4], axis=-1, keepdims=True)) + lambda_init)

    diag_mask = _chunk_mask(0, half, half)

    def body(W):
        extent = lambda r: W - (1 - r) * half

        def scores(m, r):
            s = lax.dot_general(q_ref[r * half:(r + 1) * half, m * A_DQK:(m + 1) * A_DQK],
                                k_ref[0:extent(r), m * A_DQK:(m + 1) * A_DQK],
                                NT_DIMS, preferred_element_type=F32)
            diag = jnp.where(diag_mask, s[:, extent(r) - half:], NEG)
            return diag if extent(r) == half else jnp.concatenate([s[:, :extent(r) - half], diag], axis=1)

        def numerator(s):
            e = jnp.exp2((s - jnp.max(s, axis=-1, keepdims=True)) * (scale * LOG2_E))
            return e.astype(BF16), jnp.sum(e, axis=-1, keepdims=True)

        ss = [[scores(m, r) for m in range(2)] for r in range(2)]
        outs = []
        for r in range(2):
            (e1, l1), (e2, l2) = numerator(ss[r][0]), numerator(ss[r][1])
            o1 = jnp.dot(e1, v_ref[0:extent(r), :], preferred_element_type=F32)
            o2 = jnp.dot(e2, v_ref[0:extent(r), :], preferred_element_type=F32)
            outs.append(o1 * (1.0 / l1) - o2 * (lam / l2))
        o = jnp.concatenate(outs, axis=0)
        o_ref[...] = (_rms(o, subln_ref[...]) * (1.0 - lambda_init)).astype(o_ref.dtype)

    for ci, W in enumerate(widths):
        pl.when(step == ci)(functools.partial(body, W))


def _diffattn(pb, a_lambda_l, a_subln_l, lambda_init, B, S):
    tq = min(512, S)
    nq = S // tq
    return pl.pallas_call(
        functools.partial(_diffattn_kernel, lambda_init=lambda_init,
                          widths=tuple(tq * (i + 1) for i in range(nq))),
        grid=(nq, B, A_HEADS),
        in_specs=[
            pl.BlockSpec((4, A_DQK), lambda i, b, h: (0, 0)),
            pl.BlockSpec((1, A_DV), lambda i, b, h: (0, 0)),
            pl.BlockSpec((tq, A_DV), lambda i, b, h: (b * nq + i, h)),
            pl.BlockSpec((S, A_DV), lambda i, b, h: (b, A_HEADS + h)),
            pl.BlockSpec((S, A_DV), lambda i, b, h: (b, 2 * A_HEADS + h)),
        ],
        out_specs=pl.BlockSpec((tq, A_DV), lambda i, b, h: (b * nq + i, h)),
        out_shape=jax.ShapeDtypeStruct((B * S, A_WIDTH), BF16),
        compiler_params=_cparams("parallel", "parallel", "arbitrary"),
        name="diffattn",
    )(a_lambda_l, a_subln_l.reshape(1, A_DV), pb, pb, pb)


def _gmlp_kernel(u_ref, v_ref, ws_ref, bias_ref, o_ref):
    mask = _chunk_mask(0, B_BLOCK, B_BLOCK)
    for g in range(B_GROUPS):
        w = jnp.where(mask, ws_ref[g], 0.0).astype(BF16)
        cols = slice(g * B_CH, (g + 1) * B_CH)
        for n in range(u_ref.shape[0] // B_BLOCK):
            rows = slice(n * B_BLOCK, (n + 1) * B_BLOCK)
            z = jnp.dot(w, v_ref[rows, cols], preferred_element_type=F32) + bias_ref[:, g:g + 1]
            o_ref[rows, cols] = (u_ref[rows, cols].astype(F32) * z).astype(o_ref.dtype)


def _gmlp(pb, ws_l, bias_l, S):
    M = pb.shape[0]
    tm = min(512, S)
    return pl.pallas_call(
        _gmlp_kernel,
        grid=(M // tm,),
        in_specs=[
            pl.BlockSpec((tm, B_WIDTH), lambda i: (i, T_BU)),
            pl.BlockSpec((tm, B_WIDTH), lambda i: (i, T_BV)),
            pl.BlockSpec((B_GROUPS, B_BLOCK, B_BLOCK), lambda i: (0, 0, 0)),
            pl.BlockSpec((B_BLOCK, B_GROUPS), lambda i: (0, 0)),
        ],
        out_specs=pl.BlockSpec((tm, B_WIDTH), lambda i: (i, 0)),
        out_shape=jax.ShapeDtypeStruct((M, B_WIDTH), BF16),
        compiler_params=_cparams("parallel"),
        name="gmlp",
    )(pb, pb, ws_l, bias_l.T)


def _key_to_float(key):
    return lax.bitcast_convert_type(jnp.where(key < 0, key ^ jnp.int32(0x7FFFFFFF), key), F32)


def _dsa_kernel(q_ref, iq_ref, mq_ref, k_ref, v_ref, mk_ref, o_ref, isc_ref, p_ref, iqh_ref, iw_ref,
                *, topk, widths):
    tq = q_ref.shape[0]
    half = tq // 2
    S = k_ref.shape[0]
    scale = C_DH ** -0.5
    int_min = jnp.int32(-2 ** 31)
    step = pl.program_id(1)
    lane = lax.broadcasted_iota(jnp.int32, (half, LANES), 1)
    zero = jnp.zeros((half, LANES), jnp.int32)

    for h in range(IDX_HEADS):
        iqh_ref[h] = iq_ref[:, h * IDX_DH:(h + 1) * IDX_DH]
        iw_ref[h] = jnp.broadcast_to(
            mq_ref[:, IW_LANE0 + h:IW_LANE0 + h + 1].astype(F32), (tq, LANES))

    def body(W):
        subs = [(slice(0, half), W - half), (slice(half, tq), W)]

        def chunk(rows, c):
            return isc_ref[rows, c * LANES:(c + 1) * LANES]

        def count(rows, extent, pred):
            acc = jnp.zeros((half, LANES), F32)
            for c in range(extent // LANES):
                acc = acc + jnp.where(pred(chunk(rows, c), c), 1.0, 0.0)
            return jnp.broadcast_to(jnp.sum(acc, axis=-1, keepdims=True), acc.shape)

        isc_ref[:, 0:W] = jnp.zeros((tq, W), F32)

        def indexer_head(h, carry):
            for rows, extent in subs:
                d = lax.dot_general(iqh_ref[h, rows, :], mk_ref[0:extent, 0:IDX_DH], NT_DIMS,
                                    preferred_element_type=F32)
                w = iw_ref[h, rows, :]
                for c in range(extent // LANES):
                    isc_ref[rows, c * LANES:(c + 1) * LANES] = (
                        chunk(rows, c) + jnp.maximum(d[:, c * LANES:(c + 1) * LANES], 0.0) * w)
            return carry

        lax.fori_loop(0, IDX_HEADS, indexer_head, 0)
        for r, (rows, extent) in enumerate(subs):
            isc_ref[rows, 0:extent] = jnp.where(
                _chunk_mask(step * tq + r * half, half, extent), isc_ref[rows, 0:extent], NEG)

        def value_step(i, t_us):
            out = []
            for (rows, extent), t_u in zip(subs, t_us):
                cand_u = t_u | lax.shift_left(jnp.int32(1), jnp.int32(31) - i)
                cand_f = _key_to_float(cand_u ^ int_min)
                cnt = (count(rows, extent, lambda ch, c: ch >= cand_f)
                       + jnp.where(cand_f <= NEG, float(S - extent), 0.0))
                out.append(jnp.where(cnt >= topk, cand_u, t_u))
            return tuple(out)

        thrs = [_key_to_float(t_u ^ int_min) for t_u in lax.fori_loop(0, 32, value_step, (zero, zero))]
        needs = [topk - count(rows, extent, lambda ch, c: ch > thr)
                 for (rows, extent), thr in zip(subs, thrs)]
        n_ties = [count(rows, extent, lambda ch, c: ch == thr)
                  for (rows, extent), thr in zip(subs, thrs)]

        p_ref[...] = jnp.full((tq, LANES), S - 1, jnp.int32)
        surplus = jnp.concatenate(
            [jnp.where(n > need, 1.0, 0.0) for n, need in zip(n_ties, needs)], axis=0)

        @pl.when(jnp.max(surplus) > 0.0)
        def _():
            for (rows, extent), thr, need in zip(subs, thrs, needs):
                def index_step(i, p):
                    cand = p | lax.shift_right_logical(jnp.int32(S // 2), i)
                    ties_below = count(
                        rows, extent, lambda ch, c: (ch == thr) & (lane + c * LANES < cand))
                    return jnp.where(ties_below < need, cand, p)
                p_ref[rows, :] = lax.fori_loop(0, int(math.log2(S)), index_step, zero)

        for (rows, extent), thr in zip(subs, thrs):
            p_last = p_ref[rows, :]
            for c in range(extent // LANES):
                ch = chunk(rows, c)
                sel = ((ch > thr) | ((ch == thr) & (lane + c * LANES <= p_last))) & (ch > 0.5 * NEG)
                isc_ref[rows, c * LANES:(c + 1) * LANES] = jnp.where(sel, 0.0, NEG)

        def attention_head(h, carry):
            sl = pl.ds(pl.multiple_of(h * C_DH, C_DH), C_DH)
            for rows, extent in subs:
                s = lax.dot_general(q_ref[rows, sl], k_ref[0:extent, sl], NT_DIMS,
                                    preferred_element_type=F32)
                s = s + isc_ref[rows, 0:extent]
                e = jnp.exp2((s - jnp.max(s, axis=-1, keepdims=True)) * (scale * LOG2_E))
                o = jnp.dot(e.astype(BF16), v_ref[0:extent, sl], preferred_element_type=F32)
                o_ref[rows, sl] = (o * (1.0 / jnp.sum(e, axis=-1, keepdims=True))).astype(o_ref.dtype)
            return carry

        lax.fori_loop(0, C_HEADS, attention_head, 0)

    for ci, W in enumerate(widths):
        pl.when(step == ci)(functools.partial(body, W))


def _dsa(pb, B, S):
    tq = min(512, S)
    nq = S // tq
    topk = min(TOPK_MAX, S // 4)
    widths = tuple(tq * (i + 1) for i in range(nq))
    qspec = lambda t: pl.BlockSpec((tq, PROJ_TILE), lambda b, i: (b * nq + i, t))
    kspec = lambda t: pl.BlockSpec((S, PROJ_TILE), lambda b, i: (b, t))
    return pl.pallas_call(
        functools.partial(_dsa_kernel, topk=topk, widths=widths),
        grid=(B, nq),
        in_specs=[qspec(T_CQ), qspec(T_IQ), qspec(T_MISC), kspec(T_CK), kspec(T_CV), kspec(T_MISC)],
        out_specs=pl.BlockSpec((tq, C_WIDTH), lambda b, i: (b * nq + i, 0)),
        out_shape=jax.ShapeDtypeStruct((B * S, C_WIDTH), BF16),
        scratch_shapes=[pltpu.VMEM((tq, S), F32), pltpu.VMEM((tq, LANES), jnp.int32),
                        pltpu.VMEM((IDX_HEADS, tq, IDX_DH), BF16),
                        pltpu.VMEM((IDX_HEADS, tq, LANES), F32)],
        compiler_params=_cparams("parallel", "arbitrary"),
        name="dsa",
    )(pb, pb, pb, pb, pb, pb)


def _outproj_kernel(x_ref, mod_ref, oa_ref, ob_ref, oc_ref, w_ref, o_ref):
    y = jnp.dot(oa_ref[...], w_ref[0:A_WIDTH, :], preferred_element_type=F32)
    y = y + jnp.dot(ob_ref[...], w_ref[A_WIDTH:A_WIDTH + B_WIDTH, :], preferred_element_type=F32)
    y = y + jnp.dot(oc_ref[...], w_ref[A_WIDTH + B_WIDTH:, :], preferred_element_type=F32)
    o_ref[...] = x_ref[...] + mod_ref[0, 2:3, :] * y


def _outproj(x2, mod_l, oa, ob, oc, w, S):
    M, D = x2.shape
    tm = min(512, S)
    nrow = S // tm
    row = lambda width: pl.BlockSpec((tm, width), lambda i: (i, 0))
    return pl.pallas_call(
        _outproj_kernel,
        grid=(M // tm,),
        in_specs=[
            row(D),
            pl.BlockSpec((1, 6, D), lambda i: (i // nrow, 0, 0)),
            row(A_WIDTH), row(B_WIDTH), row(C_WIDTH),
            pl.BlockSpec((D, D), lambda i: (0, 0)),
        ],
        out_specs=row(D),
        out_shape=jax.ShapeDtypeStruct((M, D), F32),
        compiler_params=_cparams("parallel"),
        name="outproj",
    )(x2, mod_l, oa, ob, oc, w)


def _ffn_kernel(x_ref, mod_ref, w1_ref, w2_ref, *rest, convert_next):
    if convert_next:
        n = (len(rest) - 3) // 2
        src, o_ref, dst, (h_ref, acc_ref) = rest[:n], rest[n], rest[n + 1:2 * n + 1], rest[2 * n + 1:]
        for s_ref, d_ref in zip(src, dst):
            d_ref[...] = s_ref[...].astype(BF16)
    else:
        o_ref, h_ref, acc_ref = rest
    j = pl.program_id(1)

    @pl.when(j == 0)
    def _():
        y = _rms(x_ref[...])
        h_ref[...] = (y * (1.0 + mod_ref[0, 4:5, :]) + mod_ref[0, 3:4, :]).astype(BF16)
        acc_ref[...] = jnp.zeros_like(acc_ref)

    hid = jnp.square(jnp.maximum(jnp.dot(h_ref[...], w1_ref[...], preferred_element_type=F32), 0.0))
    acc_ref[...] += jnp.dot(hid.astype(BF16), w2_ref[...], preferred_element_type=F32)

    @pl.when(j == pl.num_programs(1) - 1)
    def _():
        o_ref[...] = x_ref[...] + mod_ref[0, 5:6, :] * acc_ref[...]


def _ffn(x2, mod_l, w1, w2, S, next_weights=None):
    M, D = x2.shape
    F = w1.shape[-1]
    tm = min(512, S)
    tf = 1024
    nrow = S // tm
    nsteps = (M // tm) * (F // tf)
    in_specs = [
        pl.BlockSpec((tm, D), lambda i, j: (i, 0)),
        pl.BlockSpec((1, 6, D), lambda i, j: (i // nrow, 0, 0)),
        pl.BlockSpec((D, tf), lambda i, j: (0, j)),
        pl.BlockSpec((tf, D), lambda i, j: (j, 0)),
    ]
    out_specs = [pl.BlockSpec((tm, D), lambda i, j: (i, 0))]
    out_shape = [jax.ShapeDtypeStruct((M, D), F32)]
    operands = [x2, mod_l, w1, w2]
    if next_weights is not None:
        stacked, layer = next_weights
        for w in stacked:
            rows, cols = w.shape[1:]
            slab = rows // nsteps
            assert slab * nsteps == rows and slab % 16 == 0
            in_specs.append(pl.BlockSpec(
                (None, slab, cols), lambda i, j: (layer, i * (F // tf) + j, 0)))
            out_specs.append(pl.BlockSpec((slab, cols), lambda i, j: (i * (F // tf) + j, 0)))
            out_shape.append(jax.ShapeDtypeStruct((rows, cols), BF16))
            operands.append(w)
    outs = pl.pallas_call(
        functools.partial(_ffn_kernel, convert_next=next_weights is not None),
        grid=(M // tm, F // tf),
        in_specs=in_specs,
        out_specs=out_specs,
        out_shape=out_shape,
        scratch_shapes=[pltpu.VMEM((tm, D), BF16), pltpu.VMEM((tm, D), F32)],
        compiler_params=_cparams("arbitrary", "arbitrary"),
        name="ffn",
    )(*operands)
    return outs[0], tuple(outs[1:])


def _rope_tables(S):
    pos = jnp.arange(S, dtype=F32)[:, None]

    def angles(dim):
        inv = 1.0 / (ROPE_THETA ** (jnp.arange(0, dim, 2, dtype=F32) / dim))
        ang = pos * inv[None, :]
        return jnp.concatenate([ang, ang], axis=-1)

    a128 = angles(A_DQK)
    half = jnp.arange(LANES) < LANES // 2
    cos128, sin128 = jnp.cos(a128), jnp.sin(a128)
    a64 = jnp.tile(angles(IDX_DH), (1, 2))
    cos64, sin64 = jnp.cos(a64), jnp.sin(a64)
    low = (jnp.arange(LANES) % IDX_DH) < IDX_DH // 2
    return jnp.stack([
        cos128, jnp.where(half, -sin128, sin128),
        cos64, jnp.where(low, -sin64, 0.0), jnp.where(low, 0.0, sin64),
    ])


def _gain_rows(a_qnorm_l, a_knorm_l, b_vnorm_l, c_qnorm_l, c_knorm_l):
    rep = PROJ_TILE // LANES
    one = jnp.ones((PROJ_TILE,), F32)
    rows = [one] * N_PROJ_TILES
    rows[T_AQ] = rows[T_AQ + 1] = jnp.tile(a_qnorm_l, rep)
    rows[T_AK] = rows[T_AK + 1] = jnp.tile(a_knorm_l, rep)
    rows[T_BV] = b_vnorm_l
    rows[T_CQ] = jnp.tile(c_qnorm_l, rep)
    rows[T_CK] = jnp.tile(c_knorm_l, rep)
    return jnp.stack(rows)[:, None, :]


def kernel(x, c, w_mod, b_mod, w_in, w_out, a_qnorm, a_knorm, a_lambda, a_subln, b_vnorm, b_ws,
           b_bias, c_qnorm, c_knorm, w_ff1, w_ff2):
    B, S, D = x.shape
    L = w_mod.shape[0]
    assert D == D_MODEL and S % Q_BLOCK == 0 and w_in.shape[-1] == N_IN

    rope = _rope_tables(S)
    w_in_b = lax.dynamic_update_slice(
        jnp.zeros((L, D, N_PROJ), BF16), w_in.astype(BF16), (0, 0, 0))
    weights_b = (w_out[0].astype(BF16), w_ff1[0].astype(BF16), w_ff2[0].astype(BF16))

    mod = _modulation(c, w_mod, b_mod).reshape(L, B, 6, D)
    x2 = x.reshape(B * S, D)
    for l in range(L):
        w_out_b, w_ff1_b, w_ff2_b = weights_b
        lambda_init = 0.8 - 0.6 * math.exp(-0.3 * l)
        gains = _gain_rows(a_qnorm[l], a_knorm[l], b_vnorm[l], c_qnorm[l], c_knorm[l])
        pb = _inproj(x2, mod[l], w_in_b, l, gains, rope, S)
        oa = _diffattn(pb, a_lambda[l], a_subln[l], lambda_init, B, S)
        ob = _gmlp(pb, b_ws[l], b_bias[l], S)
        oc = _dsa(pb, B, S)
        x2 = _outproj(x2, mod[l], oa, ob, oc, w_out_b, S)
        next_weights = ((w_out, w_ff1, w_ff2), l + 1) if l + 1 < L else None
        x2, weights_b = _ffn(x2, mod[l], w_ff1_b, w_ff2_b, S, next_weights)
    return x2.reshape(B, S, D)
```

```python
import functools
import math

import jax
import jax.numpy as jnp
import numpy as np
from jax import lax
from jax.experimental import pallas as pl
from jax.experimental.pallas import tpu as pltpu

D_MODEL = 2048
CHUNK = 64
ROPE_THETA = 10000.0
EPS = 1e-6
NEG = -1e30
D_FF = 4 * D_MODEL
A_HEADS = 4
A_DQK = D_MODEL // 16
A_DV = 2 * A_DQK
A_WIDTH = A_HEADS * A_DV
B_GROUPS = 4
B_CH = D_MODEL // 16
B_WIDTH = B_GROUPS * B_CH
B_BLOCK = 128
C_HEADS = 4
C_DH = D_MODEL // 16
C_WIDTH = C_HEADS * C_DH
IDX_HEADS = 8
IDX_DH = D_MODEL // 32
TOPK_MAX = 256
Q_BLOCK = 128

LANES = 128
VMEM_LIMIT_BYTES = 56 * 1024 * 1024

PROJ_TILE = 512
T_AQ, T_AK, T_AV, T_BU, T_BV, T_CQ, T_CK, T_CV, T_IQ, T_MISC = 0, 2, 4, 6, 7, 8, 9, 10, 11, 12
N_PROJ_TILES = 13
N_IN = 3 * A_WIDTH + 2 * B_WIDTH + 3 * C_WIDTH + IDX_HEADS * IDX_DH + IDX_DH + IDX_HEADS
N_PROJ = N_PROJ_TILES * PROJ_TILE
IW_LANE0 = IDX_DH
IW_SCALE = IDX_HEADS ** -0.5 * IDX_DH ** -0.5

BF16 = jnp.bfloat16
F32 = jnp.float32
NT_DIMS = (((1,), (1,)), ((), ()))
LOG2_E = math.log2(math.e)


def _cparams(*sem):
    return pltpu.CompilerParams(dimension_semantics=sem, vmem_limit_bytes=VMEM_LIMIT_BYTES)


def _rms(v, gain=None):
    y = v * lax.rsqrt(jnp.mean(v * v, axis=-1, keepdims=True) + EPS)
    return y if gain is None else y * gain


def _mod_kernel(c_ref, w_ref, b_ref, o_ref):
    @pl.when(pl.program_id(1) == 0)
    def _():
        o_ref[...] = jnp.broadcast_to(b_ref[...], o_ref.shape)

    ca = jax.nn.silu(c_ref[...]).astype(BF16)
    o_ref[...] += jnp.dot(ca, w_ref[...].astype(BF16), preferred_element_type=F32)


def _modulation(c, w_mod, b_mod):
    L, D, E = w_mod.shape
    B = c.shape[0]
    td = 256
    return pl.pallas_call(
        _mod_kernel,
        grid=(L, D // td),
        in_specs=[
            pl.BlockSpec((B, td), lambda l, k: (0, k)),
            pl.BlockSpec((None, td, E), lambda l, k: (l, k, 0)),
            pl.BlockSpec((None, 1, E), lambda l, k: (l, 0, 0)),
        ],
        out_specs=pl.BlockSpec((None, B, E), lambda l, k: (l, 0, 0)),
        out_shape=jax.ShapeDtypeStruct((L, B, E), F32),
        compiler_params=_cparams("parallel", "arbitrary"),
        name="modulation",
    )(c, w_mod, b_mod.reshape(L, 1, E))


def _inproj_kernel(x_ref, mod_ref, w_ref, gain_ref, rope_ref, o_ref, h_ref):
    y = _rms(x_ref[...])
    h_ref[...] = (y * (1.0 + mod_ref[0, 1:2, :]) + mod_ref[0, 0:1, :]).astype(BF16)
    groups = PROJ_TILE // LANES

    def rope128(v):
        return v * rope_ref[0] + pltpu.roll(v, LANES // 2, 1) * rope_ref[1]

    def rope64(v):
        return (v * rope_ref[2] + pltpu.roll(v, 96, 1) * rope_ref[3]
                + pltpu.roll(v, 32, 1) * rope_ref[4])

    def misc(v, g):
        if g > 0:
            return v
        lane = lax.broadcasted_iota(jnp.int32, v.shape, 1)
        return jnp.where(lane < IDX_DH, rope64(v), v * IW_SCALE)

    def gain(t, g):
        return gain_ref[t, :, g * LANES:(g + 1) * LANES]

    norm_rope = lambda v, t, g: rope128(_rms(v, gain(t, g)))
    epilogue = {t: (lambda v, t, g: v) for t in range(N_PROJ_TILES)}
    epilogue.update({t: norm_rope for t in (T_AQ, T_AQ + 1, T_AK, T_AK + 1, T_CQ, T_CK)})
    epilogue[T_BU] = lambda v, t, g: jax.nn.gelu(v)
    epilogue[T_BV] = lambda v, t, g: _rms(jax.nn.gelu(v), gain(t, g))
    epilogue[T_IQ] = lambda v, t, g: rope64(v)
    epilogue[T_MISC] = lambda v, t, g: misc(v, g)

    for t in range(N_PROJ_TILES):
        acc = jnp.dot(h_ref[...], w_ref[:, t * PROJ_TILE:(t + 1) * PROJ_TILE],
                      preferred_element_type=F32)
        for g in range(groups):
            lo = t * PROJ_TILE + g * LANES
            o_ref[:, lo:lo + LANES] = epilogue[t](
                acc[:, g * LANES:(g + 1) * LANES], t, g).astype(o_ref.dtype)


def _inproj(x2, mod_l, w_all, layer, gains, rope, S):
    M, D = x2.shape
    tm = min(256, S)
    nrow = S // tm
    return pl.pallas_call(
        _inproj_kernel,
        grid=(M // tm,),
        in_specs=[
            pl.BlockSpec((tm, D), lambda i: (i, 0)),
            pl.BlockSpec((1, 6, D), lambda i: (i // nrow, 0, 0)),
            pl.BlockSpec((None, D, N_PROJ), lambda i: (layer, 0, 0), pipeline_mode=pl.Buffered(1)),
            pl.BlockSpec((N_PROJ_TILES, 1, PROJ_TILE), lambda i: (0, 0, 0)),
            pl.BlockSpec((5, tm, LANES), lambda i: (0, i % nrow, 0)),
        ],
        out_specs=pl.BlockSpec((tm, N_PROJ), lambda i: (i, 0)),
        out_shape=jax.ShapeDtypeStruct((M, N_PROJ), BF16),
        scratch_shapes=[pltpu.VMEM((tm, D), BF16)],
        compiler_params=_cparams("parallel"),
        name="inproj",
    )(x2, mod_l, w_all, gains, rope)


def _chunk_mask(row0, rows, cols):
    r = row0 + lax.broadcasted_iota(jnp.int32, (rows, cols), 0)
    c = lax.broadcasted_iota(jnp.int32, (rows, cols), 1)
    return (c // CHUNK) <= (r // CHUNK)


def _diffattn_kernel(lam_ref, subln_ref, q_ref, k_ref, v_ref, o_ref, *, lambda_init, widths):
    tq = q_ref.shape[0]
    half = tq // 2
    scale = A_DQK ** -0.5
    step = pl.program_id(0)
    lp = lam_ref[...]
    lam = (jnp.exp(jnp.sum(lp[0:1] * lp[1:2], axis=-1, keepdims=True))
           - jnp.exp(jnp.sum(lp[2:3] * lp[3:4], axis=-1, keepdims=True)) + lambda_init)

    diag_mask = _chunk_mask(0, half, half)

    def body(W):
        extent = lambda r: W - (1 - r) * half

        def scores(m, r):
            s = lax.dot_general(q_ref[r * half:(r + 1) * half, m * A_DQK:(m + 1) * A_DQK],
                                k_ref[0:extent(r), m * A_DQK:(m + 1) * A_DQK],
                                NT_DIMS, preferred_element_type=F32)
            diag = jnp.where(diag_mask, s[:, extent(r) - half:], NEG)
            return diag if extent(r) == half else jnp.concatenate([s[:, :extent(r) - half], diag], axis=1)

        def numerator(s):
            e = jnp.exp2((s - jnp.max(s, axis=-1, keepdims=True)) * (scale * LOG2_E))
            return e.astype(BF16), jnp.sum(e, axis=-1, keepdims=True)

        ss = [[scores(m, r) for m in range(2)] for r in range(2)]
        outs = []
        for r in range(2):
            (e1, l1), (e2, l2) = numerator(ss[r][0]), numerator(ss[r][1])
            o1 = jnp.dot(e1, v_ref[0:extent(r), :], preferred_element_type=F32)
            o2 = jnp.dot(e2, v_ref[0:extent(r), :], preferred_element_type=F32)
            outs.append(o1 * (1.0 / l1) - o2 * (lam / l2))
        o = jnp.concatenate(outs, axis=0)
        o_ref[...] = (_rms(o, subln_ref[...]) * (1.0 - lambda_init)).astype(o_ref.dtype)

    for ci, W in enumerate(widths):
        pl.when(step == ci)(functools.partial(body, W))


def _diffattn(pb, a_lambda_l, a_subln_l, lambda_init, B, S):
    tq = min(512, S)
    nq = S // tq
    return pl.pallas_call(
        functools.partial(_diffattn_kernel, lambda_init=lambda_init,
                          widths=tuple(tq * (i + 1) for i in range(nq))),
        grid=(nq, B, A_HEADS),
        in_specs=[
            pl.BlockSpec((4, A_DQK), lambda i, b, h: (0, 0)),
            pl.BlockSpec((1, A_DV), lambda i, b, h: (0, 0)),
            pl.BlockSpec((tq, A_DV), lambda i, b, h: (b * nq + i, h)),
            pl.BlockSpec((S, A_DV), lambda i, b, h: (b, A_HEADS + h)),
            pl.BlockSpec((S, A_DV), lambda i, b, h: (b, 2 * A_HEADS + h)),
        ],
        out_specs=pl.BlockSpec((tq, A_DV), lambda i, b, h: (b * nq + i, h)),
        out_shape=jax.ShapeDtypeStruct((B * S, A_WIDTH), BF16),
        compiler_params=_cparams("parallel", "parallel", "arbitrary"),
        name="diffattn",
    )(a_lambda_l, a_subln_l.reshape(1, A_DV), pb, pb, pb)


def _gmlp_kernel(u_ref, v_ref, ws_ref, bias_ref, o_ref):
    mask = _chunk_mask(0, B_BLOCK, B_BLOCK)
    for g in range(B_GROUPS):
        w = jnp.where(mask, ws_ref[g], 0.0).astype(BF16)
        cols = slice(g * B_CH, (g + 1) * B_CH)
        for n in range(u_ref.shape[0] // B_BLOCK):
            rows = slice(n * B_BLOCK, (n + 1) * B_BLOCK)
            z = jnp.dot(w, v_ref[rows, cols], preferred_element_type=F32) + bias_ref[:, g:g + 1]
            o_ref[rows, cols] = (u_ref[rows, cols].astype(F32) * z).astype(o_ref.dtype)


def _gmlp(pb, ws_l, bias_l, S):
    M = pb.shape[0]
    tm = min(512, S)
    return pl.pallas_call(
        _gmlp_kernel,
        grid=(M // tm,),
        in_specs=[
            pl.BlockSpec((tm, B_WIDTH), lambda i: (i, T_BU)),
            pl.BlockSpec((tm, B_WIDTH), lambda i: (i, T_BV)),
            pl.BlockSpec((B_GROUPS, B_BLOCK, B_BLOCK), lambda i: (0, 0, 0)),
            pl.BlockSpec((B_BLOCK, B_GROUPS), lambda i: (0, 0)),
        ],
        out_specs=pl.BlockSpec((tm, B_WIDTH), lambda i: (i, 0)),
        out_shape=jax.ShapeDtypeStruct((M, B_WIDTH), BF16),
        compiler_params=_cparams("parallel"),
        name="gmlp",
    )(pb, pb, ws_l, bias_l.T)


def _key_to_float(key):
    return lax.bitcast_convert_type(jnp.where(key < 0, key ^ jnp.int32(0x7FFFFFFF), key), F32)


def _dsa_kernel(q_ref, iq_ref, mq_ref, k_ref, v_ref, mk_ref, o_ref, isc_ref, p_ref, iqh_ref, iw_ref,
                *, topk, widths):
    tq = q_ref.shape[0]
    half = tq // 2
    S = k_ref.shape[0]
    scale = C_DH ** -0.5
    int_min = jnp.int32(-2 ** 31)
    step = pl.program_id(1)
    lane = lax.broadcasted_iota(jnp.int32, (half, LANES), 1)
    zero = jnp.zeros((half, LANES), jnp.int32)

    for h in range(IDX_HEADS):
        iqh_ref[h] = iq_ref[:, h * IDX_DH:(h + 1) * IDX_DH]
        iw_ref[h] = jnp.broadcast_to(
            mq_ref[:, IW_LANE0 + h:IW_LANE0 + h + 1].astype(F32), (tq, LANES))

    def body(W):
        subs = [(slice(0, half), W - half), (slice(half, tq), W)]

        def chunk(rows, c):
            return isc_ref[rows, c * LANES:(c + 1) * LANES]

        def count(rows, extent, pred):
            acc = jnp.zeros((half, LANES), F32)
            for c in range(extent // LANES):
                acc = acc + jnp.where(pred(chunk(rows, c), c), 1.0, 0.0)
            return jnp.broadcast_to(jnp.sum(acc, axis=-1, keepdims=True), acc.shape)

        def over_chunks(extent, fn, init=0):
            return lax.fori_loop(
                0, extent // LANES,
                lambda c, carry: fn(pl.ds(pl.multiple_of(c * LANES, LANES), LANES), c, carry), init)

        def zero_fill(cols, c, carry):
            isc_ref[:, cols] = jnp.zeros((tq, LANES), F32)
            return carry

        over_chunks(W, zero_fill)

        def indexer_head(h, carry):
            for rows, extent in subs:
                d = lax.dot_general(iqh_ref[h, rows, :], mk_ref[0:extent, 0:IDX_DH], NT_DIMS,
                                    preferred_element_type=F32)
                w = iw_ref[h, rows, :]
                for c in range(extent // LANES):
                    isc_ref[rows, c * LANES:(c + 1) * LANES] = (
                        chunk(rows, c) + jnp.maximum(d[:, c * LANES:(c + 1) * LANES], 0.0) * w)
            return carry

        lax.fori_loop(0, IDX_HEADS, indexer_head, 0)
        for r, (rows, extent) in enumerate(subs):
            row_chunk = (step * tq + r * half
                         + lax.broadcasted_iota(jnp.int32, (half, LANES), 0)) // CHUNK

            def mask_inadmissible(cols, c, carry, rows=rows, row_chunk=row_chunk):
                admissible = (c * LANES + lane) // CHUNK <= row_chunk
                isc_ref[rows, cols] = jnp.where(admissible, isc_ref[rows, cols], NEG)
                return carry

            over_chunks(extent, mask_inadmissible)

        def value_step(i, t_us):
            out = []
            for (rows, extent), t_u in zip(subs, t_us):
                cand_u = t_u | lax.shift_left(jnp.int32(1), jnp.int32(31) - i)
                cand_f = _key_to_float(cand_u ^ int_min)
                cnt = (count(rows, extent, lambda ch, c: ch >= cand_f)
                       + jnp.where(cand_f <= NEG, float(S - extent), 0.0))
                out.append(jnp.where(cnt >= topk, cand_u, t_u))
            return tuple(out)

        thrs = [_key_to_float(t_u ^ int_min) for t_u in lax.fori_loop(0, 32, value_step, (zero, zero))]
        needs, n_ties = [], []
        for (rows, extent), thr in zip(subs, thrs):
            def tally(cols, c, accs, rows=rows, thr=thr):
                ch = isc_ref[rows, cols]
                return (accs[0] + jnp.where(ch > thr, 1.0, 0.0), accs[1] + jnp.where(ch == thr, 1.0, 0.0))

            above, ties = over_chunks(extent, tally, (zero.astype(F32), zero.astype(F32)))
            lane_sum = lambda a: jnp.broadcast_to(jnp.sum(a, axis=-1, keepdims=True), a.shape)
            needs.append(topk - lane_sum(above))
            n_ties.append(lane_sum(ties))

        p_ref[...] = jnp.full((tq, LANES), S - 1, jnp.int32)
        surplus = jnp.concatenate(
            [jnp.where(n > need, 1.0, 0.0) for n, need in zip(n_ties, needs)], axis=0)

        @pl.when(jnp.max(surplus) > 0.0)
        def _():
            for (rows, extent), thr, need in zip(subs, thrs, needs):
                def index_step(i, p):
                    cand = p | lax.shift_right_logical(jnp.int32(S // 2), i)
                    ties_below = count(
                        rows, extent, lambda ch, c: (ch == thr) & (lane + c * LANES < cand))
                    return jnp.where(ties_below < need, cand, p)
                p_ref[rows, :] = lax.fori_loop(0, int(math.log2(S)), index_step, zero)

        for (rows, extent), thr in zip(subs, thrs):
            p_last = p_ref[rows, :]

            def to_bias(cols, c, carry, rows=rows, thr=thr, p_last=p_last):
                ch = isc_ref[rows, cols]
                sel = ((ch > thr) | ((ch == thr) & (lane + c * LANES <= p_last))) & (ch > 0.5 * NEG)
                isc_ref[rows, cols] = jnp.where(sel, 0.0, NEG)
                return carry

            over_chunks(extent, to_bias)

        def attention_head(h, carry):
            sl = pl.ds(pl.multiple_of(h * C_DH, C_DH), C_DH)
            for rows, extent in subs:
                s = lax.dot_general(q_ref[rows, sl], k_ref[0:extent, sl], NT_DIMS,
                                    preferred_element_type=F32)
                s = s + isc_ref[rows, 0:extent]
                e = jnp.exp2((s - jnp.max(s, axis=-1, keepdims=True)) * (scale * LOG2_E))
                o = jnp.dot(e.astype(BF16), v_ref[0:extent, sl], preferred_element_type=F32)
                o_ref[rows, sl] = (o * (1.0 / jnp.sum(e, axis=-1, keepdims=True))).astype(o_ref.dtype)
            return carry

        lax.fori_loop(0, C_HEADS, attention_head, 0)

    for ci, W in enumerate(widths):
        pl.when(step == ci)(functools.partial(body, W))


def _dsa(pb, B, S):
    tq = min(512, S)
    nq = S // tq
    topk = min(TOPK_MAX, S // 4)
    widths = tuple(tq * (i + 1) for i in range(nq))
    qspec = lambda t: pl.BlockSpec((tq, PROJ_TILE), lambda b, i: (b * nq + i, t))
    kspec = lambda t: pl.BlockSpec((S, PROJ_TILE), lambda b, i: (b, t))
    return pl.pallas_call(
        functools.partial(_dsa_kernel, topk=topk, widths=widths),
        grid=(B, nq),
        in_specs=[qspec(T_CQ), qspec(T_IQ), qspec(T_MISC), kspec(T_CK), kspec(T_CV), kspec(T_MISC)],
        out_specs=pl.BlockSpec((tq, C_WIDTH), lambda b, i: (b * nq + i, 0)),
        out_shape=jax.ShapeDtypeStruct((B * S, C_WIDTH), BF16),
        scratch_shapes=[pltpu.VMEM((tq, S), F32), pltpu.VMEM((tq, LANES), jnp.int32),
                        pltpu.VMEM((IDX_HEADS, tq, IDX_DH), BF16),
                        pltpu.VMEM((IDX_HEADS, tq, LANES), F32)],
        compiler_params=_cparams("parallel", "arbitrary"),
        name="dsa",
    )(pb, pb, pb, pb, pb, pb)


def _outproj_kernel(x_ref, mod_ref, oa_ref, ob_ref, oc_ref, w_ref, o_ref):
    y = jnp.dot(oa_ref[...], w_ref[0:A_WIDTH, :], preferred_element_type=F32)
    y = y + jnp.dot(ob_ref[...], w_ref[A_WIDTH:A_WIDTH + B_WIDTH, :], preferred_element_type=F32)
    y = y + jnp.dot(oc_ref[...], w_ref[A_WIDTH + B_WIDTH:, :], preferred_element_type=F32)
    o_ref[...] = x_ref[...] + mod_ref[0, 2:3, :] * y


def _outproj(x2, mod_l, oa, ob, oc, w, S):
    M, D = x2.shape
    tm = min(512, S)
    nrow = S // tm
    row = lambda width: pl.BlockSpec((tm, width), lambda i: (i, 0))
    return pl.pallas_call(
        _outproj_kernel,
        grid=(M // tm,),
        in_specs=[
            row(D),
            pl.BlockSpec((1, 6, D), lambda i: (i // nrow, 0, 0)),
            row(A_WIDTH), row(B_WIDTH), row(C_WIDTH),
            pl.BlockSpec((D, D), lambda i: (0, 0)),
        ],
        out_specs=row(D),
        out_shape=jax.ShapeDtypeStruct((M, D), F32),
        compiler_params=_cparams("parallel"),
        name="outproj",
    )(x2, mod_l, oa, ob, oc, w)


def _ffn_kernel(x_ref, mod_ref, w1_ref, w2_ref, *rest, convert_next):
    if convert_next:
        n = (len(rest) - 3) // 2
        src, o_ref, dst, (h_ref, acc_ref) = rest[:n], rest[n], rest[n + 1:2 * n + 1], rest[2 * n + 1:]
        for s_ref, d_ref in zip(src, dst):
            d_ref[...] = s_ref[...].astype(BF16)
    else:
        o_ref, h_ref, acc_ref = rest
    j = pl.program_id(1)

    @pl.when(j == 0)
    def _():
        y = _rms(x_ref[...])
        h_ref[...] = (y * (1.0 + mod_ref[0, 4:5, :]) + mod_ref[0, 3:4, :]).astype(BF16)
        acc_ref[...] = jnp.zeros_like(acc_ref)

    hid = jnp.square(jnp.maximum(jnp.dot(h_ref[...], w1_ref[...], preferred_element_type=F32), 0.0))
    acc_ref[...] += jnp.dot(hid.astype(BF16), w2_ref[...], preferred_element_type=F32)

    @pl.when(j == pl.num_programs(1) - 1)
    def _():
        o_ref[...] = x_ref[...] + mod_ref[0, 5:6, :] * acc_ref[...]


def _ffn(x2, mod_l, w1, w2, S, next_weights=None):
    M, D = x2.shape
    F = w1.shape[-1]
    tm = min(512, S)
    tf = 1024
    nrow = S // tm
    nsteps = (M // tm) * (F // tf)
    in_specs = [
        pl.BlockSpec((tm, D), lambda i, j: (i, 0)),
        pl.BlockSpec((1, 6, D), lambda i, j: (i // nrow, 0, 0)),
        pl.BlockSpec((D, tf), lambda i, j: (0, j)),
        pl.BlockSpec((tf, D), lambda i, j: (j, 0)),
    ]
    out_specs = [pl.BlockSpec((tm, D), lambda i, j: (i, 0))]
    out_shape = [jax.ShapeDtypeStruct((M, D), F32)]
    operands = [x2, mod_l, w1, w2]
    if next_weights is not None:
        stacked, layer = next_weights
        for w in stacked:
            rows, cols = w.shape[1:]
            slab = rows // nsteps
            assert slab * nsteps == rows and slab % 16 == 0
            in_specs.append(pl.BlockSpec(
                (None, slab, cols), lambda i, j: (layer, i * (F // tf) + j, 0)))
            out_specs.append(pl.BlockSpec((slab, cols), lambda i, j: (i * (F // tf) + j, 0)))
            out_shape.append(jax.ShapeDtypeStruct((rows, cols), BF16))
            operands.append(w)
    outs = pl.pallas_call(
        functools.partial(_ffn_kernel, convert_next=next_weights is not None),
        grid=(M // tm, F // tf),
        in_specs=in_specs,
        out_specs=out_specs,
        out_shape=out_shape,
        scratch_shapes=[pltpu.VMEM((tm, D), BF16), pltpu.VMEM((tm, D), F32)],
        compiler_params=_cparams("arbitrary", "arbitrary"),
        name="ffn",
    )(*operands)
    return outs[0], tuple(outs[1:])


def _rope_tables(S):
    pos = jnp.arange(S, dtype=F32)[:, None]

    def angles(dim):
        inv = 1.0 / (ROPE_THETA ** (jnp.arange(0, dim, 2, dtype=F32) / dim))
        ang = pos * inv[None, :]
        return jnp.concatenate([ang, ang], axis=-1)

    a128 = angles(A_DQK)
    half = jnp.arange(LANES) < LANES // 2
    cos128, sin128 = jnp.cos(a128), jnp.sin(a128)
    a64 = jnp.tile(angles(IDX_DH), (1, 2))
    cos64, sin64 = jnp.cos(a64), jnp.sin(a64)
    low = (jnp.arange(LANES) % IDX_DH) < IDX_DH // 2
    return jnp.stack([
        cos128, jnp.where(half, -sin128, sin128),
        cos64, jnp.where(low, -sin64, 0.0), jnp.where(low, 0.0, sin64),
    ])


def _gain_rows(a_qnorm_l, a_knorm_l, b_vnorm_l, c_qnorm_l, c_knorm_l):
    rep = PROJ_TILE // LANES
    one = jnp.ones((PROJ_TILE,), F32)
    rows = [one] * N_PROJ_TILES
    rows[T_AQ] = rows[T_AQ + 1] = jnp.tile(a_qnorm_l, rep)
    rows[T_AK] = rows[T_AK + 1] = jnp.tile(a_knorm_l, rep)
    rows[T_BV] = b_vnorm_l
    rows[T_CQ] = jnp.tile(c_qnorm_l, rep)
    rows[T_CK] = jnp.tile(c_knorm_l, rep)
    return jnp.stack(rows)[:, None, :]


def kernel(x, c, w_mod, b_mod, w_in, w_out, a_qnorm, a_knorm, a_lambda, a_subln, b_vnorm, b_ws,
           b_bias, c_qnorm, c_knorm, w_ff1, w_ff2):
    B, S, D = x.shape
    L = w_mod.shape[0]
    assert D == D_MODEL and S % Q_BLOCK == 0 and w_in.shape[-1] == N_IN

    rope = _rope_tables(S)
    w_in_b = lax.dynamic_update_slice(
        jnp.zeros((L, D, N_PROJ), BF16), w_in.astype(BF16), (0, 0, 0))
    weights_b = (w_out[0].astype(BF16), w_ff1[0].astype(BF16), w_ff2[0].astype(BF16))

    mod = _modulation(c, w_mod, b_mod).reshape(L, B, 6, D)
    x2 = x.reshape(B * S, D)
    for l in range(L):
        w_out_b, w_ff1_b, w_ff2_b = weights_b
        lambda_init = 0.8 - 0.6 * math.exp(-0.3 * l)
        gains = _gain_rows(a_qnorm[l], a_knorm[l], b_vnorm[l], c_qnorm[l], c_knorm[l])
        pb = _inproj(x2, mod[l], w_in_b, l, gains, rope, S)
        oa = _diffattn(pb, a_lambda[l], a_subln[l], lambda_init, B, S)
        ob = _gmlp(pb, b_ws[l], b_bias[l], S)
        oc = _dsa(pb, B, S)
        x2 = _outproj(x2, mod[l], oa, ob, oc, w_out_b, S)
        next_weights = ((w_out, w_ff1, w_ff2), l + 1) if l + 1 < L else None
        x2, weights_b = _ffn(x2, mod[l], w_ff1_b, w_ff2_b, S, next_weights)
    return x2.reshape(B, S, D)
```

```python
import functools
import math

import jax
import jax.numpy as jnp
import numpy as np
from jax import lax
from jax.experimental import pallas as pl
from jax.experimental.pallas import tpu as pltpu

D_MODEL = 2048
CHUNK = 64
ROPE_THETA = 10000.0
EPS = 1e-6
NEG = -1e30
D_FF = 4 * D_MODEL
A_HEADS = 4
A_DQK = D_MODEL // 16
A_DV = 2 * A_DQK
A_WIDTH = A_HEADS * A_DV
B_GROUPS = 4
B_CH = D_MODEL // 16
B_WIDTH = B_GROUPS * B_CH
B_BLOCK = 128
C_HEADS = 4
C_DH = D_MODEL // 16
C_WIDTH = C_HEADS * C_DH
IDX_HEADS = 8
IDX_DH = D_MODEL // 32
TOPK_MAX = 256
Q_BLOCK = 128

LANES = 128
VMEM_LIMIT_BYTES = 56 * 1024 * 1024

PROJ_TILE = 512
T_AQ, T_AK, T_AV, T_BU, T_BV, T_CQ, T_CK, T_CV, T_IQ, T_MISC = 0, 2, 4, 6, 7, 8, 9, 10, 11, 12
N_PROJ_TILES = 13
N_IN = 3 * A_WIDTH + 2 * B_WIDTH + 3 * C_WIDTH + IDX_HEADS * IDX_DH + IDX_DH + IDX_HEADS
N_PROJ = N_PROJ_TILES * PROJ_TILE
IW_LANE0 = IDX_DH
IW_SCALE = IDX_HEADS ** -0.5 * IDX_DH ** -0.5

BF16 = jnp.bfloat16
F32 = jnp.float32
NT_DIMS = (((1,), (1,)), ((), ()))
LOG2_E = math.log2(math.e)


def _cparams(*sem):
    return pltpu.CompilerParams(dimension_semantics=sem, vmem_limit_bytes=VMEM_LIMIT_BYTES)


def _rms(v, gain=None):
    y = v * lax.rsqrt(jnp.mean(v * v, axis=-1, keepdims=True) + EPS)
    return y if gain is None else y * gain


def _mod_kernel(c_ref, w_ref, b_ref, o_ref):
    @pl.when(pl.program_id(1) == 0)
    def _():
        o_ref[...] = jnp.broadcast_to(b_ref[...], o_ref.shape)

    ca = jax.nn.silu(c_ref[...]).astype(BF16)
    o_ref[...] += jnp.dot(ca, w_ref[...].astype(BF16), preferred_element_type=F32)


def _modulation(c, w_mod, b_mod):
    L, D, E = w_mod.shape
    B = c.shape[0]
    td = 256
    return pl.pallas_call(
        _mod_kernel,
        grid=(L, D // td),
        in_specs=[
            pl.BlockSpec((B, td), lambda l, k: (0, k)),
            pl.BlockSpec((None, td, E), lambda l, k: (l, k, 0)),
            pl.BlockSpec((None, 1, E), lambda l, k: (l, 0, 0)),
        ],
        out_specs=pl.BlockSpec((None, B, E), lambda l, k: (l, 0, 0)),
        out_shape=jax.ShapeDtypeStruct((L, B, E), F32),
        compiler_params=_cparams("parallel", "arbitrary"),
        name="modulation",
    )(c, w_mod, b_mod.reshape(L, 1, E))


def _inproj_kernel(x_ref, mod_ref, w_ref, gain_ref, rope_ref, o_ref, h_ref):
    y = _rms(x_ref[...])
    h_ref[...] = (y * (1.0 + mod_ref[0, 1:2, :]) + mod_ref[0, 0:1, :]).astype(BF16)
    groups = PROJ_TILE // LANES

    def rope128(v):
        return v * rope_ref[0] + pltpu.roll(v, LANES // 2, 1) * rope_ref[1]

    def rope64(v):
        return (v * rope_ref[2] + pltpu.roll(v, 96, 1) * rope_ref[3]
                + pltpu.roll(v, 32, 1) * rope_ref[4])

    def misc(v, g):
        if g > 0:
            return v
        lane = lax.broadcasted_iota(jnp.int32, v.shape, 1)
        return jnp.where(lane < IDX_DH, rope64(v), v * IW_SCALE)

    def gain(t, g):
        return gain_ref[t, :, g * LANES:(g + 1) * LANES]

    norm_rope = lambda v, t, g: rope128(_rms(v, gain(t, g)))
    epilogue = {t: (lambda v, t, g: v) for t in range(N_PROJ_TILES)}
    epilogue.update({t: norm_rope for t in (T_AQ, T_AQ + 1, T_AK, T_AK + 1, T_CQ, T_CK)})
    epilogue[T_BU] = lambda v, t, g: jax.nn.gelu(v)
    epilogue[T_BV] = lambda v, t, g: _rms(jax.nn.gelu(v), gain(t, g))
    epilogue[T_IQ] = lambda v, t, g: rope64(v)
    epilogue[T_MISC] = lambda v, t, g: misc(v, g)

    for t in range(N_PROJ_TILES):
        acc = jnp.dot(h_ref[...], w_ref[:, t * PROJ_TILE:(t + 1) * PROJ_TILE],
                      preferred_element_type=F32)
        for g in range(groups):
            lo = t * PROJ_TILE + g * LANES
            o_ref[:, lo:lo + LANES] = epilogue[t](
                acc[:, g * LANES:(g + 1) * LANES], t, g).astype(o_ref.dtype)


def _inproj(x2, mod_l, w_all, layer, gains, rope, S):
    M, D = x2.shape
    tm = min(256, S)
    nrow = S // tm
    return pl.pallas_call(
        _inproj_kernel,
        grid=(M // tm,),
        in_specs=[
            pl.BlockSpec((tm, D), lambda i: (i, 0)),
            pl.BlockSpec((1, 6, D), lambda i: (i // nrow, 0, 0)),
            pl.BlockSpec((None, D, N_PROJ), lambda i: (layer, 0, 0), pipeline_mode=pl.Buffered(1)),
            pl.BlockSpec((N_PROJ_TILES, 1, PROJ_TILE), lambda i: (0, 0, 0)),
            pl.BlockSpec((5, tm, LANES), lambda i: (0, i % nrow, 0)),
        ],
        out_specs=pl.BlockSpec((tm, N_PROJ), lambda i: (i, 0)),
        out_shape=jax.ShapeDtypeStruct((M, N_PROJ), BF16),
        scratch_shapes=[pltpu.VMEM((tm, D), BF16)],
        compiler_params=_cparams("parallel"),
        name="inproj",
    )(x2, mod_l, w_all, gains, rope)


def _chunk_mask(row0, rows, cols):
    r = row0 + lax.broadcasted_iota(jnp.int32, (rows, cols), 0)
    c = lax.broadcasted_iota(jnp.int32, (rows, cols), 1)
    return (c // CHUNK) <= (r // CHUNK)


def _diffattn_kernel(lam_ref, subln_ref, q_ref, k_ref, v_ref, o_ref, *, lambda_init, widths):
    tq = q_ref.shape[0]
    half = tq // 2
    scale = A_DQK ** -0.5
    step = pl.program_id(0)
    lp = lam_ref[...]
    lam = (jnp.exp(jnp.sum(lp[0:1] * lp[1:2], axis=-1, keepdims=True))
           - jnp.exp(jnp.sum(lp[2:3] * lp[3:4], axis=-1, keepdims=True)) + lambda_init)

    diag_mask = _chunk_mask(0, half, half)

    def body(W):
        extent = lambda r: W - (1 - r) * half

        def scores(m, r):
            s = lax.dot_general(q_ref[r * half:(r + 1) * half, m * A_DQK:(m + 1) * A_DQK],
                                k_ref[0:extent(r), m * A_DQK:(m + 1) * A_DQK],
                                NT_DIMS, preferred_element_type=F32)
            diag = jnp.where(diag_mask, s[:, extent(r) - half:], NEG)
            return diag if extent(r) == half else jnp.concatenate([s[:, :extent(r) - half], diag], axis=1)

        def numerator(s):
            e = jnp.exp2((s - jnp.max(s, axis=-1, keepdims=True)) * (scale * LOG2_E))
            return e.astype(BF16), jnp.sum(e, axis=-1, keepdims=True)

        ss = [[scores(m, r) for m in range(2)] for r in range(2)]
        outs = []
        for r in range(2):
            (e1, l1), (e2, l2) = numerator(ss[r][0]), numerator(ss[r][1])
            o1 = jnp.dot(e1, v_ref[0:extent(r), :], preferred_element_type=F32)
            o2 = jnp.dot(e2, v_ref[0:extent(r), :], preferred_element_type=F32)
            outs.append(o1 * (1.0 / l1) - o2 * (lam / l2))
        o = jnp.concatenate(outs, axis=0)
        o_ref[...] = (_rms(o, subln_ref[...]) * (1.0 - lambda_init)).astype(o_ref.dtype)

    for ci, W in enumerate(widths):
        pl.when(step == ci)(functools.partial(body, W))


def _diffattn(pb, a_lambda_l, a_subln_l, lambda_init, B, S):
    tq = min(512, S)
    nq = S // tq
    return pl.pallas_call(
        functools.partial(_diffattn_kernel, lambda_init=lambda_init,
                          widths=tuple(tq * (i + 1) for i in range(nq))),
        grid=(nq, B, A_HEADS),
        in_specs=[
            pl.BlockSpec((4, A_DQK), lambda i, b, h: (0, 0)),
            pl.BlockSpec((1, A_DV), lambda i, b, h: (0, 0)),
            pl.BlockSpec((tq, A_DV), lambda i, b, h: (b * nq + i, h)),
            pl.BlockSpec((S, A_DV), lambda i, b, h: (b, A_HEADS + h)),
            pl.BlockSpec((S, A_DV), lambda i, b, h: (b, 2 * A_HEADS + h)),
        ],
        out_specs=pl.BlockSpec((tq, A_DV), lambda i, b, h: (b * nq + i, h)),
        out_shape=jax.ShapeDtypeStruct((B * S, A_WIDTH), BF16),
        compiler_params=_cparams("parallel", "parallel", "arbitrary"),
        name="diffattn",
    )(a_lambda_l, a_subln_l.reshape(1, A_DV), pb, pb, pb)


def _gmlp_kernel(u_ref, v_ref, ws_ref, bias_ref, o_ref):
    mask = _chunk_mask(0, B_BLOCK, B_BLOCK)
    for g in range(B_GROUPS):
        w = jnp.where(mask, ws_ref[g], 0.0).astype(BF16)
        cols = slice(g * B_CH, (g + 1) * B_CH)
        for n in range(u_ref.shape[0] // B_BLOCK):
            rows = slice(n * B_BLOCK, (n + 1) * B_BLOCK)
            z = jnp.dot(w, v_ref[rows, cols], preferred_element_type=F32) + bias_ref[:, g:g + 1]
            o_ref[rows, cols] = (u_ref[rows, cols].astype(F32) * z).astype(o_ref.dtype)


def _gmlp(pb, ws_l, bias_l, S):
    M = pb.shape[0]
    tm = min(512, S)
    return pl.pallas_call(
        _gmlp_kernel,
        grid=(M // tm,),
        in_specs=[
            pl.BlockSpec((tm, B_WIDTH), lambda i: (i, T_BU)),
            pl.BlockSpec((tm, B_WIDTH), lambda i: (i, T_BV)),
            pl.BlockSpec((B_GROUPS, B_BLOCK, B_BLOCK), lambda i: (0, 0, 0)),
            pl.BlockSpec((B_BLOCK, B_GROUPS), lambda i: (0, 0)),
        ],
        out_specs=pl.BlockSpec((tm, B_WIDTH), lambda i: (i, 0)),
        out_shape=jax.ShapeDtypeStruct((M, B_WIDTH), BF16),
        compiler_params=_cparams("parallel"),
        name="gmlp",
    )(pb, pb, ws_l, bias_l.T)


def _key_to_float(key):
    return lax.bitcast_convert_type(jnp.where(key < 0, key ^ jnp.int32(0x7FFFFFFF), key), F32)


def _upper_half(x):
    return lax.bitcast_convert_type(
        lax.bitcast_convert_type(x, jnp.int32) & jnp.int32(-65536), F32)


def _dsa_kernel(q_ref, iq_ref, mq_ref, k_ref, v_ref, mk_ref, o_ref, isc_ref, p_ref, iqh_ref, iw_ref,
                hi_ref, *, topk, widths):
    tq = q_ref.shape[0]
    half = tq // 2
    S = k_ref.shape[0]
    scale = C_DH ** -0.5
    int_min = jnp.int32(-2 ** 31)
    step = pl.program_id(1)
    lane = lax.broadcasted_iota(jnp.int32, (half, LANES), 1)
    zero = jnp.zeros((half, LANES), jnp.int32)

    for h in range(IDX_HEADS):
        iqh_ref[h] = iq_ref[:, h * IDX_DH:(h + 1) * IDX_DH]
        iw_ref[h] = jnp.broadcast_to(
            mq_ref[:, IW_LANE0 + h:IW_LANE0 + h + 1].astype(F32), (tq, LANES))

    def body(W):
        subs = [(slice(0, half), W - half), (slice(half, tq), W)]

        def chunk(rows, c):
            return isc_ref[rows, c * LANES:(c + 1) * LANES]

        def count(rows, extent, pred):
            acc = jnp.zeros((half, LANES), F32)
            for c in range(extent // LANES):
                acc = acc + jnp.where(pred(chunk(rows, c), c), 1.0, 0.0)
            return jnp.broadcast_to(jnp.sum(acc, axis=-1, keepdims=True), acc.shape)

        isc_ref[:, 0:W] = jnp.zeros((tq, W), F32)

        def indexer_head(h, carry):
            for rows, extent in subs:
                d = lax.dot_general(iqh_ref[h, rows, :], mk_ref[0:extent, 0:IDX_DH], NT_DIMS,
                                    preferred_element_type=F32)
                w = iw_ref[h, rows, :]
                for c in range(extent // LANES):
                    isc_ref[rows, c * LANES:(c + 1) * LANES] = (
                        chunk(rows, c) + jnp.maximum(d[:, c * LANES:(c + 1) * LANES], 0.0) * w)
            return carry

        lax.fori_loop(0, IDX_HEADS, indexer_head, 0)
        for r, (rows, extent) in enumerate(subs):
            masked = jnp.where(
                _chunk_mask(step * tq + r * half, half, extent), isc_ref[rows, 0:extent], NEG)
            isc_ref[rows, 0:extent] = masked
            hi_ref[rows, 0:extent] = _upper_half(masked).astype(BF16)

        def value_step(i, t_us):
            out = []
            for (rows, extent), t_u in zip(subs, t_us):
                cand_u = t_u | lax.shift_left(jnp.int32(1), jnp.int32(31) - i)
                cand_f = _key_to_float(cand_u ^ int_min)
                cnt = (count(rows, extent, lambda ch, c: ch >= cand_f)
                       + jnp.where(cand_f <= NEG, float(S - extent), 0.0))
                out.append(jnp.where(cnt >= topk, cand_u, t_u))
            return tuple(out)

        def coarse_step(i, t_us):
            out = []
            for (rows, extent), t_u in zip(subs, t_us):
                cand_u = t_u | lax.shift_left(jnp.int32(1), jnp.int32(31) - i)
                cand_f = _key_to_float(cand_u ^ int_min)
                cand_b = _upper_half(cand_f).astype(BF16)
                acc = jnp.zeros((half, LANES), BF16)
                for c in range(extent // LANES):
                    acc = acc + jnp.where(hi_ref[rows, c * LANES:(c + 1) * LANES] >= cand_b,
                                          jnp.ones((), BF16), jnp.zeros((), BF16))
                acc = acc.astype(F32)
                cnt = (jnp.broadcast_to(jnp.sum(acc, axis=-1, keepdims=True), acc.shape)
                       + jnp.where(cand_f <= NEG, float(S - extent), 0.0))
                out.append(jnp.where(cnt >= topk, cand_u, t_u))
            return tuple(out)

        t_us = lax.fori_loop(0, 16, coarse_step, (zero, zero))
        thrs = [_key_to_float(t_u ^ int_min) for t_u in lax.fori_loop(16, 32, value_step, t_us)]
        needs = [topk - count(rows, extent, lambda ch, c: ch > thr)
                 for (rows, extent), thr in zip(subs, thrs)]
        n_ties = [count(rows, extent, lambda ch, c: ch == thr)
                  for (rows, extent), thr in zip(subs, thrs)]

        p_ref[...] = jnp.full((tq, LANES), S - 1, jnp.int32)
        surplus = jnp.concatenate(
            [jnp.where(n > need, 1.0, 0.0) for n, need in zip(n_ties, needs)], axis=0)

        @pl.when(jnp.max(surplus) > 0.0)
        def _():
            for (rows, extent), thr, need in zip(subs, thrs, needs):
                def index_step(i, p):
                    cand = p | lax.shift_right_logical(jnp.int32(S // 2), i)
                    ties_below = count(
                        rows, extent, lambda ch, c: (ch == thr) & (lane + c * LANES < cand))
                    return jnp.where(ties_below < need, cand, p)
                p_ref[rows, :] = lax.fori_loop(0, int(math.log2(S)), index_step, zero)

        for (rows, extent), thr in zip(subs, thrs):
            p_last = p_ref[rows, :]
            for c in range(extent // LANES):
                ch = chunk(rows, c)
                sel = ((ch > thr) | ((ch == thr) & (lane + c * LANES <= p_last))) & (ch > 0.5 * NEG)
                isc_ref[rows, c * LANES:(c + 1) * LANES] = jnp.where(sel, 0.0, NEG)

        def attention_head(h, carry):
            sl = pl.ds(pl.multiple_of(h * C_DH, C_DH), C_DH)
            for rows, extent in subs:
                s = lax.dot_general(q_ref[rows, sl], k_ref[0:extent, sl], NT_DIMS,
                                    preferred_element_type=F32)
                s = s + isc_ref[rows, 0:extent]
                e = jnp.exp2((s - jnp.max(s, axis=-1, keepdims=True)) * (scale * LOG2_E))
                o = jnp.dot(e.astype(BF16), v_ref[0:extent, sl], preferred_element_type=F32)
                o_ref[rows, sl] = (o * (1.0 / jnp.sum(e, axis=-1, keepdims=True))).astype(o_ref.dtype)
            return carry

        lax.fori_loop(0, C_HEADS, attention_head, 0)

    for ci, W in enumerate(widths):
        pl.when(step == ci)(functools.partial(body, W))


def _dsa(pb, B, S):
    tq = min(512, S)
    nq = S // tq
    topk = min(TOPK_MAX, S // 4)
    widths = tuple(tq * (i + 1) for i in range(nq))
    qspec = lambda t: pl.BlockSpec((tq, PROJ_TILE), lambda b, i: (b * nq + i, t))
    kspec = lambda t: pl.BlockSpec((S, PROJ_TILE), lambda b, i: (b, t))
    return pl.pallas_call(
        functools.partial(_dsa_kernel, topk=topk, widths=widths),
        grid=(B, nq),
        in_specs=[qspec(T_CQ), qspec(T_IQ), qspec(T_MISC), kspec(T_CK), kspec(T_CV), kspec(T_MISC)],
        out_specs=pl.BlockSpec((tq, C_WIDTH), lambda b, i: (b * nq + i, 0)),
        out_shape=jax.ShapeDtypeStruct((B * S, C_WIDTH), BF16),
        scratch_shapes=[pltpu.VMEM((tq, S), F32), pltpu.VMEM((tq, LANES), jnp.int32),
                        pltpu.VMEM((IDX_HEADS, tq, IDX_DH), BF16),
                        pltpu.VMEM((IDX_HEADS, tq, LANES), F32),
                        pltpu.VMEM((tq, S), BF16)],
        compiler_params=_cparams("parallel", "arbitrary"),
        name="dsa",
    )(pb, pb, pb, pb, pb, pb)


def _outproj_kernel(x_ref, mod_ref, oa_ref, ob_ref, oc_ref, w_ref, o_ref):
    y = jnp.dot(oa_ref[...], w_ref[0:A_WIDTH, :], preferred_element_type=F32)
    y = y + jnp.dot(ob_ref[...], w_ref[A_WIDTH:A_WIDTH + B_WIDTH, :], preferred_element_type=F32)
    y = y + jnp.dot(oc_ref[...], w_ref[A_WIDTH + B_WIDTH:, :], preferred_element_type=F32)
    o_ref[...] = x_ref[...] + mod_ref[0, 2:3, :] * y


def _outproj(x2, mod_l, oa, ob, oc, w, S):
    M, D = x2.shape
    tm = min(512, S)
    nrow = S // tm
    row = lambda width: pl.BlockSpec((tm, width), lambda i: (i, 0))
    return pl.pallas_call(
        _outproj_kernel,
        grid=(M // tm,),
        in_specs=[
            row(D),
            pl.BlockSpec((1, 6, D), lambda i: (i // nrow, 0, 0)),
            row(A_WIDTH), row(B_WIDTH), row(C_WIDTH),
            pl.BlockSpec((D, D), lambda i: (0, 0)),
        ],
        out_specs=row(D),
        out_shape=jax.ShapeDtypeStruct((M, D), F32),
        compiler_params=_cparams("parallel"),
        name="outproj",
    )(x2, mod_l, oa, ob, oc, w)


def _ffn_kernel(x_ref, mod_ref, w1_ref, w2_ref, *rest, convert_next):
    if convert_next:
        n = (len(rest) - 3) // 2
        src, o_ref, dst, (h_ref, acc_ref) = rest[:n], rest[n], rest[n + 1:2 * n + 1], rest[2 * n + 1:]
        for s_ref, d_ref in zip(src, dst):
            d_ref[...] = s_ref[...].astype(BF16)
    else:
        o_ref, h_ref, acc_ref = rest
    j = pl.program_id(1)

    @pl.when(j == 0)
    def _():
        y = _rms(x_ref[...])
        h_ref[...] = (y * (1.0 + mod_ref[0, 4:5, :]) + mod_ref[0, 3:4, :]).astype(BF16)
        acc_ref[...] = jnp.zeros_like(acc_ref)

    hid = jnp.square(jnp.maximum(jnp.dot(h_ref[...], w1_ref[...], preferred_element_type=F32), 0.0))
    acc_ref[...] += jnp.dot(hid.astype(BF16), w2_ref[...], preferred_element_type=F32)

    @pl.when(j == pl.num_programs(1) - 1)
    def _():
        o_ref[...] = x_ref[...] + mod_ref[0, 5:6, :] * acc_ref[...]


def _ffn(x2, mod_l, w1, w2, S, next_weights=None):
    M, D = x2.shape
    F = w1.shape[-1]
    tm = min(512, S)
    tf = 1024
    nrow = S // tm
    nsteps = (M // tm) * (F // tf)
    in_specs = [
        pl.BlockSpec((tm, D), lambda i, j: (i, 0)),
        pl.BlockSpec((1, 6, D), lambda i, j: (i // nrow, 0, 0)),
        pl.BlockSpec((D, tf), lambda i, j: (0, j)),
        pl.BlockSpec((tf, D), lambda i, j: (j, 0)),
    ]
    out_specs = [pl.BlockSpec((tm, D), lambda i, j: (i, 0))]
    out_shape = [jax.ShapeDtypeStruct((M, D), F32)]
    operands = [x2, mod_l, w1, w2]
    if next_weights is not None:
        stacked, layer = next_weights
        for w in stacked:
            rows, cols = w.shape[1:]
            slab = rows // nsteps
            assert slab * nsteps == rows and slab % 16 == 0
            in_specs.append(pl.BlockSpec(
                (None, slab, cols), lambda i, j: (layer, i * (F // tf) + j, 0)))
            out_specs.append(pl.BlockSpec((slab, cols), lambda i, j: (i * (F // tf) + j, 0)))
            out_shape.append(jax.ShapeDtypeStruct((rows, cols), BF16))
            operands.append(w)
    outs = pl.pallas_call(
        functools.partial(_ffn_kernel, convert_next=next_weights is not None),
        grid=(M // tm, F // tf),
        in_specs=in_specs,
        out_specs=out_specs,
        out_shape=out_shape,
        scratch_shapes=[pltpu.VMEM((tm, D), BF16), pltpu.VMEM((tm, D), F32)],
        compiler_params=_cparams("arbitrary", "arbitrary"),
        name="ffn",
    )(*operands)
    return outs[0], tuple(outs[1:])


def _rope_tables(S):
    pos = jnp.arange(S, dtype=F32)[:, None]

    def angles(dim):
        inv = 1.0 / (ROPE_THETA ** (jnp.arange(0, dim, 2, dtype=F32) / dim))
        ang = pos * inv[None, :]
        return jnp.concatenate([ang, ang], axis=-1)

    a128 = angles(A_DQK)
    half = jnp.arange(LANES) < LANES // 2
    cos128, sin128 = jnp.cos(a128), jnp.sin(a128)
    a64 = jnp.tile(angles(IDX_DH), (1, 2))
    cos64, sin64 = jnp.cos(a64), jnp.sin(a64)
    low = (jnp.arange(LANES) % IDX_DH) < IDX_DH // 2
    return jnp.stack([
        cos128, jnp.where(half, -sin128, sin128),
        cos64, jnp.where(low, -sin64, 0.0), jnp.where(low, 0.0, sin64),
    ])


def _gain_rows(a_qnorm_l, a_knorm_l, b_vnorm_l, c_qnorm_l, c_knorm_l):
    rep = PROJ_TILE // LANES
    one = jnp.ones((PROJ_TILE,), F32)
    rows = [one] * N_PROJ_TILES
    rows[T_AQ] = rows[T_AQ + 1] = jnp.tile(a_qnorm_l, rep)
    rows[T_AK] = rows[T_AK + 1] = jnp.tile(a_knorm_l, rep)
    rows[T_BV] = b_vnorm_l
    rows[T_CQ] = jnp.tile(c_qnorm_l, rep)
    rows[T_CK] = jnp.tile(c_knorm_l, rep)
    return jnp.stack(rows)[:, None, :]


def kernel(x, c, w_mod, b_mod, w_in, w_out, a_qnorm, a_knorm, a_lambda, a_subln, b_vnorm, b_ws,
           b_bias, c_qnorm, c_knorm, w_ff1, w_ff2):
    B, S, D = x.shape
    L = w_mod.shape[0]
    assert D == D_MODEL and S % Q_BLOCK == 0 and w_in.shape[-1] == N_IN

    rope = _rope_tables(S)
    w_in_b = lax.dynamic_update_slice(
        jnp.zeros((L, D, N_PROJ), BF16), w_in.astype(BF16), (0, 0, 0))
    weights_b = (w_out[0].astype(BF16), w_ff1[0].astype(BF16), w_ff2[0].astype(BF16))

    mod = _modulation(c, w_mod, b_mod).reshape(L, B, 6, D)
    x2 = x.reshape(B * S, D)
    for l in range(L):
        w_out_b, w_ff1_b, w_ff2_b = weights_b
        lambda_init = 0.8 - 0.6 * math.exp(-0.3 * l)
        gains = _gain_rows(a_qnorm[l], a_knorm[l], b_vnorm[l], c_qnorm[l], c_knorm[l])
        pb = _inproj(x2, mod[l], w_in_b, l, gains, rope, S)
        oa = _diffattn(pb, a_lambda[l], a_subln[l], lambda_init, B, S)
        ob = _gmlp(pb, b_ws[l], b_bias[l], S)
        oc = _dsa(pb, B, S)
        x2 = _outproj(x2, mod[l], oa, ob, oc, w_out_b, S)
        next_weights = ((w_out, w_ff1, w_ff2), l + 1) if l + 1 < L else None
        x2, weights_b = _ffn(x2, mod[l], w_ff1_b, w_ff2_b, S, next_weights)
    return x2.reshape(B, S, D)
```

```python
import functools
import math

import jax
import jax.numpy as jnp
import numpy as np
from jax import lax
from jax.experimental import pallas as pl
from jax.experimental.pallas import tpu as pltpu

D_MODEL = 2048
CHUNK = 64
ROPE_THETA = 10000.0
EPS = 1e-6
NEG = -1e30
D_FF = 4 * D_MODEL
A_HEADS = 4
A_DQK = D_MODEL // 16
A_DV = 2 * A_DQK
A_WIDTH = A_HEADS * A_DV
B_GROUPS = 4
B_CH = D_MODEL // 16
B_WIDTH = B_GROUPS * B_CH
B_BLOCK = 128
C_HEADS = 4
C_DH = D_MODEL // 16
C_WIDTH = C_HEADS * C_DH
IDX_HEADS = 8
IDX_DH = D_MODEL // 32
TOPK_MAX = 256
Q_BLOCK = 128

LANES = 128
VMEM_LIMIT_BYTES = 56 * 1024 * 1024

PROJ_TILE = 512
T_AQ, T_AK, T_AV, T_BU, T_BV, T_CQ, T_CK, T_CV, T_IQ, T_MISC = 0, 2, 4, 6, 7, 8, 9, 10, 11, 12
N_PROJ_TILES = 13
N_IN = 3 * A_WIDTH + 2 * B_WIDTH + 3 * C_WIDTH + IDX_HEADS * IDX_DH + IDX_DH + IDX_HEADS
N_PROJ = N_PROJ_TILES * PROJ_TILE
IW_LANE0 = IDX_DH
IW_SCALE = IDX_HEADS ** -0.5 * IDX_DH ** -0.5

BF16 = jnp.bfloat16
F32 = jnp.float32
NT_DIMS = (((1,), (1,)), ((), ()))
LOG2_E = math.log2(math.e)


def _cparams(*sem):
    return pltpu.CompilerParams(dimension_semantics=sem, vmem_limit_bytes=VMEM_LIMIT_BYTES)


def _rms(v, gain=None):
    y = v * lax.rsqrt(jnp.mean(v * v, axis=-1, keepdims=True) + EPS)
    return y if gain is None else y * gain


def _mod_kernel(c_ref, w_ref, b_ref, o_ref):
    @pl.when(pl.program_id(1) == 0)
    def _():
        o_ref[...] = jnp.broadcast_to(b_ref[...], o_ref.shape)

    ca = jax.nn.silu(c_ref[...]).astype(BF16)
    o_ref[...] += jnp.dot(ca, w_ref[...].astype(BF16), preferred_element_type=F32)


def _modulation(c, w_mod, b_mod):
    L, D, E = w_mod.shape
    B = c.shape[0]
    td = 256
    return pl.pallas_call(
        _mod_kernel,
        grid=(L, D // td),
        in_specs=[
            pl.BlockSpec((B, td), lambda l, k: (0, k)),
            pl.BlockSpec((None, td, E), lambda l, k: (l, k, 0)),
            pl.BlockSpec((None, 1, E), lambda l, k: (l, 0, 0)),
        ],
        out_specs=pl.BlockSpec((None, B, E), lambda l, k: (l, 0, 0)),
        out_shape=jax.ShapeDtypeStruct((L, B, E), F32),
        compiler_params=_cparams("parallel", "arbitrary"),
        name="modulation",
    )(c, w_mod, b_mod.reshape(L, 1, E))


def _inproj_kernel(x_ref, mod_ref, w_ref, gain_ref, rope_ref, o_ref, h_ref):
    y = _rms(x_ref[...])
    h_ref[...] = (y * (1.0 + mod_ref[0, 1:2, :]) + mod_ref[0, 0:1, :]).astype(BF16)
    groups = PROJ_TILE // LANES

    def rope128(v):
        return v * rope_ref[0] + pltpu.roll(v, LANES // 2, 1) * rope_ref[1]

    def rope64(v):
        return (v * rope_ref[2] + pltpu.roll(v, 96, 1) * rope_ref[3]
                + pltpu.roll(v, 32, 1) * rope_ref[4])

    def misc(v, g):
        if g > 0:
            return v
        lane = lax.broadcasted_iota(jnp.int32, v.shape, 1)
        return jnp.where(lane < IDX_DH, rope64(v), v * IW_SCALE)

    def gain(t, g):
        return gain_ref[t, :, g * LANES:(g + 1) * LANES]

    norm_rope = lambda v, t, g: rope128(_rms(v, gain(t, g)))
    epilogue = {t: (lambda v, t, g: v) for t in range(N_PROJ_TILES)}
    epilogue.update({t: norm_rope for t in (T_AQ, T_AQ + 1, T_AK, T_AK + 1, T_CQ, T_CK)})
    epilogue[T_BU] = lambda v, t, g: jax.nn.gelu(v)
    epilogue[T_BV] = lambda v, t, g: _rms(jax.nn.gelu(v), gain(t, g))
    epilogue[T_IQ] = lambda v, t, g: rope64(v)
    epilogue[T_MISC] = lambda v, t, g: misc(v, g)

    for t in range(N_PROJ_TILES):
        acc = jnp.dot(h_ref[...], w_ref[:, t * PROJ_TILE:(t + 1) * PROJ_TILE],
                      preferred_element_type=F32)
        for g in range(groups):
            lo = t * PROJ_TILE + g * LANES
            o_ref[:, lo:lo + LANES] = epilogue[t](
                acc[:, g * LANES:(g + 1) * LANES], t, g).astype(o_ref.dtype)


def _inproj(x2, mod_l, w_all, layer, gains, rope, S):
    M, D = x2.shape
    tm = min(256, S)
    nrow = S // tm
    return pl.pallas_call(
        _inproj_kernel,
        grid=(M // tm,),
        in_specs=[
            pl.BlockSpec((tm, D), lambda i: (i, 0)),
            pl.BlockSpec((1, 6, D), lambda i: (i // nrow, 0, 0)),
            pl.BlockSpec((None, D, N_PROJ), lambda i: (layer, 0, 0), pipeline_mode=pl.Buffered(1)),
            pl.BlockSpec((N_PROJ_TILES, 1, PROJ_TILE), lambda i: (0, 0, 0)),
            pl.BlockSpec((5, tm, LANES), lambda i: (0, i % nrow, 0)),
        ],
        out_specs=pl.BlockSpec((tm, N_PROJ), lambda i: (i, 0)),
        out_shape=jax.ShapeDtypeStruct((M, N_PROJ), BF16),
        scratch_shapes=[pltpu.VMEM((tm, D), BF16)],
        compiler_params=_cparams("parallel"),
        name="inproj",
    )(x2, mod_l, w_all, gains, rope)


def _chunk_mask(row0, rows, cols):
    r = row0 + lax.broadcasted_iota(jnp.int32, (rows, cols), 0)
    c = lax.broadcasted_iota(jnp.int32, (rows, cols), 1)
    return (c // CHUNK) <= (r // CHUNK)


def _diffattn_kernel(lam_ref, subln_ref, q_ref, k_ref, v_ref, o_ref, *, lambda_init, widths):
    tq = q_ref.shape[0]
    half = tq // 2
    scale = A_DQK ** -0.5
    step = pl.program_id(0)
    lp = lam_ref[...]
    lam = (jnp.exp(jnp.sum(lp[0:1] * lp[1:2], axis=-1, keepdims=True))
           - jnp.exp(jnp.sum(lp[2:3] * lp[3:4], axis=-1, keepdims=True)) + lambda_init)

    diag_mask = _chunk_mask(0, half, half)

    def body(W):
        extent = lambda r: W - (1 - r) * half

        def scores(m, r):
            s = lax.dot_general(q_ref[r * half:(r + 1) * half, m * A_DQK:(m + 1) * A_DQK],
                                k_ref[0:extent(r), m * A_DQK:(m + 1) * A_DQK],
                                NT_DIMS, preferred_element_type=F32)
            diag = jnp.where(diag_mask, s[:, extent(r) - half:], NEG)
            return diag if extent(r) == half else jnp.concatenate([s[:, :extent(r) - half], diag], axis=1)

        def numerator(s):
            e = jnp.exp2((s - jnp.max(s, axis=-1, keepdims=True)) * (scale * LOG2_E))
            return e.astype(BF16), jnp.sum(e, axis=-1, keepdims=True)

        ss = [[scores(m, r) for m in range(2)] for r in range(2)]
        outs = []
        for r in range(2):
            (e1, l1), (e2, l2) = numerator(ss[r][0]), numerator(ss[r][1])
            o1 = jnp.dot(e1, v_ref[0:extent(r), :], preferred_element_type=F32)
            o2 = jnp.dot(e2, v_ref[0:extent(r), :], preferred_element_type=F32)
            outs.append(o1 * (1.0 / l1) - o2 * (lam / l2))
        o = jnp.concatenate(outs, axis=0)
        o_ref[...] = (_rms(o, subln_ref[...]) * (1.0 - lambda_init)).astype(o_ref.dtype)

    for ci, W in enumerate(widths):
        pl.when(step == ci)(functools.partial(body, W))


def _diffattn(pb, a_lambda_l, a_subln_l, lambda_init, B, S):
    tq = min(512, S)
    nq = S // tq
    return pl.pallas_call(
        functools.partial(_diffattn_kernel, lambda_init=lambda_init,
                          widths=tuple(tq * (i + 1) for i in range(nq))),
        grid=(nq, B, A_HEADS),
        in_specs=[
            pl.BlockSpec((4, A_DQK), lambda i, b, h: (0, 0)),
            pl.BlockSpec((1, A_DV), lambda i, b, h: (0, 0)),
            pl.BlockSpec((tq, A_DV), lambda i, b, h: (b * nq + i, h)),
            pl.BlockSpec((S, A_DV), lambda i, b, h: (b, A_HEADS + h)),
            pl.BlockSpec((S, A_DV), lambda i, b, h: (b, 2 * A_HEADS + h)),
        ],
        out_specs=pl.BlockSpec((tq, A_DV), lambda i, b, h: (b * nq + i, h)),
        out_shape=jax.ShapeDtypeStruct((B * S, A_WIDTH), BF16),
        compiler_params=_cparams("parallel", "parallel", "arbitrary"),
        name="diffattn",
    )(a_lambda_l, a_subln_l.reshape(1, A_DV), pb, pb, pb)


def _gmlp_kernel(u_ref, v_ref, ws_ref, bias_ref, o_ref):
    mask = _chunk_mask(0, B_BLOCK, B_BLOCK)
    for g in range(B_GROUPS):
        w = jnp.where(mask, ws_ref[g], 0.0).astype(BF16)
        cols = slice(g * B_CH, (g + 1) * B_CH)
        for n in range(u_ref.shape[0] // B_BLOCK):
            rows = slice(n * B_BLOCK, (n + 1) * B_BLOCK)
            z = jnp.dot(w, v_ref[rows, cols], preferred_element_type=F32) + bias_ref[:, g:g + 1]
            o_ref[rows, cols] = (u_ref[rows, cols].astype(F32) * z).astype(o_ref.dtype)


def _gmlp(pb, ws_l, bias_l, S):
    M = pb.shape[0]
    tm = min(512, S)
    return pl.pallas_call(
        _gmlp_kernel,
        grid=(M // tm,),
        in_specs=[
            pl.BlockSpec((tm, B_WIDTH), lambda i: (i, T_BU)),
            pl.BlockSpec((tm, B_WIDTH), lambda i: (i, T_BV)),
            pl.BlockSpec((B_GROUPS, B_BLOCK, B_BLOCK), lambda i: (0, 0, 0)),
            pl.BlockSpec((B_BLOCK, B_GROUPS), lambda i: (0, 0)),
        ],
        out_specs=pl.BlockSpec((tm, B_WIDTH), lambda i: (i, 0)),
        out_shape=jax.ShapeDtypeStruct((M, B_WIDTH), BF16),
        compiler_params=_cparams("parallel"),
        name="gmlp",
    )(pb, pb, ws_l, bias_l.T)


def _key_to_float(key):
    return lax.bitcast_convert_type(jnp.where(key < 0, key ^ jnp.int32(0x7FFFFFFF), key), F32)


def _dsa_kernel(q_ref, iq_ref, mq_ref, k_ref, v_ref, mk_ref, o_ref, isc_ref, p_ref, iqh_ref, iw_ref,
                *, topk, widths):
    tq = q_ref.shape[0]
    half = tq // 2
    S = k_ref.shape[0]
    scale = C_DH ** -0.5
    int_min = jnp.int32(-2 ** 31)
    step = pl.program_id(1)
    lane = lax.broadcasted_iota(jnp.int32, (half, LANES), 1)
    zero = jnp.zeros((half, LANES), jnp.int32)

    for h in range(IDX_HEADS):
        iqh_ref[h] = iq_ref[:, h * IDX_DH:(h + 1) * IDX_DH]
        iw_ref[h] = jnp.broadcast_to(
            mq_ref[:, IW_LANE0 + h:IW_LANE0 + h + 1].astype(F32), (tq, LANES))

    def body(W):
        subs = [(slice(0, half), W - half), (slice(half, tq), W)]

        def chunk(rows, c):
            return isc_ref[rows, c * LANES:(c + 1) * LANES]

        def count(rows, extent, pred):
            acc = jnp.zeros((half, LANES), F32)
            for c in range(extent // LANES):
                acc = acc + jnp.where(pred(chunk(rows, c), c), 1.0, 0.0)
            return jnp.broadcast_to(jnp.sum(acc, axis=-1, keepdims=True), acc.shape)

        isc_ref[:, 0:W] = jnp.zeros((tq, W), F32)

        def indexer_head(h, carry):
            for rows, extent in subs:
                d = lax.dot_general(iqh_ref[h, rows, :], mk_ref[0:extent, 0:IDX_DH], NT_DIMS,
                                    preferred_element_type=F32)
                w = iw_ref[h, rows, :]
                for c in range(extent // LANES):
                    isc_ref[rows, c * LANES:(c + 1) * LANES] = (
                        chunk(rows, c) + jnp.maximum(d[:, c * LANES:(c + 1) * LANES], 0.0) * w)
            return carry

        lax.fori_loop(0, IDX_HEADS, indexer_head, 0)
        for r, (rows, extent) in enumerate(subs):
            isc_ref[rows, 0:extent] = jnp.where(
                _chunk_mask(step * tq + r * half, half, extent), isc_ref[rows, 0:extent], NEG)

        def value_step(i, t_us):
            out = []
            for (rows, extent), t_u in zip(subs, t_us):
                cand_u = t_u | lax.shift_left(jnp.int32(1), jnp.int32(31) - i)
                cand_f = _key_to_float(cand_u ^ int_min)
                cnt = (count(rows, extent, lambda ch, c: ch >= cand_f)
                       + jnp.where(cand_f <= NEG, float(S - extent), 0.0))
                out.append(jnp.where(cnt >= topk, cand_u, t_u))
            return tuple(out)

        thrs = [_key_to_float(t_u ^ int_min) for t_u in lax.fori_loop(0, 32, value_step, (zero, zero))]
        needs = [topk - count(rows, extent, lambda ch, c: ch > thr)
                 for (rows, extent), thr in zip(subs, thrs)]
        n_ties = [count(rows, extent, lambda ch, c: ch == thr)
                  for (rows, extent), thr in zip(subs, thrs)]

        p_ref[...] = jnp.full((tq, LANES), S - 1, jnp.int32)
        surplus = jnp.concatenate(
            [jnp.where(n > need, 1.0, 0.0) for n, need in zip(n_ties, needs)], axis=0)

        @pl.when(jnp.max(surplus) > 0.0)
        def _():
            for (rows, extent), thr, need in zip(subs, thrs, needs):
                def index_step(i, p):
                    cand = p | lax.shift_right_logical(jnp.int32(S // 2), i)
                    ties_below = count(
                        rows, extent, lambda ch, c: (ch == thr) & (lane + c * LANES < cand))
                    return jnp.where(ties_below < need, cand, p)
                p_ref[rows, :] = lax.fori_loop(0, int(math.log2(S)), index_step, zero)

        for (rows, extent), thr in zip(subs, thrs):
            p_last = p_ref[rows, :]
            for c in range(extent // LANES):
                ch = chunk(rows, c)
                sel = ((ch > thr) | ((ch == thr) & (lane + c * LANES <= p_last))) & (ch > 0.5 * NEG)
                isc_ref[rows, c * LANES:(c + 1) * LANES] = jnp.where(sel, 0.0, NEG)

        def attention_head(h, carry):
            sl = pl.ds(pl.multiple_of(h * C_DH, C_DH), C_DH)
            for rows, extent in subs:
                s = lax.dot_general(q_ref[rows, sl], k_ref[0:extent, sl], NT_DIMS,
                                    preferred_element_type=F32)
                s = s + isc_ref[rows, 0:extent]
                e = jnp.exp2((s - jnp.max(s, axis=-1, keepdims=True)) * (scale * LOG2_E))
                o = jnp.dot(e.astype(BF16), v_ref[0:extent, sl], preferred_element_type=F32)
                o_ref[rows, sl] = (o * (1.0 / jnp.sum(e, axis=-1, keepdims=True))).astype(o_ref.dtype)
            return carry

        lax.fori_loop(0, C_HEADS, attention_head, 0)

    for ci, W in enumerate(widths):
        pl.when(step == ci)(functools.partial(body, W))


def _dsa(pb, B, S):
    tq = min(512, S)
    nq = S // tq
    topk = min(TOPK_MAX, S // 4)
    widths = tuple(tq * (i + 1) for i in range(nq))
    qspec = lambda t: pl.BlockSpec((tq, PROJ_TILE), lambda b, i: (b * nq + i, t))
    kspec = lambda t: pl.BlockSpec((S, PROJ_TILE), lambda b, i: (b, t))
    return pl.pallas_call(
        functools.partial(_dsa_kernel, topk=topk, widths=widths),
        grid=(B, nq),
        in_specs=[qspec(T_CQ), qspec(T_IQ), qspec(T_MISC), kspec(T_CK), kspec(T_CV), kspec(T_MISC)],
        out_specs=pl.BlockSpec((tq, C_WIDTH), lambda b, i: (b * nq + i, 0)),
        out_shape=jax.ShapeDtypeStruct((B * S, C_WIDTH), BF16),
        scratch_shapes=[pltpu.VMEM((tq, S), F32), pltpu.VMEM((tq, LANES), jnp.int32),
                        pltpu.VMEM((IDX_HEADS, tq, IDX_DH), BF16),
                        pltpu.VMEM((IDX_HEADS, tq, LANES), F32)],
        compiler_params=_cparams("parallel", "arbitrary"),
        name="dsa",
    )(pb, pb, pb, pb, pb, pb)


def _outproj_kernel(x_ref, mod_ref, oa_ref, ob_ref, oc_ref, w_ref, o_ref):
    y = jnp.dot(oa_ref[...], w_ref[0:A_WIDTH, :], preferred_element_type=F32)
    y = y + jnp.dot(ob_ref[...], w_ref[A_WIDTH:A_WIDTH + B_WIDTH, :], preferred_element_type=F32)
    y = y + jnp.dot(oc_ref[...], w_ref[A_WIDTH + B_WIDTH:, :], preferred_element_type=F32)
    o_ref[...] = x_ref[...] + mod_ref[0, 2:3, :] * y


def _outproj(x2, mod_l, oa, ob, oc, w, S):
    M, D = x2.shape
    tm = min(512, S)
    nrow = S // tm
    row = lambda width: pl.BlockSpec((tm, width), lambda i: (i, 0))
    return pl.pallas_call(
        _outproj_kernel,
        grid=(M // tm,),
        in_specs=[
            row(D),
            pl.BlockSpec((1, 6, D), lambda i: (i // nrow, 0, 0)),
            row(A_WIDTH), row(B_WIDTH), row(C_WIDTH),
            pl.BlockSpec((D, D), lambda i: (0, 0)),
        ],
        out_specs=row(D),
        out_shape=jax.ShapeDtypeStruct((M, D), F32),
        compiler_params=_cparams("parallel"),
        name="outproj",
    )(x2, mod_l, oa, ob, oc, w)


def _ffn_kernel(x_ref, xn_ref, mod_ref, modn_ref, w1_ref, w2_ref, *rest, convert_next, nj):
    if convert_next:
        n = (len(rest) - 3) // 2
        src, o_ref, dst, (h_ref, acc_ref) = rest[:n], rest[n], rest[n + 1:2 * n + 1], rest[2 * n + 1:]
        for s_ref, d_ref in zip(src, dst):
            d_ref[...] = s_ref[...].astype(BF16)
    else:
        o_ref, h_ref, acc_ref = rest
    i, j = pl.program_id(0), pl.program_id(1)
    slot = i % 2

    def prenorm(xv, m_ref):
        return (_rms(xv) * (1.0 + m_ref[0, 4:5, :]) + m_ref[0, 3:4, :]).astype(BF16)

    @pl.when((i == 0) & (j == 0))
    def _():
        h_ref[0] = prenorm(x_ref[...], mod_ref)

    @pl.when(j == 0)
    def _():
        acc_ref[...] = jnp.zeros_like(acc_ref)

    slab = x_ref.shape[0] // nj
    rows = pl.ds(pl.multiple_of(j * slab, slab), slab)
    h_ref[1 - slot, rows, :] = prenorm(xn_ref[rows, :], modn_ref)

    hid = jnp.square(jnp.maximum(jnp.dot(h_ref[slot], w1_ref[...], preferred_element_type=F32), 0.0))
    acc_ref[...] += jnp.dot(hid.astype(BF16), w2_ref[...], preferred_element_type=F32)

    @pl.when(j == nj - 1)
    def _():
        o_ref[...] = x_ref[...] + mod_ref[0, 5:6, :] * acc_ref[...]


def _ffn(x2, mod_l, w1, w2, S, next_weights=None):
    M, D = x2.shape
    F = w1.shape[-1]
    tm = min(512, S)
    tf = 1024
    nrow = S // tm
    ni, nj = M // tm, F // tf
    nsteps = ni * nj
    nxt = lambda i: jnp.minimum(i + 1, ni - 1)
    in_specs = [
        pl.BlockSpec((tm, D), lambda i, j: (i, 0)),
        pl.BlockSpec((tm, D), lambda i, j: (nxt(i), 0)),
        pl.BlockSpec((1, 6, D), lambda i, j: (i // nrow, 0, 0)),
        pl.BlockSpec((1, 6, D), lambda i, j: (nxt(i) // nrow, 0, 0)),
        pl.BlockSpec((D, tf), lambda i, j: (0, j)),
        pl.BlockSpec((tf, D), lambda i, j: (j, 0)),
    ]
    out_specs = [pl.BlockSpec((tm, D), lambda i, j: (i, 0))]
    out_shape = [jax.ShapeDtypeStruct((M, D), F32)]
    operands = [x2, x2, mod_l, mod_l, w1, w2]
    if next_weights is not None:
        stacked, layer = next_weights
        for w in stacked:
            rows, cols = w.shape[1:]
            slab = rows // nsteps
            assert slab * nsteps == rows and slab % 16 == 0
            in_specs.append(pl.BlockSpec(
                (None, slab, cols), lambda i, j: (layer, i * (F // tf) + j, 0)))
            out_specs.append(pl.BlockSpec((slab, cols), lambda i, j: (i * (F // tf) + j, 0)))
            out_shape.append(jax.ShapeDtypeStruct((rows, cols), BF16))
            operands.append(w)
    outs = pl.pallas_call(
        functools.partial(_ffn_kernel, convert_next=next_weights is not None, nj=nj),
        grid=(ni, nj),
        in_specs=in_specs,
        out_specs=out_specs,
        out_shape=out_shape,
        scratch_shapes=[pltpu.VMEM((2, tm, D), BF16), pltpu.VMEM((tm, D), F32)],
        compiler_params=_cparams("arbitrary", "arbitrary"),
        name="ffn",
    )(*operands)
    return outs[0], tuple(outs[1:])


def _rope_tables(S):
    pos = jnp.arange(S, dtype=F32)[:, None]

    def angles(dim):
        inv = 1.0 / (ROPE_THETA ** (jnp.arange(0, dim, 2, dtype=F32) / dim))
        ang = pos * inv[None, :]
        return jnp.concatenate([ang, ang], axis=-1)

    a128 = angles(A_DQK)
    half = jnp.arange(LANES) < LANES // 2
    cos128, sin128 = jnp.cos(a128), jnp.sin(a128)
    a64 = jnp.tile(angles(IDX_DH), (1, 2))
    cos64, sin64 = jnp.cos(a64), jnp.sin(a64)
    low = (jnp.arange(LANES) % IDX_DH) < IDX_DH // 2
    return jnp.stack([
        cos128, jnp.where(half, -sin128, sin128),
        cos64, jnp.where(low, -sin64, 0.0), jnp.where(low, 0.0, sin64),
    ])


def _gain_rows(a_qnorm_l, a_knorm_l, b_vnorm_l, c_qnorm_l, c_knorm_l):
    rep = PROJ_TILE // LANES
    one = jnp.ones((PROJ_TILE,), F32)
    rows = [one] * N_PROJ_TILES
    rows[T_AQ] = rows[T_AQ + 1] = jnp.tile(a_qnorm_l, rep)
    rows[T_AK] = rows[T_AK + 1] = jnp.tile(a_knorm_l, rep)
    rows[T_BV] = b_vnorm_l
    rows[T_CQ] = jnp.tile(c_qnorm_l, rep)
    rows[T_CK] = jnp.tile(c_knorm_l, rep)
    return jnp.stack(rows)[:, None, :]


def kernel(x, c, w_mod, b_mod, w_in, w_out, a_qnorm, a_knorm, a_lambda, a_subln, b_vnorm, b_ws,
           b_bias, c_qnorm, c_knorm, w_ff1, w_ff2):
    B, S, D = x.shape
    L = w_mod.shape[0]
    assert D == D_MODEL and S % Q_BLOCK == 0 and w_in.shape[-1] == N_IN

    rope = _rope_tables(S)
    w_in_b = lax.dynamic_update_slice(
        jnp.zeros((L, D, N_PROJ), BF16), w_in.astype(BF16), (0, 0, 0))
    weights_b = (w_out[0].astype(BF16), w_ff1[0].astype(BF16), w_ff2[0].astype(BF16))

    mod = _modulation(c, w_mod, b_mod).reshape(L, B, 6, D)
    x2 = x.reshape(B * S, D)
    for l in range(L):
        w_out_b, w_ff1_b, w_ff2_b = weights_b
        lambda_init = 0.8 - 0.6 * math.exp(-0.3 * l)
        gains = _gain_rows(a_qnorm[l], a_knorm[l], b_vnorm[l], c_qnorm[l], c_knorm[l])
        pb = _inproj(x2, mod[l], w_in_b, l, gains, rope, S)
        oa = _diffattn(pb, a_lambda[l], a_subln[l], lambda_init, B, S)
        ob = _gmlp(pb, b_ws[l], b_bias[l], S)
        oc = _dsa(pb, B, S)
        x2 = _outproj(x2, mod[l], oa, ob, oc, w_out_b, S)
        next_weights = ((w_out, w_ff1, w_ff2), l + 1) if l + 1 < L else None
        x2, weights_b = _ffn(x2, mod[l], w_ff1_b, w_ff2_b, S, next_weights)
    return x2.reshape(B, S, D)
```

```python
import functools
import math

import jax
import jax.numpy as jnp
import numpy as np
from jax import lax
from jax.experimental import pallas as pl
from jax.experimental.pallas import tpu as pltpu

D_MODEL = 2048
CHUNK = 64
ROPE_THETA = 10000.0
EPS = 1e-6
NEG = -1e30
D_FF = 4 * D_MODEL
A_HEADS = 4
A_DQK = D_MODEL // 16
A_DV = 2 * A_DQK
A_WIDTH = A_HEADS * A_DV
B_GROUPS = 4
B_CH = D_MODEL // 16
B_WIDTH = B_GROUPS * B_CH
B_BLOCK = 128
C_HEADS = 4
C_DH = D_MODEL // 16
C_WIDTH = C_HEADS * C_DH
IDX_HEADS = 8
IDX_DH = D_MODEL // 32
TOPK_MAX = 256
Q_BLOCK = 128

LANES = 128
VMEM_LIMIT_BYTES = 56 * 1024 * 1024

PROJ_TILE = 512
T_AQ, T_AK, T_AV, T_BU, T_BV, T_CQ, T_CK, T_CV, T_IQ, T_MISC = 0, 2, 4, 6, 7, 8, 9, 10, 11, 12
N_PROJ_TILES = 13
N_IN = 3 * A_WIDTH + 2 * B_WIDTH + 3 * C_WIDTH + IDX_HEADS * IDX_DH + IDX_DH + IDX_HEADS
N_PROJ = N_PROJ_TILES * PROJ_TILE
IW_LANE0 = IDX_DH
IW_SCALE = IDX_HEADS ** -0.5 * IDX_DH ** -0.5

BF16 = jnp.bfloat16
F32 = jnp.float32
NT_DIMS = (((1,), (1,)), ((), ()))
LOG2_E = math.log2(math.e)


def _cparams(*sem):
    return pltpu.CompilerParams(dimension_semantics=sem, vmem_limit_bytes=VMEM_LIMIT_BYTES)


def _rms(v, gain=None):
    y = v * lax.rsqrt(jnp.mean(v * v, axis=-1, keepdims=True) + EPS)
    return y if gain is None else y * gain


def _mod_kernel(c_ref, w_ref, b_ref, o_ref):
    @pl.when(pl.program_id(1) == 0)
    def _():
        o_ref[...] = jnp.broadcast_to(b_ref[...], o_ref.shape)

    ca = jax.nn.silu(c_ref[...]).astype(BF16)
    o_ref[...] += jnp.dot(ca, w_ref[...].astype(BF16), preferred_element_type=F32)


def _modulation(c, w_mod, b_mod):
    L, D, E = w_mod.shape
    B = c.shape[0]
    td = 256
    return pl.pallas_call(
        _mod_kernel,
        grid=(L, D // td),
        in_specs=[
            pl.BlockSpec((B, td), lambda l, k: (0, k)),
            pl.BlockSpec((None, td, E), lambda l, k: (l, k, 0)),
            pl.BlockSpec((None, 1, E), lambda l, k: (l, 0, 0)),
        ],
        out_specs=pl.BlockSpec((None, B, E), lambda l, k: (l, 0, 0)),
        out_shape=jax.ShapeDtypeStruct((L, B, E), F32),
        compiler_params=_cparams("parallel", "arbitrary"),
        name="modulation",
    )(c, w_mod, b_mod.reshape(L, 1, E))


def _inproj_kernel(x_ref, mod_ref, w_ref, gain_ref, rope_ref, o_ref, h_ref):
    y = _rms(x_ref[...])
    h_ref[...] = (y * (1.0 + mod_ref[0, 1:2, :]) + mod_ref[0, 0:1, :]).astype(BF16)
    groups = PROJ_TILE // LANES

    def rope128(v):
        return v * rope_ref[0] + pltpu.roll(v, LANES // 2, 1) * rope_ref[1]

    def rope64(v):
        return (v * rope_ref[2] + pltpu.roll(v, 96, 1) * rope_ref[3]
                + pltpu.roll(v, 32, 1) * rope_ref[4])

    def misc(v, g):
        if g > 0:
            return v
        lane = lax.broadcasted_iota(jnp.int32, v.shape, 1)
        return jnp.where(lane < IDX_DH, rope64(v), v * IW_SCALE)

    def gain(t, g):
        return gain_ref[t, :, g * LANES:(g + 1) * LANES]

    norm_rope = lambda v, t, g: rope128(_rms(v, gain(t, g)))
    epilogue = {t: (lambda v, t, g: v) for t in range(N_PROJ_TILES)}
    epilogue.update({t: norm_rope for t in (T_AQ, T_AQ + 1, T_AK, T_AK + 1, T_CQ, T_CK)})
    epilogue[T_BU] = lambda v, t, g: jax.nn.gelu(v)
    epilogue[T_BV] = lambda v, t, g: _rms(jax.nn.gelu(v), gain(t, g))
    epilogue[T_IQ] = lambda v, t, g: rope64(v)
    epilogue[T_MISC] = lambda v, t, g: misc(v, g)

    for t in range(N_PROJ_TILES):
        acc = jnp.dot(h_ref[...], w_ref[:, t * PROJ_TILE:(t + 1) * PROJ_TILE],
                      preferred_element_type=F32)
        for g in range(groups):
            lo = t * PROJ_TILE + g * LANES
            o_ref[:, lo:lo + LANES] = epilogue[t](
                acc[:, g * LANES:(g + 1) * LANES], t, g).astype(o_ref.dtype)


def _inproj(x2, mod_l, w_all, layer, gains, rope, S):
    M, D = x2.shape
    tm = min(256, S)
    nrow = S // tm
    return pl.pallas_call(
        _inproj_kernel,
        grid=(M // tm,),
        in_specs=[
            pl.BlockSpec((tm, D), lambda i: (i, 0)),
            pl.BlockSpec((1, 6, D), lambda i: (i // nrow, 0, 0)),
            pl.BlockSpec((None, D, N_PROJ), lambda i: (layer, 0, 0), pipeline_mode=pl.Buffered(1)),
            pl.BlockSpec((N_PROJ_TILES, 1, PROJ_TILE), lambda i: (0, 0, 0)),
            pl.BlockSpec((5, tm, LANES), lambda i: (0, i % nrow, 0)),
        ],
        out_specs=pl.BlockSpec((tm, N_PROJ), lambda i: (i, 0)),
        out_shape=jax.ShapeDtypeStruct((M, N_PROJ), BF16),
        scratch_shapes=[pltpu.VMEM((tm, D), BF16)],
        compiler_params=_cparams("parallel"),
        name="inproj",
    )(x2, mod_l, w_all, gains, rope)


def _chunk_mask(row0, rows, cols):
    r = row0 + lax.broadcasted_iota(jnp.int32, (rows, cols), 0)
    c = lax.broadcasted_iota(jnp.int32, (rows, cols), 1)
    return (c // CHUNK) <= (r // CHUNK)


def _diffattn_kernel(lam_ref, subln_ref, q_ref, k_ref, v_ref, o_ref, *, lambda_init, widths):
    tq = q_ref.shape[0]
    half = tq // 2
    scale = A_DQK ** -0.5
    step = pl.program_id(0)
    lp = lam_ref[...]
    lam = (jnp.exp(jnp.sum(lp[0:1] * lp[1:2], axis=-1, keepdims=True))
           - jnp.exp(jnp.sum(lp[2:3] * lp[3:4], axis=-1, keepdims=True)) + lambda_init)

    diag_mask = _chunk_mask(0, half, half)

    def body(W):
        extent = lambda r: W - (1 - r) * half

        def scores(m, r):
            s = lax.dot_general(q_ref[r * half:(r + 1) * half, m * A_DQK:(m + 1) * A_DQK],
                                k_ref[0:extent(r), m * A_DQK:(m + 1) * A_DQK],
                                NT_DIMS, preferred_element_type=F32)
            diag = jnp.where(diag_mask, s[:, extent(r) - half:], NEG)
            return diag if extent(r) == half else jnp.concatenate([s[:, :extent(r) - half], diag], axis=1)

        def numerator(s):
            e = jnp.exp2((s - jnp.max(s, axis=-1, keepdims=True)) * (scale * LOG2_E))
            return e.astype(BF16), jnp.sum(e, axis=-1, keepdims=True)

        ss = [[scores(m, r) for m in range(2)] for r in range(2)]
        outs = []
        for r in range(2):
            (e1, l1), (e2, l2) = numerator(ss[r][0]), numerator(ss[r][1])
            o1 = jnp.dot(e1, v_ref[0:extent(r), :], preferred_element_type=F32)
            o2 = jnp.dot(e2, v_ref[0:extent(r), :], preferred_element_type=F32)
            outs.append(o1 * (1.0 / l1) - o2 * (lam / l2))
        o = jnp.concatenate(outs, axis=0)
        o_ref[...] = (_rms(o, subln_ref[...]) * (1.0 - lambda_init)).astype(o_ref.dtype)

    for ci, W in enumerate(widths):
        pl.when(step == ci)(functools.partial(body, W))


def _diffattn(pb, a_lambda_l, a_subln_l, lambda_init, B, S):
    tq = min(512, S)
    nq = S // tq
    return pl.pallas_call(
        functools.partial(_diffattn_kernel, lambda_init=lambda_init,
                          widths=tuple(tq * (i + 1) for i in range(nq))),
        grid=(nq, B, A_HEADS),
        in_specs=[
            pl.BlockSpec((4, A_DQK), lambda i, b, h: (0, 0)),
            pl.BlockSpec((1, A_DV), lambda i, b, h: (0, 0)),
            pl.BlockSpec((tq, A_DV), lambda i, b, h: (b * nq + i, h)),
            pl.BlockSpec((S, A_DV), lambda i, b, h: (b, A_HEADS + h)),
            pl.BlockSpec((S, A_DV), lambda i, b, h: (b, 2 * A_HEADS + h)),
        ],
        out_specs=pl.BlockSpec((tq, A_DV), lambda i, b, h: (b * nq + i, h)),
        out_shape=jax.ShapeDtypeStruct((B * S, A_WIDTH), BF16),
        compiler_params=_cparams("parallel", "parallel", "arbitrary"),
        name="diffattn",
    )(a_lambda_l, a_subln_l.reshape(1, A_DV), pb, pb, pb)


def _gmlp_rows(u_ref, v_ref, ws_ref, bias_ref, o_ref):
    mask = _chunk_mask(0, B_BLOCK, B_BLOCK)
    for g in range(B_GROUPS):
        w = jnp.where(mask, ws_ref[g], 0.0).astype(BF16)
        cols = slice(g * B_CH, (g + 1) * B_CH)
        for n in range(u_ref.shape[0] // B_BLOCK):
            rows = slice(n * B_BLOCK, (n + 1) * B_BLOCK)
            z = jnp.dot(w, v_ref[rows, cols], preferred_element_type=F32) + bias_ref[:, g:g + 1]
            o_ref[rows, cols] = (u_ref[rows, cols].astype(F32) * z).astype(o_ref.dtype)


def _key_to_float(key):
    return lax.bitcast_convert_type(jnp.where(key < 0, key ^ jnp.int32(0x7FFFFFFF), key), F32)


def _dsa_kernel(q_ref, iq_ref, mq_ref, k_ref, v_ref, mk_ref, o_ref, isc_ref, p_ref, iqh_ref, iw_ref,
                *, topk, widths):
    tq = q_ref.shape[0]
    half = tq // 2
    S = k_ref.shape[0]
    scale = C_DH ** -0.5
    int_min = jnp.int32(-2 ** 31)
    step = pl.program_id(1)
    lane = lax.broadcasted_iota(jnp.int32, (half, LANES), 1)
    zero = jnp.zeros((half, LANES), jnp.int32)

    for h in range(IDX_HEADS):
        iqh_ref[h] = iq_ref[:, h * IDX_DH:(h + 1) * IDX_DH]
        iw_ref[h] = jnp.broadcast_to(
            mq_ref[:, IW_LANE0 + h:IW_LANE0 + h + 1].astype(F32), (tq, LANES))

    def body(W):
        subs = [(slice(0, half), W - half), (slice(half, tq), W)]

        def chunk(rows, c):
            return isc_ref[rows, c * LANES:(c + 1) * LANES]

        def count(rows, extent, pred):
            acc = jnp.zeros((half, LANES), F32)
            for c in range(extent // LANES):
                acc = acc + jnp.where(pred(chunk(rows, c), c), 1.0, 0.0)
            return jnp.broadcast_to(jnp.sum(acc, axis=-1, keepdims=True), acc.shape)

        isc_ref[:, 0:W] = jnp.zeros((tq, W), F32)

        def indexer_head(h, carry):
            for rows, extent in subs:
                d = lax.dot_general(iqh_ref[h, rows, :], mk_ref[0:extent, 0:IDX_DH], NT_DIMS,
                                    preferred_element_type=F32)
                w = iw_ref[h, rows, :]
                for c in range(extent // LANES):
                    isc_ref[rows, c * LANES:(c + 1) * LANES] = (
                        chunk(rows, c) + jnp.maximum(d[:, c * LANES:(c + 1) * LANES], 0.0) * w)
            return carry

        lax.fori_loop(0, IDX_HEADS, indexer_head, 0)
        for r, (rows, extent) in enumerate(subs):
            isc_ref[rows, 0:extent] = jnp.where(
                _chunk_mask(step * tq + r * half, half, extent), isc_ref[rows, 0:extent], NEG)

        def value_step(i, t_us):
            out = []
            for (rows, extent), t_u in zip(subs, t_us):
                cand_u = t_u | lax.shift_left(jnp.int32(1), jnp.int32(31) - i)
                cand_f = _key_to_float(cand_u ^ int_min)
                cnt = (count(rows, extent, lambda ch, c: ch >= cand_f)
                       + jnp.where(cand_f <= NEG, float(S - extent), 0.0))
                out.append(jnp.where(cnt >= topk, cand_u, t_u))
            return tuple(out)

        thrs = [_key_to_float(t_u ^ int_min) for t_u in lax.fori_loop(0, 32, value_step, (zero, zero))]
        needs = [topk - count(rows, extent, lambda ch, c: ch > thr)
                 for (rows, extent), thr in zip(subs, thrs)]
        n_ties = [count(rows, extent, lambda ch, c: ch == thr)
                  for (rows, extent), thr in zip(subs, thrs)]

        p_ref[...] = jnp.full((tq, LANES), S - 1, jnp.int32)
        surplus = jnp.concatenate(
            [jnp.where(n > need, 1.0, 0.0) for n, need in zip(n_ties, needs)], axis=0)

        @pl.when(jnp.max(surplus) > 0.0)
        def _():
            for (rows, extent), thr, need in zip(subs, thrs, needs):
                def index_step(i, p):
                    cand = p | lax.shift_right_logical(jnp.int32(S // 2), i)
                    ties_below = count(
                        rows, extent, lambda ch, c: (ch == thr) & (lane + c * LANES < cand))
                    return jnp.where(ties_below < need, cand, p)
                p_ref[rows, :] = lax.fori_loop(0, int(math.log2(S)), index_step, zero)

        for (rows, extent), thr in zip(subs, thrs):
            p_last = p_ref[rows, :]
            for c in range(extent // LANES):
                ch = chunk(rows, c)
                sel = ((ch > thr) | ((ch == thr) & (lane + c * LANES <= p_last))) & (ch > 0.5 * NEG)
                isc_ref[rows, c * LANES:(c + 1) * LANES] = jnp.where(sel, 0.0, NEG)

        def attention_head(h, carry):
            sl = pl.ds(pl.multiple_of(h * C_DH, C_DH), C_DH)
            for rows, extent in subs:
                s = lax.dot_general(q_ref[rows, sl], k_ref[0:extent, sl], NT_DIMS,
                                    preferred_element_type=F32)
                s = s + isc_ref[rows, 0:extent]
                e = jnp.exp2((s - jnp.max(s, axis=-1, keepdims=True)) * (scale * LOG2_E))
                o = jnp.dot(e.astype(BF16), v_ref[0:extent, sl], preferred_element_type=F32)
                o_ref[rows, sl] = (o * (1.0 / jnp.sum(e, axis=-1, keepdims=True))).astype(o_ref.dtype)
            return carry

        lax.fori_loop(0, C_HEADS, attention_head, 0)

    for ci, W in enumerate(widths):
        pl.when(step == ci)(functools.partial(body, W))


def _dsa(pb, B, S):
    tq = min(512, S)
    nq = S // tq
    topk = min(TOPK_MAX, S // 4)
    widths = tuple(tq * (i + 1) for i in range(nq))
    qspec = lambda t: pl.BlockSpec((tq, PROJ_TILE), lambda b, i: (b * nq + i, t))
    kspec = lambda t: pl.BlockSpec((S, PROJ_TILE), lambda b, i: (b, t))
    return pl.pallas_call(
        functools.partial(_dsa_kernel, topk=topk, widths=widths),
        grid=(B, nq),
        in_specs=[qspec(T_CQ), qspec(T_IQ), qspec(T_MISC), kspec(T_CK), kspec(T_CV), kspec(T_MISC)],
        out_specs=pl.BlockSpec((tq, C_WIDTH), lambda b, i: (b * nq + i, 0)),
        out_shape=jax.ShapeDtypeStruct((B * S, C_WIDTH), BF16),
        scratch_shapes=[pltpu.VMEM((tq, S), F32), pltpu.VMEM((tq, LANES), jnp.int32),
                        pltpu.VMEM((IDX_HEADS, tq, IDX_DH), BF16),
                        pltpu.VMEM((IDX_HEADS, tq, LANES), F32)],
        compiler_params=_cparams("parallel", "arbitrary"),
        name="dsa",
    )(pb, pb, pb, pb, pb, pb)


def _outproj_kernel(x_ref, mod_ref, oa_ref, u_ref, v_ref, ws_ref, bias_ref, oc_ref, w_ref, o_ref,
                    ob_ref):
    _gmlp_rows(u_ref, v_ref, ws_ref, bias_ref, ob_ref)
    y = jnp.dot(oa_ref[...], w_ref[0:A_WIDTH, :], preferred_element_type=F32)
    y = y + jnp.dot(ob_ref[...], w_ref[A_WIDTH:A_WIDTH + B_WIDTH, :], preferred_element_type=F32)
    y = y + jnp.dot(oc_ref[...], w_ref[A_WIDTH + B_WIDTH:, :], preferred_element_type=F32)
    o_ref[...] = x_ref[...] + mod_ref[0, 2:3, :] * y


def _outproj(x2, mod_l, oa, pb, ws_l, bias_l, oc, w, S):
    M, D = x2.shape
    tm = min(512, S)
    nrow = S // tm
    row = lambda width: pl.BlockSpec((tm, width), lambda i: (i, 0))
    return pl.pallas_call(
        _outproj_kernel,
        grid=(M // tm,),
        in_specs=[
            row(D),
            pl.BlockSpec((1, 6, D), lambda i: (i // nrow, 0, 0)),
            row(A_WIDTH),
            pl.BlockSpec((tm, B_WIDTH), lambda i: (i, T_BU)),
            pl.BlockSpec((tm, B_WIDTH), lambda i: (i, T_BV)),
            pl.BlockSpec((B_GROUPS, B_BLOCK, B_BLOCK), lambda i: (0, 0, 0)),
            pl.BlockSpec((B_BLOCK, B_GROUPS), lambda i: (0, 0)),
            row(C_WIDTH),
            pl.BlockSpec((D, D), lambda i: (0, 0)),
        ],
        out_specs=row(D),
        out_shape=jax.ShapeDtypeStruct((M, D), F32),
        scratch_shapes=[pltpu.VMEM((tm, B_WIDTH), BF16)],
        compiler_params=_cparams("parallel"),
        name="outproj",
    )(x2, mod_l, oa, pb, pb, ws_l, bias_l.T, oc, w)


def _ffn_kernel(x_ref, mod_ref, w1_ref, w2_ref, *rest, convert_next):
    if convert_next:
        n = (len(rest) - 3) // 2
        src, o_ref, dst, (h_ref, acc_ref) = rest[:n], rest[n], rest[n + 1:2 * n + 1], rest[2 * n + 1:]
        for s_ref, d_ref in zip(src, dst):
            d_ref[...] = s_ref[...].astype(BF16)
    else:
        o_ref, h_ref, acc_ref = rest
    j = pl.program_id(1)

    @pl.when(j == 0)
    def _():
        y = _rms(x_ref[...])
        h_ref[...] = (y * (1.0 + mod_ref[0, 4:5, :]) + mod_ref[0, 3:4, :]).astype(BF16)
        acc_ref[...] = jnp.zeros_like(acc_ref)

    hid = jnp.square(jnp.maximum(jnp.dot(h_ref[...], w1_ref[...], preferred_element_type=F32), 0.0))
    acc_ref[...] += jnp.dot(hid.astype(BF16), w2_ref[...], preferred_element_type=F32)

    @pl.when(j == pl.num_programs(1) - 1)
    def _():
        o_ref[...] = x_ref[...] + mod_ref[0, 5:6, :] * acc_ref[...]


def _ffn(x2, mod_l, w1, w2, S, next_weights=None):
    M, D = x2.shape
    F = w1.shape[-1]
    tm = min(512, S)
    tf = 1024
    nrow = S // tm
    nsteps = (M // tm) * (F // tf)
    in_specs = [
        pl.BlockSpec((tm, D), lambda i, j: (i, 0)),
        pl.BlockSpec((1, 6, D), lambda i, j: (i // nrow, 0, 0)),
        pl.BlockSpec((D, tf), lambda i, j: (0, j)),
        pl.BlockSpec((tf, D), lambda i, j: (j, 0)),
    ]
    out_specs = [pl.BlockSpec((tm, D), lambda i, j: (i, 0))]
    out_shape = [jax.ShapeDtypeStruct((M, D), F32)]
    operands = [x2, mod_l, w1, w2]
    if next_weights is not None:
        stacked, layer = next_weights
        for w in stacked:
            rows, cols = w.shape[1:]
            slab = rows // nsteps
            assert slab * nsteps == rows and slab % 16 == 0
            in_specs.append(pl.BlockSpec(
                (None, slab, cols), lambda i, j: (layer, i * (F // tf) + j, 0)))
            out_specs.append(pl.BlockSpec((slab, cols), lambda i, j: (i * (F // tf) + j, 0)))
            out_shape.append(jax.ShapeDtypeStruct((rows, cols), BF16))
            operands.append(w)
    outs = pl.pallas_call(
        functools.partial(_ffn_kernel, convert_next=next_weights is not None),
        grid=(M // tm, F // tf),
        in_specs=in_specs,
        out_specs=out_specs,
        out_shape=out_shape,
        scratch_shapes=[pltpu.VMEM((tm, D), BF16), pltpu.VMEM((tm, D), F32)],
        compiler_params=_cparams("arbitrary", "arbitrary"),
        name="ffn",
    )(*operands)
    return outs[0], tuple(outs[1:])


def _rope_tables(S):
    pos = jnp.arange(S, dtype=F32)[:, None]

    def angles(dim):
        inv = 1.0 / (ROPE_THETA ** (jnp.arange(0, dim, 2, dtype=F32) / dim))
        ang = pos * inv[None, :]
        return jnp.concatenate([ang, ang], axis=-1)

    a128 = angles(A_DQK)
    half = jnp.arange(LANES) < LANES // 2
    cos128, sin128 = jnp.cos(a128), jnp.sin(a128)
    a64 = jnp.tile(angles(IDX_DH), (1, 2))
    cos64, sin64 = jnp.cos(a64), jnp.sin(a64)
    low = (jnp.arange(LANES) % IDX_DH) < IDX_DH // 2
    return jnp.stack([
        cos128, jnp.where(half, -sin128, sin128),
        cos64, jnp.where(low, -sin64, 0.0), jnp.where(low, 0.0, sin64),
    ])


def _gain_rows(a_qnorm_l, a_knorm_l, b_vnorm_l, c_qnorm_l, c_knorm_l):
    rep = PROJ_TILE // LANES
    one = jnp.ones((PROJ_TILE,), F32)
    rows = [one] * N_PROJ_TILES
    rows[T_AQ] = rows[T_AQ + 1] = jnp.tile(a_qnorm_l, rep)
    rows[T_AK] = rows[T_AK + 1] = jnp.tile(a_knorm_l, rep)
    rows[T_BV] = b_vnorm_l
    rows[T_CQ] = jnp.tile(c_qnorm_l, rep)
    rows[T_CK] = jnp.tile(c_knorm_l, rep)
    return jnp.stack(rows)[:, None, :]


def kernel(x, c, w_mod, b_mod, w_in, w_out, a_qnorm, a_knorm, a_lambda, a_subln, b_vnorm, b_ws,
           b_bias, c_qnorm, c_knorm, w_ff1, w_ff2):
    B, S, D = x.shape
    L = w_mod.shape[0]
    assert D == D_MODEL and S % Q_BLOCK == 0 and w_in.shape[-1] == N_IN

    rope = _rope_tables(S)
    w_in_b = jnp.concatenate(
        [w_in.astype(BF16), jnp.zeros((L, D, N_PROJ - N_IN), BF16)], axis=-1)
    weights_b = (w_out[0].astype(BF16), w_ff1[0].astype(BF16), w_ff2[0].astype(BF16))

    mod = _modulation(c, w_mod, b_mod).reshape(L, B, 6, D)
    x2 = x.reshape(B * S, D)
    for l in range(L):
        w_out_b, w_ff1_b, w_ff2_b = weights_b
        lambda_init = 0.8 - 0.6 * math.exp(-0.3 * l)
        gains = _gain_rows(a_qnorm[l], a_knorm[l], b_vnorm[l], c_qnorm[l], c_knorm[l])
        pb = _inproj(x2, mod[l], w_in_b, l, gains, rope, S)
        oa = _diffattn(pb, a_lambda[l], a_subln[l], lambda_init, B, S)
        oc = _dsa(pb, B, S)
        x2 = _outproj(x2, mod[l], oa, pb, b_ws[l], b_bias[l], oc, w_out_b, S)
        next_weights = ((w_out, w_ff1, w_ff2), l + 1) if l + 1 < L else None
        x2, weights_b = _ffn(x2, mod[l], w_ff1_b, w_ff2_b, S, next_weights)
    return x2.reshape(B, S, D)
```

```python
import functools
import math

import jax
import jax.numpy as jnp
import numpy as np
from jax import lax
from jax.experimental import pallas as pl
from jax.experimental.pallas import tpu as pltpu

D_MODEL = 2048
CHUNK = 64
ROPE_THETA = 10000.0
EPS = 1e-6
NEG = -1e30
D_FF = 4 * D_MODEL
A_HEADS = 4
A_DQK = D_MODEL // 16
A_DV = 2 * A_DQK
A_WIDTH = A_HEADS * A_DV
B_GROUPS = 4
B_CH = D_MODEL // 16
B_WIDTH = B_GROUPS * B_CH
B_BLOCK = 128
C_HEADS = 4
C_DH = D_MODEL // 16
C_WIDTH = C_HEADS * C_DH
IDX_HEADS = 8
IDX_DH = D_MODEL // 32
TOPK_MAX = 256
Q_BLOCK = 128

LANES = 128
VMEM_LIMIT_BYTES = 56 * 1024 * 1024

PROJ_TILE = 512
T_AQ, T_AK, T_AV, T_BU, T_BV, T_CQ, T_CK, T_CV, T_IQ, T_MISC = 0, 2, 4, 6, 7, 8, 9, 10, 11, 12
N_PROJ_TILES = 13
N_IN = 3 * A_WIDTH + 2 * B_WIDTH + 3 * C_WIDTH + IDX_HEADS * IDX_DH + IDX_DH + IDX_HEADS
N_PROJ = N_PROJ_TILES * PROJ_TILE
IW_LANE0 = IDX_DH
IW_SCALE = IDX_HEADS ** -0.5 * IDX_DH ** -0.5

BF16 = jnp.bfloat16
F32 = jnp.float32
NT_DIMS = (((1,), (1,)), ((), ()))
LOG2_E = math.log2(math.e)


def _cparams(*sem):
    return pltpu.CompilerParams(dimension_semantics=sem, vmem_limit_bytes=VMEM_LIMIT_BYTES)


def _rms(v, gain=None):
    y = v * lax.rsqrt(jnp.mean(v * v, axis=-1, keepdims=True) + EPS)
    return y if gain is None else y * gain


def _mod_kernel(c_ref, w_ref, b_ref, o_ref):
    @pl.when(pl.program_id(1) == 0)
    def _():
        o_ref[...] = jnp.broadcast_to(b_ref[...], o_ref.shape)

    ca = jax.nn.silu(c_ref[...]).astype(BF16)
    o_ref[...] += jnp.dot(ca, w_ref[...].astype(BF16), preferred_element_type=F32)


def _modulation(c, w_mod, b_mod):
    L, D, E = w_mod.shape
    B = c.shape[0]
    td = 256
    return pl.pallas_call(
        _mod_kernel,
        grid=(L, D // td),
        in_specs=[
            pl.BlockSpec((B, td), lambda l, k: (0, k)),
            pl.BlockSpec((None, td, E), lambda l, k: (l, k, 0)),
            pl.BlockSpec((None, 1, E), lambda l, k: (l, 0, 0)),
        ],
        out_specs=pl.BlockSpec((None, B, E), lambda l, k: (l, 0, 0)),
        out_shape=jax.ShapeDtypeStruct((L, B, E), F32),
        compiler_params=_cparams("parallel", "arbitrary"),
        name="modulation",
    )(c, w_mod, b_mod.reshape(L, 1, E))


def _inproj_kernel(x_ref, mod_ref, w_ref, wt_ref, gain_ref, rope_ref, o_ref, h_ref):
    y = _rms(x_ref[...])
    h_ref[...] = (y * (1.0 + mod_ref[0, 1:2, :]) + mod_ref[0, 0:1, :]).astype(BF16)
    groups = PROJ_TILE // LANES

    def rope128(v):
        return v * rope_ref[0] + pltpu.roll(v, LANES // 2, 1) * rope_ref[1]

    def rope64(v):
        return (v * rope_ref[2] + pltpu.roll(v, 96, 1) * rope_ref[3]
                + pltpu.roll(v, 32, 1) * rope_ref[4])

    def misc(v, g):
        if g > 0:
            return v
        lane = lax.broadcasted_iota(jnp.int32, v.shape, 1)
        return jnp.where(lane < IDX_DH, rope64(v), v * IW_SCALE)

    def gain(t, g):
        return gain_ref[t, :, g * LANES:(g + 1) * LANES]

    norm_rope = lambda v, t, g: rope128(_rms(v, gain(t, g)))
    epilogue = {t: (lambda v, t, g: v) for t in range(N_PROJ_TILES)}
    epilogue.update({t: norm_rope for t in (T_AQ, T_AQ + 1, T_AK, T_AK + 1, T_CQ, T_CK)})
    epilogue[T_BU] = lambda v, t, g: jax.nn.gelu(v)
    epilogue[T_BV] = lambda v, t, g: _rms(jax.nn.gelu(v), gain(t, g))
    epilogue[T_IQ] = lambda v, t, g: rope64(v)
    epilogue[T_MISC] = lambda v, t, g: misc(v, g)

    for t in range(N_PROJ_TILES):
        w_tile = wt_ref[...] if t == T_MISC else w_ref[:, t * PROJ_TILE:(t + 1) * PROJ_TILE]
        acc = jnp.dot(h_ref[...], w_tile, preferred_element_type=F32)
        for g in range(groups):
            lo = t * PROJ_TILE + g * LANES
            o_ref[:, lo:lo + LANES] = epilogue[t](
                acc[:, g * LANES:(g + 1) * LANES], t, g).astype(o_ref.dtype)


def _inproj(x2, mod_l, w_all, w_tail_all, layer, gains, rope, S):
    M, D = x2.shape
    tm = min(256, S)
    nrow = S // tm
    return pl.pallas_call(
        _inproj_kernel,
        grid=(M // tm,),
        in_specs=[
            pl.BlockSpec((tm, D), lambda i: (i, 0)),
            pl.BlockSpec((1, 6, D), lambda i: (i // nrow, 0, 0)),
            pl.BlockSpec((None, D, T_MISC * PROJ_TILE), lambda i: (layer, 0, 0),
                         pipeline_mode=pl.Buffered(1)),
            pl.BlockSpec((None, D, PROJ_TILE), lambda i: (layer, 0, 0), pipeline_mode=pl.Buffered(1)),
            pl.BlockSpec((N_PROJ_TILES, 1, PROJ_TILE), lambda i: (0, 0, 0)),
            pl.BlockSpec((5, tm, LANES), lambda i: (0, i % nrow, 0)),
        ],
        out_specs=pl.BlockSpec((tm, N_PROJ), lambda i: (i, 0)),
        out_shape=jax.ShapeDtypeStruct((M, N_PROJ), BF16),
        scratch_shapes=[pltpu.VMEM((tm, D), BF16)],
        compiler_params=_cparams("parallel"),
        name="inproj",
    )(x2, mod_l, w_all, w_tail_all, gains, rope)


def _chunk_mask(row0, rows, cols):
    r = row0 + lax.broadcasted_iota(jnp.int32, (rows, cols), 0)
    c = lax.broadcasted_iota(jnp.int32, (rows, cols), 1)
    return (c // CHUNK) <= (r // CHUNK)


def _diffattn_kernel(lam_ref, subln_ref, q_ref, k_ref, v_ref, o_ref, *, lambda_init, widths):
    tq = q_ref.shape[0]
    half = tq // 2
    scale = A_DQK ** -0.5
    step = pl.program_id(0)
    lp = lam_ref[...]
    lam = (jnp.exp(jnp.sum(lp[0:1] * lp[1:2], axis=-1, keepdims=True))
           - jnp.exp(jnp.sum(lp[2:3] * lp[3:4], axis=-1, keepdims=True)) + lambda_init)

    diag_mask = _chunk_mask(0, half, half)

    def body(W):
        extent = lambda r: W - (1 - r) * half

        def scores(m, r):
            s = lax.dot_general(q_ref[r * half:(r + 1) * half, m * A_DQK:(m + 1) * A_DQK],
                                k_ref[0:extent(r), m * A_DQK:(m + 1) * A_DQK],
                                NT_DIMS, preferred_element_type=F32)
            diag = jnp.where(diag_mask, s[:, extent(r) - half:], NEG)
            return diag if extent(r) == half else jnp.concatenate([s[:, :extent(r) - half], diag], axis=1)

        def numerator(s):
            e = jnp.exp2((s - jnp.max(s, axis=-1, keepdims=True)) * (scale * LOG2_E))
            return e.astype(BF16), jnp.sum(e, axis=-1, keepdims=True)

        ss = [[scores(m, r) for m in range(2)] for r in range(2)]
        outs = []
        for r in range(2):
            (e1, l1), (e2, l2) = numerator(ss[r][0]), numerator(ss[r][1])
            o1 = jnp.dot(e1, v_ref[0:extent(r), :], preferred_element_type=F32)
            o2 = jnp.dot(e2, v_ref[0:extent(r), :], preferred_element_type=F32)
            outs.append(o1 * (1.0 / l1) - o2 * (lam / l2))
        o = jnp.concatenate(outs, axis=0)
        o_ref[...] = (_rms(o, subln_ref[...]) * (1.0 - lambda_init)).astype(o_ref.dtype)

    for ci, W in enumerate(widths):
        pl.when(step == ci)(functools.partial(body, W))


def _diffattn(pb, a_lambda_l, a_subln_l, lambda_init, B, S):
    tq = min(512, S)
    nq = S // tq
    return pl.pallas_call(
        functools.partial(_diffattn_kernel, lambda_init=lambda_init,
                          widths=tuple(tq * (i + 1) for i in range(nq))),
        grid=(nq, B, A_HEADS),
        in_specs=[
            pl.BlockSpec((4, A_DQK), lambda i, b, h: (0, 0)),
            pl.BlockSpec((1, A_DV), lambda i, b, h: (0, 0)),
            pl.BlockSpec((tq, A_DV), lambda i, b, h: (b * nq + i, h)),
            pl.BlockSpec((S, A_DV), lambda i, b, h: (b, A_HEADS + h)),
            pl.BlockSpec((S, A_DV), lambda i, b, h: (b, 2 * A_HEADS + h)),
        ],
        out_specs=pl.BlockSpec((tq, A_DV), lambda i, b, h: (b * nq + i, h)),
        out_shape=jax.ShapeDtypeStruct((B * S, A_WIDTH), BF16),
        compiler_params=_cparams("parallel", "parallel", "arbitrary"),
        name="diffattn",
    )(a_lambda_l, a_subln_l.reshape(1, A_DV), pb, pb, pb)


def _gmlp_rows(u_ref, v_ref, ws_ref, bias_ref, o_ref):
    mask = _chunk_mask(0, B_BLOCK, B_BLOCK)
    for g in range(B_GROUPS):
        w = jnp.where(mask, ws_ref[g], 0.0).astype(BF16)
        cols = slice(g * B_CH, (g + 1) * B_CH)
        for n in range(u_ref.shape[0] // B_BLOCK):
            rows = slice(n * B_BLOCK, (n + 1) * B_BLOCK)
            z = jnp.dot(w, v_ref[rows, cols], preferred_element_type=F32) + bias_ref[:, g:g + 1]
            o_ref[rows, cols] = (u_ref[rows, cols].astype(F32) * z).astype(o_ref.dtype)


def _key_to_float(key):
    return lax.bitcast_convert_type(jnp.where(key < 0, key ^ jnp.int32(0x7FFFFFFF), key), F32)


def _dsa_kernel(q_ref, iq_ref, mq_ref, k_ref, v_ref, mk_ref, o_ref, isc_ref, p_ref, iqh_ref, iw_ref,
                *, topk, widths):
    tq = q_ref.shape[0]
    half = tq // 2
    S = k_ref.shape[0]
    scale = C_DH ** -0.5
    int_min = jnp.int32(-2 ** 31)
    step = pl.program_id(1)
    lane = lax.broadcasted_iota(jnp.int32, (half, LANES), 1)
    zero = jnp.zeros((half, LANES), jnp.int32)

    for h in range(IDX_HEADS):
        iqh_ref[h] = iq_ref[:, h * IDX_DH:(h + 1) * IDX_DH]
        iw_ref[h] = jnp.broadcast_to(
            mq_ref[:, IW_LANE0 + h:IW_LANE0 + h + 1].astype(F32), (tq, LANES))

    def body(W):
        subs = [(slice(0, half), W - half), (slice(half, tq), W)]

        def chunk(rows, c):
            return isc_ref[rows, c * LANES:(c + 1) * LANES]

        def count(rows, extent, pred):
            acc = jnp.zeros((half, LANES), F32)
            for c in range(extent // LANES):
                acc = acc + jnp.where(pred(chunk(rows, c), c), 1.0, 0.0)
            return jnp.broadcast_to(jnp.sum(acc, axis=-1, keepdims=True), acc.shape)

        isc_ref[:, 0:W] = jnp.zeros((tq, W), F32)

        def indexer_head(h, carry):
            for rows, extent in subs:
                d = lax.dot_general(iqh_ref[h, rows, :], mk_ref[0:extent, 0:IDX_DH], NT_DIMS,
                                    preferred_element_type=F32)
                w = iw_ref[h, rows, :]
                for c in range(extent // LANES):
                    isc_ref[rows, c * LANES:(c + 1) * LANES] = (
                        chunk(rows, c) + jnp.maximum(d[:, c * LANES:(c + 1) * LANES], 0.0) * w)
            return carry

        lax.fori_loop(0, IDX_HEADS, indexer_head, 0)
        for r, (rows, extent) in enumerate(subs):
            isc_ref[rows, 0:extent] = jnp.where(
                _chunk_mask(step * tq + r * half, half, extent), isc_ref[rows, 0:extent], NEG)

        def value_step(i, t_us):
            out = []
            for (rows, extent), t_u in zip(subs, t_us):
                cand_u = t_u | lax.shift_left(jnp.int32(1), jnp.int32(31) - i)
                cand_f = _key_to_float(cand_u ^ int_min)
                cnt = (count(rows, extent, lambda ch, c: ch >= cand_f)
                       + jnp.where(cand_f <= NEG, float(S - extent), 0.0))
                out.append(jnp.where(cnt >= topk, cand_u, t_u))
            return tuple(out)

        thrs = [_key_to_float(t_u ^ int_min) for t_u in lax.fori_loop(0, 32, value_step, (zero, zero))]
        needs = [topk - count(rows, extent, lambda ch, c: ch > thr)
                 for (rows, extent), thr in zip(subs, thrs)]
        n_ties = [count(rows, extent, lambda ch, c: ch == thr)
                  for (rows, extent), thr in zip(subs, thrs)]

        p_ref[...] = jnp.full((tq, LANES), S - 1, jnp.int32)
        surplus = jnp.concatenate(
            [jnp.where(n > need, 1.0, 0.0) for n, need in zip(n_ties, needs)], axis=0)

        @pl.when(jnp.max(surplus) > 0.0)
        def _():
            for (rows, extent), thr, need in zip(subs, thrs, needs):
                def index_step(i, p):
                    cand = p | lax.shift_right_logical(jnp.int32(S // 2), i)
                    ties_below = count(
                        rows, extent, lambda ch, c: (ch == thr) & (lane + c * LANES < cand))
                    return jnp.where(ties_below < need, cand, p)
                p_ref[rows, :] = lax.fori_loop(0, int(math.log2(S)), index_step, zero)

        for (rows, extent), thr in zip(subs, thrs):
            p_last = p_ref[rows, :]
            for c in range(extent // LANES):
                ch = chunk(rows, c)
                sel = ((ch > thr) | ((ch == thr) & (lane + c * LANES <= p_last))) & (ch > 0.5 * NEG)
                isc_ref[rows, c * LANES:(c + 1) * LANES] = jnp.where(sel, 0.0, NEG)

        def attention_head(h, carry):
            sl = pl.ds(pl.multiple_of(h * C_DH, C_DH), C_DH)
            for rows, extent in subs:
                s = lax.dot_general(q_ref[rows, sl], k_ref[0:extent, sl], NT_DIMS,
                                    preferred_element_type=F32)
                s = s + isc_ref[rows, 0:extent]
                e = jnp.exp2((s - jnp.max(s, axis=-1, keepdims=True)) * (scale * LOG2_E))
                o = jnp.dot(e.astype(BF16), v_ref[0:extent, sl], preferred_element_type=F32)
                o_ref[rows, sl] = (o * (1.0 / jnp.sum(e, axis=-1, keepdims=True))).astype(o_ref.dtype)
            return carry

        lax.fori_loop(0, C_HEADS, attention_head, 0)

    for ci, W in enumerate(widths):
        pl.when(step == ci)(functools.partial(body, W))


def _dsa(pb, B, S):
    tq = min(512, S)
    nq = S // tq
    topk = min(TOPK_MAX, S // 4)
    widths = tuple(tq * (i + 1) for i in range(nq))
    qspec = lambda t: pl.BlockSpec((tq, PROJ_TILE), lambda b, i: (b * nq + i, t))
    kspec = lambda t: pl.BlockSpec((S, PROJ_TILE), lambda b, i: (b, t))
    return pl.pallas_call(
        functools.partial(_dsa_kernel, topk=topk, widths=widths),
        grid=(B, nq),
        in_specs=[qspec(T_CQ), qspec(T_IQ), qspec(T_MISC), kspec(T_CK), kspec(T_CV), kspec(T_MISC)],
        out_specs=pl.BlockSpec((tq, C_WIDTH), lambda b, i: (b * nq + i, 0)),
        out_shape=jax.ShapeDtypeStruct((B * S, C_WIDTH), BF16),
        scratch_shapes=[pltpu.VMEM((tq, S), F32), pltpu.VMEM((tq, LANES), jnp.int32),
                        pltpu.VMEM((IDX_HEADS, tq, IDX_DH), BF16),
                        pltpu.VMEM((IDX_HEADS, tq, LANES), F32)],
        compiler_params=_cparams("parallel", "arbitrary"),
        name="dsa",
    )(pb, pb, pb, pb, pb, pb)


def _outproj_kernel(x_ref, mod_ref, oa_ref, u_ref, v_ref, ws_ref, bias_ref, oc_ref, w_ref, o_ref,
                    ob_ref):
    _gmlp_rows(u_ref, v_ref, ws_ref, bias_ref, ob_ref)
    y = jnp.dot(oa_ref[...], w_ref[0:A_WIDTH, :], preferred_element_type=F32)
    y = y + jnp.dot(ob_ref[...], w_ref[A_WIDTH:A_WIDTH + B_WIDTH, :], preferred_element_type=F32)
    y = y + jnp.dot(oc_ref[...], w_ref[A_WIDTH + B_WIDTH:, :], preferred_element_type=F32)
    o_ref[...] = x_ref[...] + mod_ref[0, 2:3, :] * y


def _outproj(x2, mod_l, oa, pb, ws_l, bias_l, oc, w, S):
    M, D = x2.shape
    tm = min(512, S)
    nrow = S // tm
    row = lambda width: pl.BlockSpec((tm, width), lambda i: (i, 0))
    return pl.pallas_call(
        _outproj_kernel,
        grid=(M // tm,),
        in_specs=[
            row(D),
            pl.BlockSpec((1, 6, D), lambda i: (i // nrow, 0, 0)),
            row(A_WIDTH),
            pl.BlockSpec((tm, B_WIDTH), lambda i: (i, T_BU)),
            pl.BlockSpec((tm, B_WIDTH), lambda i: (i, T_BV)),
            pl.BlockSpec((B_GROUPS, B_BLOCK, B_BLOCK), lambda i: (0, 0, 0)),
            pl.BlockSpec((B_BLOCK, B_GROUPS), lambda i: (0, 0)),
            row(C_WIDTH),
            pl.BlockSpec((D, D), lambda i: (0, 0)),
        ],
        out_specs=row(D),
        out_shape=jax.ShapeDtypeStruct((M, D), F32),
        scratch_shapes=[pltpu.VMEM((tm, B_WIDTH), BF16)],
        compiler_params=_cparams("parallel"),
        name="outproj",
    )(x2, mod_l, oa, pb, pb, ws_l, bias_l.T, oc, w)


def _ffn_kernel(x_ref, mod_ref, w1_ref, w2_ref, *rest, convert_next):
    if convert_next:
        n = (len(rest) - 3) // 2
        src, o_ref, dst, (h_ref, acc_ref) = rest[:n], rest[n], rest[n + 1:2 * n + 1], rest[2 * n + 1:]
        for s_ref, d_ref in zip(src, dst):
            d_ref[...] = s_ref[...].astype(BF16)
    else:
        o_ref, h_ref, acc_ref = rest
    j = pl.program_id(1)

    @pl.when(j == 0)
    def _():
        y = _rms(x_ref[...])
        h_ref[...] = (y * (1.0 + mod_ref[0, 4:5, :]) + mod_ref[0, 3:4, :]).astype(BF16)
        acc_ref[...] = jnp.zeros_like(acc_ref)

    hid = jnp.square(jnp.maximum(jnp.dot(h_ref[...], w1_ref[...], preferred_element_type=F32), 0.0))
    acc_ref[...] += jnp.dot(hid.astype(BF16), w2_ref[...], preferred_element_type=F32)

    @pl.when(j == pl.num_programs(1) - 1)
    def _():
        o_ref[...] = x_ref[...] + mod_ref[0, 5:6, :] * acc_ref[...]


def _ffn(x2, mod_l, w1, w2, S, next_weights=None):
    M, D = x2.shape
    F = w1.shape[-1]
    tm = min(512, S)
    tf = 1024
    nrow = S // tm
    nsteps = (M // tm) * (F // tf)
    in_specs = [
        pl.BlockSpec((tm, D), lambda i, j: (i, 0)),
        pl.BlockSpec((1, 6, D), lambda i, j: (i // nrow, 0, 0)),
        pl.BlockSpec((D, tf), lambda i, j: (0, j)),
        pl.BlockSpec((tf, D), lambda i, j: (j, 0)),
    ]
    out_specs = [pl.BlockSpec((tm, D), lambda i, j: (i, 0))]
    out_shape = [jax.ShapeDtypeStruct((M, D), F32)]
    operands = [x2, mod_l, w1, w2]
    if next_weights is not None:
        stacked, layer = next_weights
        for w in stacked:
            rows, cols = w.shape[1:]
            slab = rows // nsteps
            assert slab * nsteps == rows and slab % 16 == 0
            in_specs.append(pl.BlockSpec(
                (None, slab, cols), lambda i, j: (layer, i * (F // tf) + j, 0)))
            out_specs.append(pl.BlockSpec((slab, cols), lambda i, j: (i * (F // tf) + j, 0)))
            out_shape.append(jax.ShapeDtypeStruct((rows, cols), BF16))
            operands.append(w)
    outs = pl.pallas_call(
        functools.partial(_ffn_kernel, convert_next=next_weights is not None),
        grid=(M // tm, F // tf),
        in_specs=in_specs,
        out_specs=out_specs,
        out_shape=out_shape,
        scratch_shapes=[pltpu.VMEM((tm, D), BF16), pltpu.VMEM((tm, D), F32)],
        compiler_params=_cparams("arbitrary", "arbitrary"),
        name="ffn",
    )(*operands)
    return outs[0], tuple(outs[1:])


def _rope_tables(S):
    pos = jnp.arange(S, dtype=F32)[:, None]

    def angles(dim):
        inv = 1.0 / (ROPE_THETA ** (jnp.arange(0, dim, 2, dtype=F32) / dim))
        ang = pos * inv[None, :]
        return jnp.concatenate([ang, ang], axis=-1)

    a128 = angles(A_DQK)
    half = jnp.arange(LANES) < LANES // 2
    cos128, sin128 = jnp.cos(a128), jnp.sin(a128)
    a64 = jnp.tile(angles(IDX_DH), (1, 2))
    cos64, sin64 = jnp.cos(a64), jnp.sin(a64)
    low = (jnp.arange(LANES) % IDX_DH) < IDX_DH // 2
    return jnp.stack([
        cos128, jnp.where(half, -sin128, sin128),
        cos64, jnp.where(low, -sin64, 0.0), jnp.where(low, 0.0, sin64),
    ])


def _gain_rows(a_qnorm_l, a_knorm_l, b_vnorm_l, c_qnorm_l, c_knorm_l):
    rep = PROJ_TILE // LANES
    one = jnp.ones((PROJ_TILE,), F32)
    rows = [one] * N_PROJ_TILES
    rows[T_AQ] = rows[T_AQ + 1] = jnp.tile(a_qnorm_l, rep)
    rows[T_AK] = rows[T_AK + 1] = jnp.tile(a_knorm_l, rep)
    rows[T_BV] = b_vnorm_l
    rows[T_CQ] = jnp.tile(c_qnorm_l, rep)
    rows[T_CK] = jnp.tile(c_knorm_l, rep)
    return jnp.stack(rows)[:, None, :]


def kernel(x, c, w_mod, b_mod, w_in, w_out, a_qnorm, a_knorm, a_lambda, a_subln, b_vnorm, b_ws,
           b_bias, c_qnorm, c_knorm, w_ff1, w_ff2):
    B, S, D = x.shape
    L = w_mod.shape[0]
    assert D == D_MODEL and S % Q_BLOCK == 0 and w_in.shape[-1] == N_IN

    rope = _rope_tables(S)
    n_main = T_MISC * PROJ_TILE
    w_in_b = w_in[:, :, :n_main].astype(BF16)
    w_in_tail_b = jnp.pad(w_in[:, :, n_main:], ((0, 0), (0, 0), (0, N_PROJ - N_IN))).astype(BF16)
    weights_b = (w_out[0].astype(BF16), w_ff1[0].astype(BF16), w_ff2[0].astype(BF16))

    mod = _modulation(c, w_mod, b_mod).reshape(L, B, 6, D)
    x2 = x.reshape(B * S, D)
    for l in range(L):
        w_out_b, w_ff1_b, w_ff2_b = weights_b
        lambda_init = 0.8 - 0.6 * math.exp(-0.3 * l)
        gains = _gain_rows(a_qnorm[l], a_knorm[l], b_vnorm[l], c_qnorm[l], c_knorm[l])
        pb = _inproj(x2, mod[l], w_in_b, w_in_tail_b, l, gains, rope, S)
        oa = _diffattn(pb, a_lambda[l], a_subln[l], lambda_init, B, S)
        oc = _dsa(pb, B, S)
        x2 = _outproj(x2, mod[l], oa, pb, b_ws[l], b_bias[l], oc, w_out_b, S)
        next_weights = ((w_out, w_ff1, w_ff2), l + 1) if l + 1 < L else None
        x2, weights_b = _ffn(x2, mod[l], w_ff1_b, w_ff2_b, S, next_weights)
    return x2.reshape(B, S, D)
```
